```python
import math
import jax
import jax.numpy as jnp
from jax import lax
import numpy as np

D_MODEL = 1024
BATCH = 16
SEQ = 2048
DEPTH = 4
DEC_BATCH = 8
DEC_SEQ = 16
PAST_LEN = 1024

CHUNK = 64
QBLOCK = 128
N_GROUPS = 4
HEAD_DIM = 64
GW = D_MODEL // N_GROUPS
N_HEADS = GW // HEAD_DIM
D_MIX = N_GROUPS * GW
GDN_CONV = 4
BAND_CHUNKS = 8
BAND_ROWS = BAND_CHUNKS * CHUNK
REL_CLIP = 2 * CHUNK
D_FF = 2816
FFN_CONV = 3
EPS = 1e-6

A_COLS = 4 * GW + 2 * N_HEADS
C_COLS = 3 * GW + N_HEADS
OFF_A = 0
OFF_B = OFF_A + A_COLS
OFF_C = OFF_B + 3 * GW
OFF_D = OFF_C + C_COLS
IN_COLS = OFF_D + 3 * GW

STATE_KEYS = ('gdn_conv', 'gdn_state', 'sb_k', 'sb_v', 'fox_k', 'fox_v', 'fox_logf', 'band_k', 'band_v', 'ffn_conv')

kernel_name = 'hybrid_streaming_encoder_step'

F32 = jnp.float32


def rms_norm(x, g):
    x32 = x.astype(F32)
    y = x32 * lax.rsqrt(jnp.mean(x32 * x32, axis=-1, keepdims=True) + EPS)
    return (y * g.astype(F32)).astype(x.dtype)


def l2_normalize(x):
    return x * lax.rsqrt(jnp.sum(x * x, axis=-1, keepdims=True) + EPS)


def ada_modulation(c, w, b):
    m = jax.nn.silu(c) @ w + b
    return jnp.split(m[:, None, :], 6, axis=-1)


def causal_depthwise_conv(x, w, state):
    width = w.shape[0]
    T = x.shape[1]
    if state is None:
        state = jnp.zeros((x.shape[0], width - 1, x.shape[2]), x.dtype)
    xp = jnp.concatenate([state.astype(x.dtype), x], axis=1)
    y = sum(w[i] * xp[:, i:i + T] for i in range(width))
    return y, xp[:, -(width - 1):]


def split_qkv(a):
    B, T, _ = a.shape
    a = a.reshape(B, T, 3, N_HEADS, HEAD_DIM)
    return a[:, :, 0], a[:, :, 1], a[:, :, 2]


def sweep_query_blocks(fn, q_arrays, q_pos):
    B, Tq = q_arrays[0].shape[:2]
    blk = QBLOCK if Tq % QBLOCK == 0 else Tq
    nb = Tq // blk
    split = lambda a: jnp.moveaxis(a.reshape(B, nb, blk, *a.shape[2:]), 1, 0)
    out = lax.map(fn, (tuple(split(a) for a in q_arrays), q_pos.reshape(nb, blk)))
    return jnp.moveaxis(out, 0, 1).reshape(B, Tq, *out.shape[3:])


def gated_delta_chunked(q, k, v, g, beta, s0):
    B, T, H, _ = q.shape
    L = min(CHUNK, T)
    n = T // L

    def chunks(a):
        a = a.reshape(B, n, L, *a.shape[2:])
        return jnp.moveaxis(jnp.moveaxis(a, 1, 0), 3, 2)

    idx = jnp.arange(L)
    incl = idx[:, None] >= idx[None, :]
    strict = idx[:, None] > idx[None, :]
    eye = jnp.eye(L, dtype=F32)

    def step(S, inp):
        qc, kc, vc, gc, bc = inp
        G = jnp.cumsum(gc, axis=-1)
        decay = jnp.exp(jnp.where(incl, G[..., :, None] - G[..., None, :], -jnp.inf))
        kk = jnp.einsum('bhid,bhjd->bhij', kc, kc)
        m = eye + jnp.where(strict, bc[..., :, None] * kk * decay, 0.0)
        u = lax.linalg.triangular_solve(m, vc * bc[..., None], left_side=True, lower=True)
        wk = lax.linalg.triangular_solve(m, kc * (bc * jnp.exp(G))[..., None], left_side=True, lower=True)
        v_new = u - jnp.einsum('bhik,bhkv->bhiv', wk, S)
        qk = jnp.einsum('bhid,bhjd->bhij', qc, kc) * decay
        o = (jnp.einsum('bhik,bhkv->bhiv', qc * jnp.exp(G)[..., None], S)
             + jnp.einsum('bhij,bhjv->bhiv', qk, v_new))
        g_last = G[..., -1:]
        S = (jnp.exp(g_last)[..., None] * S
             + jnp.einsum('bhik,bhiv->bhkv', kc * jnp.exp(g_last - G)[..., None], v_new))
        return S, o

    S, o = lax.scan(step, s0, (chunks(q), chunks(k), chunks(v), chunks(g), chunks(beta)))
    o = jnp.swapaxes(jnp.moveaxis(o, 0, 1), 2, 3).reshape(B, T, H, -1)
    return o, S


def gdn_mixer(cols, conv_w, a_log, dt_bias, norm_g, conv_state, rec_state):
    B, T, _ = cols.shape
    qkv, conv_new = causal_depthwise_conv(cols[..., :3 * GW], conv_w, conv_state)
    qkv = jax.nn.silu(qkv.astype(F32)).reshape(B, T, 3, N_HEADS, HEAD_DIM)
    q = l2_normalize(qkv[:, :, 0]) * HEAD_DIM ** -0.5
    k = l2_normalize(qkv[:, :, 1])
    v = qkv[:, :, 2]
    z = cols[..., 3 * GW:4 * GW].astype(F32).reshape(B, T, N_HEADS, HEAD_DIM)
    beta = jax.nn.sigmoid(cols[..., 4 * GW:4 * GW + N_HEADS].astype(F32))
    a_in = cols[..., 4 * GW + N_HEADS:A_COLS].astype(F32)
    g = -jnp.exp(a_log.astype(F32)) * jax.nn.softplus(a_in + dt_bias.astype(F32))
    if rec_state is None:
        rec_state = jnp.zeros((B, N_HEADS, HEAD_DIM, HEAD_DIM), F32)
    o, S = gated_delta_chunked(q, k, v, g, beta, rec_state.astype(F32))
    o = rms_norm(o, norm_g) * jax.nn.silu(z)
    return o.reshape(B, T, GW).astype(cols.dtype), conv_new, S.astype(cols.dtype)


def stick_breaking_attend(q, k, v, q_pos, k_pos):
    scale = HEAD_DIM ** -0.5
    k32, v32 = k.astype(F32), v.astype(F32)

    def block(args):
        (qb,), pb = args
        z = jnp.einsum('bqhd,bkhd->bhqk', qb.astype(F32), k32) * scale
        mask = k_pos[None, :] < pb[:, None]
        log_rest = jnp.where(mask, jax.nn.log_sigmoid(-z), 0.0)
        log_w = jax.nn.log_sigmoid(z) + lax.cumsum(log_rest, axis=3, reverse=True) - log_rest
        w = jnp.where(mask, jnp.exp(log_w), 0.0)
        return jnp.einsum('bhqk,bkhd->bqhd', w, v32)

    return sweep_query_blocks(block, (q,), q_pos)


def forgetting_attend(q, k, v, f_q, f_k, q_pos, k_pos):
    scale = HEAD_DIM ** -0.5
    k32, v32 = k.astype(F32), v.astype(F32)
    fk = jnp.swapaxes(f_k, 1, 2)[:, :, None, :]

    def block(args):
        (qb, fb), pb = args
        s = (jnp.einsum('bqhd,bkhd->bhqk', qb.astype(F32), k32) * scale
             + jnp.swapaxes(fb, 1, 2)[..., :, None] - fk)
        mask = k_pos[None, :] <= pb[:, None]
        p = jax.nn.softmax(jnp.where(mask, s, -jnp.inf), axis=-1)
        return jnp.einsum('bhqk,bkhd->bqhd', p, v32)

    return sweep_query_blocks(block, (q, f_q), q_pos)


def band_attend(q, k, v, q_pos, k_pos, valid, rel_table):
    rel = jnp.clip(q_pos[:, :, None] - k_pos[:, None, :], -REL_CLIP, REL_CLIP) + REL_CLIP
    bias = jnp.moveaxis(rel_table.astype(F32)[:, rel], 0, 1)
    s = jnp.einsum('bcqhd,bckhd->bchqk', q.astype(F32), k.astype(F32)) * HEAD_DIM ** -0.5 + bias
    s = jnp.where(valid[None, :, None, None, :], s, -jnp.inf)
    p = jax.nn.softmax(s, axis=-1)
    return jnp.einsum('bchqk,bckhd->bcqhd', p, v.astype(F32))


def trunk_layer(x, c, lp, cache):
    B, T, _ = x.shape
    past = cache is not None
    pos0 = cache['sb_k'].shape[1] if past else 0
    q_pos = pos0 + jnp.arange(T)
    k_pos = jnp.arange(pos0 + T)
    sh1, sc1, ga1, sh2, sc2, ga2 = ada_modulation(c, lp['ada_w'], lp['ada_b'])
    h = rms_norm(x, lp['norm_mix_g']) * (1 + sc1) + sh1
    cols = h @ lp['w_in']
    new = {}

    oa, new['gdn_conv'], new['gdn_state'] = gdn_mixer(
        cols[..., OFF_A:OFF_A + A_COLS], lp['gdn_conv_w'], lp['gdn_a_log'], lp['gdn_dt_bias'],
        lp['gdn_norm_g'], cache['gdn_conv'] if past else None, cache['gdn_state'] if past else None)

    qb, kb, vb = split_qkv(cols[..., OFF_B:OFF_B + 3 * GW])
    new['sb_k'], new['sb_v'] = kb, vb
    if past:
        kb = jnp.concatenate([cache['sb_k'].astype(kb.dtype), kb], axis=1)
        vb = jnp.concatenate([cache['sb_v'].astype(vb.dtype), vb], axis=1)
    ob = stick_breaking_attend(qb, kb, vb, q_pos, k_pos)

    qc, kc, vc = split_qkv(cols[..., OFF_C:OFF_C + 3 * GW])
    qc = rms_norm(qc, lp['fox_q_g'])
    kc = rms_norm(kc, lp['fox_k_g'])
    logf = jax.nn.log_sigmoid(cols[..., OFF_C + 3 * GW:OFF_C + C_COLS].astype(F32)
                              + lp['fox_b_f'].astype(F32))
    new['fox_k'], new['fox_v'], new['fox_logf'] = kc, vc, logf.astype(x.dtype)
    if past:
        kc = jnp.concatenate([cache['fox_k'].astype(kc.dtype), kc], axis=1)
        vc = jnp.concatenate([cache['fox_v'].astype(vc.dtype), vc], axis=1)
        logf = jnp.concatenate([cache['fox_logf'].astype(F32), logf], axis=1)
    f_cum = jnp.cumsum(logf, axis=1)
    oc = forgetting_attend(qc, kc, vc, f_cum[:, -T:], f_cum, q_pos, k_pos)

    qd, kd, vd = split_qkv(cols[..., OFF_D:OFF_D + 3 * GW])
    qd = rms_norm(qd, lp['band_q_g'])
    kd = rms_norm(kd, lp['band_k_g'])
    if past:
        R = cache['band_k'].shape[1]
        kd = jnp.concatenate([cache['band_k'].astype(kd.dtype), kd], axis=1)
        vd = jnp.concatenate([cache['band_v'].astype(vd.dtype), vd], axis=1)
        band_kpos = (pos0 - R + jnp.arange(R + T))[None]
        od = band_attend(qd[:, None], kd[:, None], vd[:, None], q_pos[None], band_kpos,
                         jnp.ones((1, R + T), bool), lp['band_rel_bias'])
        new['band_k'], new['band_v'] = kd[:, -R:], vd[:, -R:]
    else:
        n_chunks = T // CHUNK
        idx = jnp.arange(n_chunks)[:, None] * CHUNK + jnp.arange(BAND_ROWS + CHUNK)[None, :]
        pad = ((0, 0), (BAND_ROWS, 0), (0, 0), (0, 0))
        band_kpos = idx - BAND_ROWS
        od = band_attend(qd.reshape(B, n_chunks, CHUNK, N_HEADS, HEAD_DIM),
                         jnp.pad(kd, pad)[:, idx], jnp.pad(vd, pad)[:, idx],
                         q_pos.reshape(n_chunks, CHUNK), band_kpos, band_kpos >= 0,
                         lp['band_rel_bias'])
        keep = min(BAND_ROWS, T)
        new['band_k'], new['band_v'] = kd[:, -keep:], vd[:, -keep:]
    od = od.reshape(B, T, N_HEADS, HEAD_DIM)

    mg = lp['merge_g']
    mixed = jnp.concatenate(
        [oa.astype(x.dtype)]
        + [rms_norm(o, mg[i]).reshape(B, T, GW).astype(x.dtype) for i, o in enumerate((ob, oc, od))],
        axis=-1)
    x = x + ga1 * (mixed @ lp['w_o'])

    h = rms_norm(x, lp['norm_ffn_g']) * (1 + sc2) + sh2
    u, new['ffn_conv'] = causal_depthwise_conv(h @ lp['w_up'], lp['ffn_conv_w'],
                                               cache['ffn_conv'] if past else None)
    a, b = jnp.split(u, 2, axis=-1)
    x = x + ga2 * ((jax.nn.silu(a) * b) @ lp['w_down'])
    return x, new


def setup_inputs(seed: int = 0) -> dict:
    key = jax.random.key(seed)
    ks = iter(jax.random.split(key, 48))

    def nrm(shape, scale=1.0):
        return jax.random.normal(next(ks), shape, F32) * scale

    def gain(shape):
        return 1.0 + nrm(shape, 0.05)

    band_past = min(BAND_ROWS, PAST_LEN)
    dt = jnp.exp(jax.random.uniform(next(ks), (DEPTH, N_HEADS), F32, math.log(1e-3), math.log(1e-1)))
    a_init = jax.random.uniform(next(ks), (DEPTH, N_HEADS), F32, 1.0, 16.0)
    return {
        'x_prompt': nrm((BATCH, SEQ, D_MODEL)),
        'x_sample': nrm((DEC_BATCH, DEC_SEQ, D_MODEL)),
        'c_prompt': nrm((BATCH, D_MODEL)),
        'c_sample': nrm((DEC_BATCH, D_MODEL)),
        'state_gdn_conv': nrm((DEPTH, DEC_BATCH, GDN_CONV - 1, 3 * GW)),
        'state_gdn': nrm((DEPTH, DEC_BATCH, N_HEADS, HEAD_DIM, HEAD_DIM), 0.2),
        'cache_sb_k': nrm((DEPTH, DEC_BATCH, PAST_LEN, N_HEADS, HEAD_DIM)),
        'cache_sb_v': nrm((DEPTH, DEC_BATCH, PAST_LEN, N_HEADS, HEAD_DIM)),
        'cache_fox_k': nrm((DEPTH, DEC_BATCH, PAST_LEN, N_HEADS, HEAD_DIM)),
        'cache_fox_v': nrm((DEPTH, DEC_BATCH, PAST_LEN, N_HEADS, HEAD_DIM)),
        'cache_fox_logf': jax.nn.log_sigmoid(nrm((DEPTH, DEC_BATCH, PAST_LEN, N_HEADS)) + 2.0),
        'cache_band_k': nrm((DEPTH, DEC_BATCH, band_past, N_HEADS, HEAD_DIM)),
        'cache_band_v': nrm((DEPTH, DEC_BATCH, band_past, N_HEADS, HEAD_DIM)),
        'state_ffn_conv': nrm((DEPTH, DEC_BATCH, FFN_CONV - 1, 2 * D_FF)),
        'ada_w': nrm((DEPTH, D_MODEL, 6 * D_MODEL), 0.5 * D_MODEL ** -0.5),
        'ada_b': nrm((DEPTH, 6 * D_MODEL), 0.02),
        'norm_mix_g': gain((DEPTH, D_MODEL)),
        'w_in': nrm((DEPTH, D_MODEL, IN_COLS), D_MODEL ** -0.5),
        'gdn_conv_w': nrm((DEPTH, GDN_CONV, 3 * GW), GDN_CONV ** -0.5),
        'gdn_a_log': jnp.log(a_init),
        'gdn_dt_bias': dt + jnp.log(-jnp.expm1(-dt)),
        'gdn_norm_g': gain((DEPTH, HEAD_DIM)),
        'fox_q_g': gain((DEPTH, HEAD_DIM)),
        'fox_k_g': gain((DEPTH, HEAD_DIM)),
        'fox_b_f': 2.0 + nrm((DEPTH, N_HEADS), 0.5),
        'band_q_g': gain((DEPTH, HEAD_DIM)),
        'band_k_g': gain((DEPTH, HEAD_DIM)),
        'band_rel_bias': nrm((DEPTH, N_HEADS, 2 * REL_CLIP + 1), 0.2),
        'merge_g': gain((DEPTH, N_GROUPS - 1, N_HEADS, HEAD_DIM)),
        'w_o': nrm((DEPTH, D_MIX, D_MODEL), D_MIX ** -0.5),
        'norm_ffn_g': gain((DEPTH, D_MODEL)),
        'w_up': nrm((DEPTH, D_MODEL, 2 * D_FF), D_MODEL ** -0.5),
        'ffn_conv_w': nrm((DEPTH, FFN_CONV, 2 * D_FF), FFN_CONV ** -0.5),
        'w_down': nrm((DEPTH, D_FF, D_MODEL), D_FF ** -0.5),
    }


def reference(x_prompt, x_sample, c_prompt, c_sample, state_gdn_conv, state_gdn,
              cache_sb_k, cache_sb_v, cache_fox_k, cache_fox_v, cache_fox_logf,
              cache_band_k, cache_band_v, state_ffn_conv, ada_w, ada_b, norm_mix_g, w_in,
              gdn_conv_w, gdn_a_log, gdn_dt_bias, gdn_norm_g, fox_q_g, fox_k_g, fox_b_f,
              band_q_g, band_k_g, band_rel_bias, merge_g, w_o, norm_ffn_g, w_up, ffn_conv_w,
              w_down):
    new_p = {n: [] for n in STATE_KEYS}
    new_s = {n: [] for n in STATE_KEYS}
    y_p, y_s = x_prompt, x_sample
    for l in range(DEPTH):
        lp = {'ada_w': ada_w[l], 'ada_b': ada_b[l], 'norm_mix_g': norm_mix_g[l], 'w_in': w_in[l],
              'gdn_conv_w': gdn_conv_w[l], 'gdn_a_log': gdn_a_log[l], 'gdn_dt_bias': gdn_dt_bias[l],
              'gdn_norm_g': gdn_norm_g[l], 'fox_q_g': fox_q_g[l], 'fox_k_g': fox_k_g[l],
              'fox_b_f': fox_b_f[l], 'band_q_g': band_q_g[l], 'band_k_g': band_k_g[l],
              'band_rel_bias': band_rel_bias[l], 'merge_g': merge_g[l], 'w_o': w_o[l],
              'norm_ffn_g': norm_ffn_g[l], 'w_up': w_up[l], 'ffn_conv_w': ffn_conv_w[l],
              'w_down': w_down[l]}
        cache = {'gdn_conv': state_gdn_conv[l], 'gdn_state': state_gdn[l],
                 'sb_k': cache_sb_k[l], 'sb_v': cache_sb_v[l],
                 'fox_k': cache_fox_k[l], 'fox_v': cache_fox_v[l], 'fox_logf': cache_fox_logf[l],
                 'band_k': cache_band_k[l], 'band_v': cache_band_v[l], 'ffn_conv': state_ffn_conv[l]}
        y_p, st_p = trunk_layer(y_p, c_prompt, lp, None)
        y_s, st_s = trunk_layer(y_s, c_sample, lp, cache)
        for n in STATE_KEYS:
            new_p[n].append(st_p[n])
            new_s[n].append(st_s[n])
    return (y_p, y_s,
            jnp.stack(new_p['gdn_conv']), jnp.stack(new_p['gdn_state']),
            jnp.stack(new_p['sb_k']), jnp.stack(new_p['sb_v']),
            jnp.stack(new_p['fox_k']), jnp.stack(new_p['fox_v']), jnp.stack(new_p['fox_logf']),
            jnp.stack(new_p['band_k']), jnp.stack(new_p['band_v']), jnp.stack(new_p['ffn_conv']),
            jnp.stack(new_s['gdn_conv']), jnp.stack(new_s['gdn_state']),
            jnp.stack(new_s['sb_k']), jnp.stack(new_s['sb_v']),
            jnp.stack(new_s['fox_k']), jnp.stack(new_s['fox_v']), jnp.stack(new_s['fox_logf']),
            jnp.stack(new_s['band_k']), jnp.stack(new_s['band_v']), jnp.stack(new_s['ffn_conv']))
```

```python
import functools

import numpy as np
import jax
import jax.numpy as jnp
from jax import lax
from jax.experimental import pallas as pl
from jax.experimental.pallas import tpu as pltpu

F32 = jnp.float32
BF16 = jnp.bfloat16

HEAD_DIM = 64
N_HEADS = 4
GW = N_HEADS * HEAD_DIM
CHUNK = 64
BAND_ROWS = 512
REL_CLIP = 2 * CHUNK
EPS = 1e-6
NEG = -1e30
VMEM_LIMIT = 56 * 1024 * 1024
BAND_WIN = BAND_ROWS + 128


def _bf(x):
    return x.astype(BF16)


def _dot(a, b):
    return jnp.dot(_bf(a), _bf(b), preferred_element_type=F32)


def _dot_nt(a, b):
    return lax.dot_general(_bf(a), _bf(b), (((1,), (1,)), ((), ())), preferred_element_type=F32)


def _split(x, n):
    parts = []
    for _ in range(n - 1):
        hi = _bf(x)
        parts.append(hi)
        x = x - hi.astype(F32)
    parts.append(_bf(x))
    return parts


def _dot3(a, b):
    ah, al = _split(a, 2)
    bh, bl = _split(b, 2)
    d = lambda x, y: jnp.dot(x, y, preferred_element_type=F32)
    return d(ah, bh) + d(ah, bl) + d(al, bh)


def _dot_exact_rhs(x, m, n):
    return sum(jnp.dot(p, m, preferred_element_type=F32) for p in _split(x, n))


def _dot_exact_lhs(m, x, n):
    return sum(jnp.dot(m, p, preferred_element_type=F32) for p in _split(x, n))


def _sigmoid(x):
    return 1.0 / (1.0 + jnp.exp(-x))


def _silu(x):
    return x * _sigmoid(x)


def _softplus(x):
    return jnp.maximum(x, 0.0) + jnp.log(1.0 + jnp.exp(-jnp.abs(x)))


def _head_rms(o, bd, gain):
    ss = _dot_exact_rhs(o * o, bd, 2)
    return o * lax.rsqrt(ss * (1.0 / HEAD_DIM) + EPS) * gain


def _head_masks(width=GW):
    lane = lax.broadcasted_iota(jnp.int32, (1, width), 1)
    return [((lane & (GW - 1)) >> 6) == h for h in range(N_HEADS)]


def _params(sem, **kw):
    return pltpu.CompilerParams(dimension_semantics=sem, vmem_limit_bytes=VMEM_LIMIT, **kw)


def _np_bd(n, blk):
    i = np.arange(n)
    return (i[:, None] // blk == i[None, :] // blk)


def _const_bd256():
    return jnp.asarray(_np_bd(GW, HEAD_DIM), BF16)


def _const_tri_fwd(n):
    i = np.arange(n)
    return jnp.asarray(i[:, None] <= i[None, :], BF16)


def _const_tri_rev(n):
    i = np.arange(n)
    return jnp.asarray(i[:, None] >= i[None, :], BF16)


def _ada_kernel(c_ref, w_ref, b_ref, o_ref):
    s = _silu(c_ref[...])
    o_ref[0] = _dot(s, w_ref[0]) + b_ref[0]


def _ada_call(c_all, ada_w, ada_b):
    depth, d, n = ada_w.shape
    rows = c_all.shape[0]
    tn = 1536
    return pl.pallas_call(
        _ada_kernel,
        grid=(depth, n // tn),
        in_specs=[pl.BlockSpec((rows, d), lambda l, j: (0, 0)),
                  pl.BlockSpec((1, d, tn), lambda l, j: (l, 0, j)),
                  pl.BlockSpec((1, 1, tn), lambda l, j: (l, 0, j))],
        out_specs=pl.BlockSpec((1, rows, tn), lambda l, j: (l, 0, j)),
        out_shape=jax.ShapeDtypeStruct((depth, rows, n), F32),
        compiler_params=_params(("arbitrary", "arbitrary")),
        name="ada_modulation",
    )(c_all, ada_w, ada_b.reshape(depth, 1, n))


IN_GROUPS = 14
IN_COLS_PACKED = IN_GROUPS * GW


def _in_kernel(x_ref, sc_ref, sh_ref, g_ref, w_ref, gains_ref, bf_ref, bd_ref,
               gqkv_ref, gz_ref, small_ref, sbq_ref, sbk_ref, sbv_ref,
               fq_ref, fk_ref, fv_ref, logf_ref, bq_ref, bk_ref, bv_ref):
    x = x_ref[...]
    ms = jnp.mean(x * x, axis=-1, keepdims=True)
    h = x * lax.rsqrt(ms + EPS) * g_ref[...]
    h = _bf(h * (1.0 + sc_ref[0]) + sh_ref[0])
    bd = bd_ref[...]
    scale = HEAD_DIM ** -0.5

    def col(j, n=1):
        return jnp.dot(h, w_ref[:, j * GW:(j + n) * GW], preferred_element_type=F32)

    gqkv_ref[...] = col(0, 3)
    gz_ref[...] = col(3)
    sbq_ref[...] = _bf(col(4) * scale)
    sbk_ref[...] = col(5)
    sbv_ref[...] = col(6)
    fq_ref[...] = _bf(_head_rms(col(7), bd, gains_ref[0:1, :]) * scale)
    fk_ref[...] = _head_rms(col(8), bd, gains_ref[1:2, :])
    fv_ref[...] = col(9)
    bq_ref[...] = _bf(_head_rms(col(10), bd, gains_ref[2:3, :]) * scale)
    bk_ref[...] = _head_rms(col(11), bd, gains_ref[3:4, :])
    bv_ref[...] = col(12)
    sm = col(13)
    small_ref[...] = sm[:, :128]
    lf = sm[:, 128:] + bf_ref[...]
    logf = jnp.minimum(lf, 0.0) - jnp.log(1.0 + jnp.exp(-jnp.abs(lf)))
    logf_ref[...] = logf[:, :N_HEADS]


def _in_call(x2, sc, sh, g, w, gains, bfv, bd, tm, rows_per_mod):
    m, d = x2.shape
    r = sc.shape[1]
    tiles_per_mod = rows_per_mod // tm
    mod_spec = pl.BlockSpec((1, r, d), lambda i: (i // tiles_per_mod, 0, 0))
    full = lambda a: pl.BlockSpec(a.shape, lambda i: (0,) * a.ndim)
    row = lambda n: pl.BlockSpec((tm, n), lambda i: (i, 0))
    out_defs = [(3 * GW, F32), (GW, F32), (128, F32), (GW, BF16), (GW, F32), (GW, F32),
                (GW, BF16), (GW, F32), (GW, F32), (N_HEADS, F32), (GW, BF16), (GW, F32), (GW, F32)]
    return pl.pallas_call(
        _in_kernel,
        grid=(m // tm,),
        in_specs=[row(d), mod_spec, mod_spec, full(g), full(w), full(gains), full(bfv), full(bd)],
        out_specs=[row(n) for n, _ in out_defs],
        out_shape=[jax.ShapeDtypeStruct((m, n), dt) for n, dt in out_defs],
        compiler_params=_params(("arbitrary",)),
        name="in_projection",
    )(x2, sc, sh, g, w, gains, bfv, bd)


def _cumsum_kernel(x_ref, u_ref, o_ref):
    r, tk = x_ref.shape
    carry = jnp.zeros((r, 1), F32)
    for blk in range(tk // 256):
        o = _dot_exact_rhs(x_ref[:, blk * 256:(blk + 1) * 256], u_ref[...], 3) + carry
        o_ref[:, blk * 256:(blk + 1) * 256] = o
        carry = o[:, 255:256]


def _cumsum_call(x):
    return pl.pallas_call(
        _cumsum_kernel,
        out_shape=jax.ShapeDtypeStruct(x.shape, F32),
        compiler_params=pltpu.CompilerParams(vmem_limit_bytes=VMEM_LIMIT),
        name="logf_cumsum",
    )(x, _const_tri_fwd(256))


def _fill_kv(k_ref, v_ref, kb_ref, vm_ref, hmask):
    tk_all = k_ref.shape[1]
    step = min(256, tk_all)
    for r in range(0, tk_all, step):
        kb_ref[r:r + step, :] = _bf(k_ref[0, r:r + step, :])
        vv = v_ref[0, r:r + step, :]
        for h in range(N_HEADS):
            vm_ref[h, r:r + step, :] = _bf(jnp.where(hmask[h], vv, 0.0))


def _sb_kernel(q_ref, k_ref, v_ref, u_ref, bd_ref, mg_ref, o_ref,
               kb_ref, vm_ref, acc_ref, car_ref, *, tq, tk, pos0):
    qi = pl.program_id(1)
    hmask = _head_masks()

    @pl.when(qi == 0)
    def _():
        _fill_kv(k_ref, v_ref, kb_ref, vm_ref, hmask)

    q = q_ref[0]
    qm = [jnp.where(hmask[h], q, jnp.zeros_like(q)) for h in range(N_HEADS)]
    acc_ref[...] = jnp.zeros_like(acc_ref)
    car_ref[...] = jnp.zeros_like(car_ref)
    u = u_ref[...]
    q0 = pos0 + qi * tq
    n_full = q0 // tk

    def block(kbi, masked):
        start = pl.multiple_of(kbi * tk, tk)
        kblk = kb_ref[pl.ds(start, tk), :]
        if masked:
            rowp = q0 + lax.broadcasted_iota(jnp.int32, (tq, tk), 0)
            colp = kbi * tk + lax.broadcasted_iota(jnp.int32, (tq, tk), 1)
            valid = colp < rowp
        acc = acc_ref[...]
        for h in range(N_HEADS):
            z = _dot_nt(qm[h], kblk)
            r = -_softplus(z)
            if masked:
                r = jnp.where(valid, r, 0.0)
            rc = _dot_exact_rhs(r, u, 2)
            car = car_ref[h]
            w = jnp.exp(z + rc + jnp.concatenate([car] * (tk // 128), axis=1))
            if masked:
                w = jnp.where(valid, w, 0.0)
            car_ref[h] = car + jnp.broadcast_to(rc[:, 0:1], car.shape)
            acc = acc + jnp.dot(_bf(w), vm_ref[h, pl.ds(start, tk), :], preferred_element_type=F32)
        acc_ref[...] = acc

    block(n_full, True)

    def body(i, c):
        block(n_full - 1 - i, False)
        return c

    lax.fori_loop(0, n_full, body, 0)
    o_ref[0] = _bf(_head_rms(acc_ref[...], bd_ref[...], mg_ref[...]))


def _fox_kernel(q_ref, k_ref, v_ref, fq_ref, fk_ref, bd_ref, mg_ref, o_ref,
                kb_ref, vm_ref, acc_ref, *, tq, tk, pos0):
    qi = pl.program_id(1)
    hmask = _head_masks()

    @pl.when(qi == 0)
    def _():
        _fill_kv(k_ref, v_ref, kb_ref, vm_ref, hmask)

    q = q_ref[0]
    qm = [jnp.where(hmask[h], q, jnp.zeros_like(q)) for h in range(N_HEADS)]
    fq = fq_ref[0]
    acc_ref[...] = jnp.zeros_like(acc_ref)
    q0 = pos0 + qi * tq
    n_full = q0 // tk

    def block(kbi, masked, ml):
        start = pl.multiple_of(kbi * tk, tk)
        kblk = kb_ref[pl.ds(start, tk), :]
        fk = fk_ref[0, kbi]
        if masked:
            rowp = q0 + lax.broadcasted_iota(jnp.int32, (tq, tk), 0)
            colp = kbi * tk + lax.broadcasted_iota(jnp.int32, (tq, tk), 1)
            valid = colp <= rowp
        alpha_full = jnp.zeros((tq, GW), F32)
        pv = jnp.zeros((tq, GW), F32)
        new_ml = []
        for h in range(N_HEADS):
            m_old, l_old = ml[2 * h], ml[2 * h + 1]
            s = _dot_nt(qm[h], kblk) + fq[:, h:h + 1] - fk[h:h + 1, :]
            if masked:
                s = jnp.where(valid, s, NEG)
            m_new = jnp.maximum(m_old, jnp.max(s, axis=1, keepdims=True))
            alpha = jnp.exp(m_old - m_new)
            p = jnp.exp(s - m_new)
            new_ml += [m_new, alpha * l_old + jnp.sum(p, axis=1, keepdims=True)]
            alpha_full = alpha_full + jnp.where(hmask[h], alpha, 0.0)
            pv = pv + jnp.dot(_bf(p), vm_ref[h, pl.ds(start, tk), :], preferred_element_type=F32)
        acc_ref[...] = acc_ref[...] * alpha_full + pv
        return tuple(new_ml)

    ml0 = tuple(jnp.full((tq, 1), NEG, F32) if i % 2 == 0 else jnp.zeros((tq, 1), F32)
                for i in range(2 * N_HEADS))
    ml = lax.fori_loop(0, n_full, lambda i, c: block(i, False, c), ml0)
    ml = block(n_full, True, ml)
    l_full = jnp.zeros((tq, GW), F32)
    for h in range(N_HEADS):
        l_full = l_full + jnp.where(hmask[h], ml[2 * h + 1], 0.0)
    o = acc_ref[...] / l_full
    o_ref[0] = _bf(_head_rms(o, bd_ref[...], mg_ref[...]))


def _attn_specs(b, tq_all, tk_all, tq):
    qspec = pl.BlockSpec((1, tq, GW), lambda bi, qi: (bi, qi, 0))
    kspec = pl.BlockSpec((1, tk_all, GW), lambda bi, qi: (bi, 0, 0))
    return qspec, kspec


def _sb_call(q, k, v, mg, bd, tq, tk, pos0):
    b, tq_all, _ = q.shape
    tk_all = k.shape[1]
    qspec, kspec = _attn_specs(b, tq_all, tk_all, tq)
    full = lambda a: pl.BlockSpec(a.shape, lambda bi, qi: (0,) * a.ndim)
    u = _const_tri_rev(tk)
    return pl.pallas_call(
        functools.partial(_sb_kernel, tq=tq, tk=tk, pos0=pos0),
        grid=(b, tq_all // tq),
        in_specs=[qspec, kspec, kspec, full(u), full(bd), full(mg)],
        out_specs=qspec,
        out_shape=jax.ShapeDtypeStruct((b, tq_all, GW), BF16),
        scratch_shapes=[pltpu.VMEM((tk_all, GW), BF16), pltpu.VMEM((N_HEADS, tk_all, GW), BF16),
                        pltpu.VMEM((tq, GW), F32), pltpu.VMEM((N_HEADS, tq, 128), F32)],
        compiler_params=_params(("arbitrary", "arbitrary")),
        name="stick_breaking_attention",
    )(q, k, v, u, bd, mg)


def _fox_call(q, k, v, fq, fk, mg, bd, tq, tk, pos0):
    b, tq_all, _ = q.shape
    tk_all = k.shape[1]
    qspec, kspec = _attn_specs(b, tq_all, tk_all, tq)
    full = lambda a: pl.BlockSpec(a.shape, lambda bi, qi: (0,) * a.ndim)
    return pl.pallas_call(
        functools.partial(_fox_kernel, tq=tq, tk=tk, pos0=pos0),
        grid=(b, tq_all // tq),
        in_specs=[qspec, kspec, kspec,
                  pl.BlockSpec((1, tq, N_HEADS), lambda bi, qi: (bi, qi, 0)),
                  pl.BlockSpec((1, tk_all // tk, N_HEADS, tk), lambda bi, qi: (bi, 0, 0, 0)),
                  full(bd), full(mg)],
        out_specs=qspec,
        out_shape=jax.ShapeDtypeStruct((b, tq_all, GW), BF16),
        scratch_shapes=[pltpu.VMEM((tk_all, GW), BF16), pltpu.VMEM((N_HEADS, tk_all, GW), BF16),
                        pltpu.VMEM((tq, GW), F32)],
        compiler_params=_params(("arbitrary", "arbitrary")),
        name="forgetting_attention",
    )(q, k, v, fq, fk, bd, mg)


def _band_kernel(q_ref, k_ref, v_ref, hk_ref, hv_ref, bias_ref, bd_ref, mg_ref, o_ref,
                 kp_ref, vp_ref, *, lq, nch, has_hist):
    qi = pl.program_id(1)
    hmask = _head_masks()
    t = k_ref.shape[1]
    hist = hk_ref.shape[1]

    @pl.when(qi == 0)
    def _():
        def put(r0, kk, vv):
            n = kk.shape[0]
            kp_ref[r0:r0 + n, :] = _bf(kk)
            for h in range(N_HEADS):
                vp_ref[h, r0:r0 + n, :] = _bf(jnp.where(hmask[h], vv, 0.0))
        step = min(256, t)
        for r in range(0, hist, 256):
            put(r, hk_ref[0, r:r + 256, :], hv_ref[0, r:r + 256, :])
        for r in range(0, t, step):
            put(hist + r, k_ref[0, r:r + step, :], v_ref[0, r:r + step, :])
        zeros = jnp.zeros((128, GW), F32)
        put(hist + t, zeros, zeros)

    col = lax.broadcasted_iota(jnp.int32, (lq, BAND_WIN), 1)
    for ci in range(nch):
        c = qi * nch + ci
        start = pl.multiple_of(c * lq, lq)
        kwin = kp_ref[pl.ds(start, BAND_WIN), :]
        valid = col < hist + lq
        if not has_hist:
            valid = jnp.logical_and(valid, col + c * lq >= hist)
        q = q_ref[0, ci * lq:(ci + 1) * lq, :]
        acc = jnp.zeros((lq, GW), F32)
        for h in range(N_HEADS):
            qh = jnp.where(hmask[h], q, jnp.zeros_like(q))
            s = jnp.where(valid, _dot_nt(qh, kwin) + bias_ref[h], NEG)
            p = jnp.exp(s - jnp.max(s, axis=1, keepdims=True))
            p = p / jnp.sum(p, axis=1, keepdims=True)
            acc = acc + jnp.dot(_bf(p), vp_ref[h, pl.ds(start, BAND_WIN), :], preferred_element_type=F32)
        o_ref[0, ci * lq:(ci + 1) * lq, :] = _bf(_head_rms(acc, bd_ref[...], mg_ref[...]))


def _band_call(q, k, v, hk, hv, bias, mg, bd, lq, nch, has_hist):
    b, t, _ = q.shape
    hist = hk.shape[1]
    tqb = lq * nch
    qspec = pl.BlockSpec((1, tqb, GW), lambda bi, qi: (bi, qi, 0))
    kspec = pl.BlockSpec((1, t, GW), lambda bi, qi: (bi, 0, 0))
    hspec = pl.BlockSpec((1, hist, GW), lambda bi, qi: (bi, 0, 0))
    full = lambda a: pl.BlockSpec(a.shape, lambda bi, qi: (0,) * a.ndim)
    rows = hist + t + 128
    return pl.pallas_call(
        functools.partial(_band_kernel, lq=lq, nch=nch, has_hist=has_hist),
        grid=(b, t // tqb),
        in_specs=[qspec, kspec, kspec, hspec, hspec, full(bias), full(bd), full(mg)],
        out_specs=qspec,
        out_shape=jax.ShapeDtypeStruct((b, t, GW), BF16),
        scratch_shapes=[pltpu.VMEM((rows, GW), BF16), pltpu.VMEM((N_HEADS, rows, GW), BF16)],
        compiler_params=_params(("arbitrary", "arbitrary")),
        name="band_attention",
    )(q, k, v, hk, hv, bias, bd, mg)


def _tile4(x):
    return jnp.concatenate([x] * N_HEADS, axis=0)


def _gdn1_kernel(qkv_ref, small_ref, st_ref, cw_ref, alog_ref, dtb_ref,
                 lblk_ref, oblk_ref, eb_ref, eg_ref, bd_ref,
                 xu_ref, xwk_ref, qkd_ref, qg_ref, kdec_ref, egl_ref, conv_ref,
                 xbuf_ref, *, tg, t_valid, n_tiles):
    ti = pl.program_id(1)

    @pl.when(ti == 0)
    def _():
        xbuf_ref[0:8, :] = st_ref[0]

    xbuf_ref[8:8 + tg, :] = qkv_ref[0]
    y = cw_ref[3:4, :] * xbuf_ref[8:8 + tg, :]
    for i in range(3):
        y = y + cw_ref[i:i + 1, :] * xbuf_ref[5 + i:5 + i + tg, :]
    tv = t_valid - (n_tiles - 1) * tg

    @pl.when(ti == n_tiles - 1)
    def _():
        conv_ref[0] = xbuf_ref[8 + tv - 3:8 + tv, :]

    xbuf_ref[0:8, :] = xbuf_ref[tg:tg + 8, :]
    y = _silu(y)
    bd = bd_ref[...]
    q, k, v = y[:, :GW], y[:, GW:2 * GW], y[:, 2 * GW:]
    qn = q * lax.rsqrt(_dot_exact_rhs(q * q, bd, 2) + EPS) * (HEAD_DIM ** -0.5)
    kn = k * lax.rsqrt(_dot_exact_rhs(k * k, bd, 2) + EPS)

    sm = small_ref[0]
    rowv = ti * tg + lax.broadcasted_iota(jnp.int32, (tg, 1), 0) < t_valid
    beta_all = jnp.where(rowv, _sigmoid(sm), 0.0)
    g_all = jnp.where(rowv, -jnp.exp(alog_ref[...]) * _softplus(sm + dtb_ref[...]), 0.0)
    gc = _dot_exact_lhs(lblk_ref[...], g_all, 3)
    gl = _dot_exact_lhs(oblk_ref[...], g_all, 3)
    bexp = _dot_exact_rhs(beta_all, eb_ref[...], 3)
    gx = _dot_exact_rhs(gc, eg_ref[...], 3)
    glx = _dot_exact_rhs(gl, eg_ref[...], 3)
    eg = jnp.exp(gx)
    kb = kn * bexp
    vb = v * bexp
    kbg = kb * eg
    qg_ref[0] = _bf(qn * eg)
    kdec_ref[0] = _bf(kn * jnp.exp(glx - gx))
    egl = jnp.exp(glx)

    ri = lax.broadcasted_iota(jnp.int32, (CHUNK, GW), 0)
    li = lax.broadcasted_iota(jnp.int32, (CHUNK, GW), 1) & (CHUNK - 1)
    eye = ri == li
    incl = ri >= li
    strict = ri > li
    rb = lax.broadcasted_iota(jnp.int32, (GW, GW), 0) >> 6
    bdm = rb == (lax.broadcasted_iota(jnp.int32, (GW, GW), 1) >> 6)
    bdm2 = jnp.concatenate([bdm, bdm], axis=1)

    for c in range(tg // CHUNK):
        sl = slice(c * CHUNK, (c + 1) * CHUNK)
        kbd = jnp.where(bdm, _tile4(kn[sl]), 0.0)
        kq = _dot_nt(jnp.concatenate([kb[sl], qn[sl]], axis=0), kbd)
        gxc = gx[sl]
        grow = jnp.sum(jnp.where(eye, gxc, 0.0), axis=0, keepdims=True)
        dec = jnp.where(incl, jnp.exp(gxc - grow), 0.0)
        nmat = jnp.where(strict, kq[:CHUNK] * dec, 0.0)
        qkd_ref[0, sl, :] = _bf(kq[CHUNK:] * dec)
        tmat = jnp.where(eye, 1.0, 0.0) - nmat
        pmat = _dot3(nmat, jnp.where(bdm, _tile4(nmat), 0.0))
        for step in range(5):
            pbd = jnp.where(bdm, _tile4(pmat), 0.0)
            if step < 4:
                tp = _dot3(jnp.concatenate([tmat, pmat], axis=0), pbd)
                tmat = tmat + tp[:CHUNK]
                pmat = tp[CHUNK:]
            else:
                tmat = tmat + _dot3(tmat, pbd)
        rhs = jnp.where(bdm2, _tile4(jnp.concatenate([vb[sl], kbg[sl]], axis=1)), 0.0)
        xs = _dot(tmat, rhs)
        xu_ref[0, sl, :] = xs[:, :GW]
        xwk_ref[0, sl, :] = _bf(xs[:, GW:])
        egl_ref[0, c] = egl[c * CHUNK:c * CHUNK + 8, :]


def _gdn2_kernel(xu_ref, xwk_ref, qkd_ref, qg_ref, kdec_ref, egl_ref, z_ref, s0_ref, ng_ref, bd_ref,
                 o_ref, sfin_ref, s_ref, *, bb, nchunk):
    ti = pl.program_id(1)

    @pl.when(ti == 0)
    def _():
        s_ref[...] = s0_ref[...]

    rb = lax.broadcasted_iota(jnp.int32, (GW, GW), 0) >> 6
    bdm = rb == (lax.broadcasted_iota(jnp.int32, (GW, GW), 1) >> 6)
    bd = bd_ref[...]
    ng = ng_ref[...]

    def chunk(c, carry):
        r0 = pl.multiple_of(c * CHUNK, CHUNK)
        rows = pl.ds(r0, CHUNK)
        for b in range(bb):
            s = s_ref[b]
            lhs = jnp.concatenate([xwk_ref[b, rows, :], qg_ref[b, rows, :]], axis=0)
            r = jnp.dot(lhs, _bf(s), preferred_element_type=F32)
            vnew = xu_ref[b, rows, :] - r[:CHUNK]
            vbd = jnp.where(bdm, _tile4(vnew), 0.0)
            o = r[CHUNK:] + jnp.dot(qkd_ref[b, rows, :], _bf(vbd), preferred_element_type=F32)
            kd_t = kdec_ref[b, rows, :].astype(F32).T
            upd = _dot(kd_t, vnew)
            s_ref[b] = egl_ref[b, c, 0:1, :] * s + jnp.where(bdm, upd, 0.0)
            zz = z_ref[b, rows, :]
            o_ref[b, rows, :] = _bf(_head_rms(o, bd, ng) * _silu(zz))
        return carry

    lax.fori_loop(0, nchunk, chunk, 0)
    sfin_ref[...] = s_ref[...]


def _gdn_consts(tg):
    i = np.arange(tg)
    same = i[:, None] // CHUNK == i[None, :] // CHUNK
    lblk = jnp.asarray(same & (i[:, None] >= i[None, :]), BF16)
    oblk = jnp.asarray(same, BF16)
    eb = np.zeros((128, GW), np.float32)
    eg = np.zeros((128, GW), np.float32)
    for h in range(N_HEADS):
        eb[h, h * HEAD_DIM:(h + 1) * HEAD_DIM] = 1
        eg[N_HEADS + h, h * HEAD_DIM:(h + 1) * HEAD_DIM] = 1
    return lblk, oblk, jnp.asarray(eb, BF16), jnp.asarray(eg, BF16)


def _gdn_call(qkv, small, z, st8, s0bd, cw, alog, dtb, ng, bd, tg, t_valid, bb):
    b, t, _ = qkv.shape
    n_tiles = t // tg
    lblk, oblk, eb, eg = _gdn_consts(tg)
    full = lambda a: pl.BlockSpec(a.shape, lambda bi, ti: (0,) * a.ndim)
    row = lambda n: pl.BlockSpec((1, tg, n), lambda bi, ti: (bi, ti, 0))
    nct = tg // CHUNK
    xu, xwk, qkd, qg, kdec, egl, conv_new = pl.pallas_call(
        functools.partial(_gdn1_kernel, tg=tg, t_valid=t_valid, n_tiles=n_tiles),
        grid=(b, n_tiles),
        in_specs=[row(3 * GW), row(128),
                  pl.BlockSpec((1, 8, 3 * GW), lambda bi, ti: (bi, 0, 0)),
                  full(cw), full(alog), full(dtb), full(lblk), full(oblk), full(eb), full(eg), full(bd)],
        out_specs=[row(GW), row(GW), row(GW), row(GW), row(GW),
                   pl.BlockSpec((1, nct, 8, GW), lambda bi, ti: (bi, ti, 0, 0)),
                   pl.BlockSpec((1, 3, 3 * GW), lambda bi, ti: (bi, 0, 0))],
        out_shape=[jax.ShapeDtypeStruct((b, t, GW), F32), jax.ShapeDtypeStruct((b, t, GW), BF16),
                   jax.ShapeDtypeStruct((b, t, GW), BF16), jax.ShapeDtypeStruct((b, t, GW), BF16),
                   jax.ShapeDtypeStruct((b, t, GW), BF16),
                   jax.ShapeDtypeStruct((b, t // CHUNK, 8, GW), F32),
                   jax.ShapeDtypeStruct((b, 3, 3 * GW), F32)],
        scratch_shapes=[pltpu.VMEM((8 + tg, 3 * GW), F32)],
        compiler_params=_params(("arbitrary", "arbitrary")),
        name="gdn_chunk_solve",
    )(qkv, small, st8, cw, alog, dtb, lblk, oblk, eb, eg, bd)

    full2 = lambda a: pl.BlockSpec(a.shape, lambda bi, ti: (0,) * a.ndim)
    rowb = pl.BlockSpec((bb, tg, GW), lambda bi, ti: (bi, ti, 0))
    sspec = pl.BlockSpec((bb, GW, GW), lambda bi, ti: (bi, 0, 0))
    o, sfin = pl.pallas_call(
        functools.partial(_gdn2_kernel, bb=bb, nchunk=nct),
        grid=(b // bb, n_tiles),
        in_specs=[rowb, rowb, rowb, rowb, rowb,
                  pl.BlockSpec((bb, nct, 8, GW), lambda bi, ti: (bi, ti, 0, 0)),
                  rowb, sspec, full2(ng), full2(bd)],
        out_specs=[rowb, sspec],
        out_shape=[jax.ShapeDtypeStruct((b, t, GW), BF16), jax.ShapeDtypeStruct((b, GW, GW), F32)],
        scratch_shapes=[pltpu.VMEM((bb, GW, GW), F32)],
        compiler_params=_params(("arbitrary", "arbitrary")),
        name="gdn_recurrence",
    )(xu, xwk, qkd, qg, kdec, egl, z, s0bd, ng, bd)
    return o, conv_new, sfin


def _out_kernel(x_ref, oa_ref, ob_ref, oc_ref, od_ref, ga1_ref, sc2_ref, sh2_ref, ga2_ref, g2_ref,
                wo_ref, wup_ref, cw_ref, wdn_ref, st_ref, y_ref, cnew_ref,
                ubuf_ref, car_ref, gs_ref, *, tm, tiles_per_seq, nff):
    i = pl.program_id(0)

    @pl.when(i % tiles_per_seq == 0)
    def _():
        car_ref[...] = st_ref[0]

    mix = jnp.dot(oa_ref[...], wo_ref[0:GW, :], preferred_element_type=F32)
    for n, ref in enumerate((ob_ref, oc_ref, od_ref)):
        mix = mix + jnp.dot(ref[...], wo_ref[(n + 1) * GW:(n + 2) * GW, :], preferred_element_type=F32)
    x1 = x_ref[...] + ga1_ref[0] * mix
    ms = jnp.mean(x1 * x1, axis=-1, keepdims=True)
    h = x1 * lax.rsqrt(ms + EPS) * g2_ref[...]
    h = _bf(h * (1.0 + sc2_ref[0]) + sh2_ref[0])
    for j in range(nff):
        cols = slice(j * 2 * GW, (j + 1) * 2 * GW)
        u = jnp.dot(h, wup_ref[:, cols], preferred_element_type=F32)
        ubuf_ref[0:8, :] = car_ref[:, cols]
        ubuf_ref[8:8 + tm, :] = u
        cv = (cw_ref[0:1, cols] * ubuf_ref[6:6 + tm, :] + cw_ref[1:2, cols] * ubuf_ref[7:7 + tm, :]
              + cw_ref[2:3, cols] * u)
        car_ref[:, cols] = ubuf_ref[tm:tm + 8, :]
        gs_ref[:, j * GW:(j + 1) * GW] = _bf(_silu(cv[:, :GW]) * cv[:, GW:])
    y_ref[...] = x1 + ga2_ref[0] * jnp.dot(gs_ref[...], wdn_ref[...], preferred_element_type=F32)
    cnew_ref[0] = car_ref[...]


def _out_call(x2, oa, ob, oc, od, ga1, sc2, sh2, ga2, g2, wo, wup, cw, wdn, st8, tm, rows_per_seq):
    m, d = x2.shape
    dff2 = wup.shape[1]
    nff = dff2 // (2 * GW)
    tiles_per_seq = rows_per_seq // tm
    nseq = m // rows_per_seq
    r = ga1.shape[1]
    rows_per_mod = m // ga1.shape[0]
    tiles_per_mod = rows_per_mod // tm
    mod_spec = pl.BlockSpec((1, r, d), lambda i: (i // tiles_per_mod, 0, 0))
    full = lambda a: pl.BlockSpec(a.shape, lambda i: (0,) * a.ndim)
    row = lambda n: pl.BlockSpec((tm, n), lambda i: (i, 0))
    stspec = pl.BlockSpec((1, 8, dff2), lambda i: (i // tiles_per_seq, 0, 0))
    return pl.pallas_call(
        functools.partial(_out_kernel, tm=tm, tiles_per_seq=tiles_per_seq, nff=nff),
        grid=(m // tm,),
        in_specs=[row(d), row(GW), row(GW), row(GW), row(GW), mod_spec, mod_spec, mod_spec, mod_spec,
                  full(g2), full(wo), full(wup), full(cw), full(wdn), stspec],
        out_specs=[row(d), stspec],
        out_shape=[jax.ShapeDtypeStruct((m, d), F32), jax.ShapeDtypeStruct((nseq, 8, dff2), F32)],
        scratch_shapes=[pltpu.VMEM((8 + tm, 2 * GW), F32), pltpu.VMEM((8, dff2), F32),
                        pltpu.VMEM((tm, dff2 // 2), BF16)],
        compiler_params=_params(("arbitrary",)),
        name="out_projection_mlp",
    )(x2, oa, ob, oc, od, ga1, sc2, sh2, ga2, g2, wo, wup, cw, wdn, st8)


def _pack_in_weights(w_in):
    a = 4 * GW + 2 * N_HEADS
    off_b = a
    off_c = off_b + 3 * GW
    off_d = off_c + 3 * GW + N_HEADS
    dd = w_in.shape[:2]
    zeros = lambda n: jnp.zeros(dd + (n,), w_in.dtype)
    small = jnp.concatenate([w_in[..., 4 * GW:a], zeros(128 - 2 * N_HEADS),
                             w_in[..., off_c + 3 * GW:off_d], zeros(128 - N_HEADS)], axis=-1)
    packed = jnp.concatenate([w_in[..., :4 * GW], w_in[..., off_b:off_b + 3 * GW],
                              w_in[..., off_c:off_c + 3 * GW], w_in[..., off_d:off_d + 3 * GW], small],
                             axis=-1)
    return _bf(packed)


def _pack_ff(a):
    lead = a.shape[:-1]
    nff = a.shape[-1] // (2 * GW)
    a = a.reshape(lead + (2, nff, GW))
    return jnp.swapaxes(a, -3, -2).reshape(lead + (2 * nff * GW,))


def _unpack_ff(a):
    lead = a.shape[:-1]
    nff = a.shape[-1] // (2 * GW)
    a = a.reshape(lead + (nff, 2, GW))
    return jnp.swapaxes(a, -3, -2).reshape(lead + (2 * nff * GW,))


def _lane_vec(v, offset):
    depth = v.shape[0]
    out = jnp.zeros((depth, 1, 128), F32)
    return out.at[:, 0, offset:offset + N_HEADS].set(v.astype(F32))


def _band_bias(rel_table, lq, hist):
    i = np.arange(lq)[:, None]
    j = np.arange(BAND_WIN)[None, :]
    rel = np.clip(i - j + hist, -REL_CLIP, REL_CLIP) + REL_CLIP
    return rel_table.astype(F32)[:, rel]


def _layer(x, mods_in, mods_out, wts, cache, cfg):
    b, t, d = x.shape
    m = b * t
    bd = wts["bd"]
    sc1, sh1 = mods_in
    (gqkv, gz, small, sbq, sbk, sbv, fq, fk, fv, logf, bq, bk, bv) = _in_call(
        x.reshape(m, d), sc1, sh1, wts["norm_mix_g"], wts["w_in"], wts["qk_gains"], wts["fox_b_f"], bd,
        cfg["tm_in"], cfg["rows_per_mod_in"])
    r3 = lambda a: a.reshape(b, t, a.shape[-1])
    new = {"sb_k": r3(sbk), "sb_v": r3(sbv), "fox_k": r3(fk), "fox_v": r3(fv), "fox_logf": r3(logf)}
    past = cache is not None

    tp = cfg["t_gdn"]
    pad_t = lambda a: jnp.pad(r3(a), ((0, 0), (0, tp - t), (0, 0)))
    if past:
        st8 = jnp.pad(cache["gdn_conv"], ((0, 0), (5, 0), (0, 0)))
        s0 = cache["gdn_state"]
        eye = jnp.eye(N_HEADS, dtype=F32)
        s0bd = jnp.einsum("bhkv,hg->bhkgv", s0, eye).reshape(b, GW, GW)
    else:
        st8 = jnp.zeros((b, 8, 3 * GW), F32)
        s0bd = jnp.zeros((b, GW, GW), F32)
    oa, conv_new, sfin = _gdn_call(pad_t(gqkv), pad_t(small), pad_t(gz), st8, s0bd,
                                   wts["gdn_conv_w"], wts["gdn_a_log"], wts["gdn_dt_bias"],
                                   wts["gdn_norm_g"], bd, cfg["tg"], t, cfg["bb"])
    oa = oa[:, :t].reshape(m, GW)
    sfin = sfin.reshape(b, N_HEADS, HEAD_DIM, N_HEADS, HEAD_DIM)
    new["gdn_state"] = jnp.stack([sfin[:, h, :, h, :] for h in range(N_HEADS)], axis=1)
    new["gdn_conv"] = conv_new

    tq, tk = cfg["tq"], cfg["tk"]
    if past:
        pos0 = cache["sb_k"].shape[1]
        tk_all = -(-(pos0 + t) // tk) * tk
        cat = lambda c, n: jnp.pad(jnp.concatenate([c.reshape(b, pos0, -1), r3(n)], axis=1),
                                   ((0, 0), (0, tk_all - pos0 - t), (0, 0)))
        ksb, vsb = cat(cache["sb_k"], sbk), cat(cache["sb_v"], sbv)
        kfx, vfx = cat(cache["fox_k"], fk), cat(cache["fox_v"], fv)
        lf_all = cat(cache["fox_logf"], logf)
    else:
        pos0, tk_all = 0, t
        ksb, vsb, kfx, vfx, lf_all = r3(sbk), r3(sbv), r3(fk), r3(fv), r3(logf)
    ob = _sb_call(r3(sbq), ksb, vsb, wts["merge_g"][0:1], bd, tq, tk, pos0)
    fcum = _cumsum_call(jnp.swapaxes(lf_all, 1, 2).reshape(b * N_HEADS, tk_all))
    fcum = fcum.reshape(b, N_HEADS, tk_all)
    f_q = jnp.swapaxes(fcum[:, :, pos0:pos0 + t], 1, 2)
    f_k = jnp.swapaxes(fcum.reshape(b, N_HEADS, tk_all // tk, tk), 1, 2)
    oc = _fox_call(r3(fq), kfx, vfx, f_q, f_k, wts["merge_g"][1:2], bd, tq, tk, pos0)

    if past:
        hk, hv = cache["band_k"].reshape(b, -1, GW), cache["band_v"].reshape(b, -1, GW)
        keep = hk.shape[1]
        new["band_k"] = jnp.concatenate([hk, r3(bk)], axis=1)[:, -keep:]
        new["band_v"] = jnp.concatenate([hv, r3(bv)], axis=1)[:, -keep:]
    else:
        hk = hv = jnp.zeros((b, BAND_ROWS, GW), F32)
        keep = min(BAND_ROWS, t)
        new["band_k"], new["band_v"] = r3(bk)[:, -keep:], r3(bv)[:, -keep:]
    od = _band_call(r3(bq), r3(bk), r3(bv), hk, hv, wts["band_bias_" + cfg["name"]],
                    wts["merge_g"][2:3], bd, cfg["lq"], cfg["nch"], past)

    if past:
        st_ffn = jnp.pad(_pack_ff(cache["ffn_conv"]), ((0, 0), (6, 0), (0, 0)))
    else:
        st_ffn = jnp.zeros((b, 8, wts["w_up"].shape[1]), F32)
    ga1, sc2, sh2, ga2 = mods_out
    y, cnew = _out_call(x.reshape(m, d), oa, ob.reshape(m, GW), oc.reshape(m, GW), od.reshape(m, GW),
                        ga1, sc2, sh2, ga2, wts["norm_ffn_g"], wts["w_o"], wts["w_up"],
                        wts["ffn_conv_w"], wts["w_down"], st_ffn, cfg["tm_out"], t)
    new["ffn_conv"] = _unpack_ff(cnew[:, 6:8])
    hd = lambda a: a.reshape(a.shape[0], a.shape[1], N_HEADS, HEAD_DIM)
    for n in ("sb_k", "sb_v", "fox_k", "fox_v", "band_k", "band_v"):
        new[n] = hd(new[n])
    return y.reshape(b, t, d), new


STATE_KEYS = ("gdn_conv", "gdn_state", "sb_k", "sb_v", "fox_k", "fox_v", "fox_logf",
              "band_k", "band_v", "ffn_conv")


def _group_cfg(name, b, t):
    if t % 512 == 0:
        return dict(name=name, tm_in=512, rows_per_mod_in=t, tm_out=512, t_gdn=t, tg=512,
                    bb=4 if b % 4 == 0 else 1, tq=256, tk=256, lq=CHUNK, nch=4)
    assert t % 16 == 0 and t <= CHUNK
    return dict(name=name, tm_in=b * t, rows_per_mod_in=b * t, tm_out=t, t_gdn=CHUNK, tg=CHUNK,
                bb=4 if b % 4 == 0 else 1, tq=t, tk=256, lq=t, nch=1)


def kernel(x_prompt, x_sample, c_prompt, c_sample, state_gdn_conv, state_gdn, cache_sb_k, cache_sb_v, cache_fox_k, cache_fox_v, cache_fox_logf, cache_band_k, cache_band_v, state_ffn_conv, ada_w, ada_b, norm_mix_g, w_in, gdn_conv_w, gdn_a_log, gdn_dt_bias, gdn_norm_g, fox_q_g, fox_k_g, fox_b_f, band_q_g, band_k_g, band_rel_bias, merge_g, w_o, norm_ffn_g, w_up, ffn_conv_w, w_down):
    depth = ada_w.shape[0]
    bp, tp, d = x_prompt.shape
    bs, ts, _ = x_sample.shape
    cfg_p = _group_cfg("p", bp, tp)
    cfg_s = _group_cfg("s", bs, ts)
    hist_s = cache_band_k.shape[2]

    mod = _ada_call(jnp.concatenate([c_prompt, c_sample], axis=0), ada_w, ada_b)
    mod = mod.reshape(depth, bp + bs, 6, d)

    tile_h = lambda g: jnp.tile(g.astype(F32), (1, N_HEADS))[:, None, :]
    w_in_p = _pack_in_weights(w_in)
    w_o_b, w_up_b, w_dn_b = _bf(w_o), _bf(_pack_ff(w_up)), _bf(w_down)
    ffn_cw = jnp.pad(_pack_ff(ffn_conv_w.astype(F32)), ((0, 0), (0, 5), (0, 0)))
    gdn_cw = jnp.pad(gdn_conv_w.astype(F32), ((0, 0), (0, 4), (0, 0)))
    qk_gains = jnp.concatenate([tile_h(fox_q_g), tile_h(fox_k_g), tile_h(band_q_g), tile_h(band_k_g),
                                jnp.zeros((depth, 4, GW), F32)], axis=1)
    bd = _const_bd256()

    y_p, y_s = x_prompt, x_sample
    new_p = {n: [] for n in STATE_KEYS}
    new_s = {n: [] for n in STATE_KEYS}
    for l in range(depth):
        wts = {
            "bd": bd, "norm_mix_g": norm_mix_g[l][None].astype(F32), "w_in": w_in_p[l],
            "qk_gains": qk_gains[l], "fox_b_f": _lane_vec(fox_b_f, 0)[l],
            "gdn_conv_w": gdn_cw[l], "gdn_a_log": _lane_vec(gdn_a_log, N_HEADS)[l],
            "gdn_dt_bias": _lane_vec(gdn_dt_bias, N_HEADS)[l], "gdn_norm_g": tile_h(gdn_norm_g)[l],
            "merge_g": merge_g[l].reshape(3, GW).astype(F32),
            "band_bias_p": _band_bias(band_rel_bias[l], cfg_p["lq"], BAND_ROWS),
            "band_bias_s": _band_bias(band_rel_bias[l], cfg_s["lq"], hist_s),
            "w_o": w_o_b[l], "norm_ffn_g": norm_ffn_g[l][None].astype(F32), "w_up": w_up_b[l],
            "ffn_conv_w": ffn_cw[l], "w_down": w_dn_b[l],
        }
        mp = mod[l, :bp]
        ms = mod[l, bp:]
        pm = lambda i: mp[:, i][:, None, :]
        mods_in_p = (pm(1), pm(0))
        mods_out_p = (pm(2), pm(4), pm(3), pm(5))
        srow = lambda i: jnp.repeat(ms[:, i], ts, axis=0)[None]
        sm1 = lambda i: ms[:, i][:, None, :]
        mods_in_s = (srow(1), srow(0))
        mods_out_s = (sm1(2), sm1(4), sm1(3), sm1(5))
        cache = {"gdn_conv": state_gdn_conv[l], "gdn_state": state_gdn[l],
                 "sb_k": cache_sb_k[l], "sb_v": cache_sb_v[l], "fox_k": cache_fox_k[l],
                 "fox_v": cache_fox_v[l], "fox_logf": cache_fox_logf[l],
                 "band_k": cache_band_k[l], "band_v": cache_band_v[l], "ffn_conv": state_ffn_conv[l]}
        y_p, st_p = _layer(y_p, mods_in_p, mods_out_p, wts, None, cfg_p)
        y_s, st_s = _layer(y_s, mods_in_s, mods_out_s, wts, cache, cfg_s)
        for n in STATE_KEYS:
            new_p[n].append(st_p[n])
            new_s[n].append(st_s[n])
    return ((y_p, y_s) + tuple(jnp.stack(new_p[n]) for n in STATE_KEYS)
            + tuple(jnp.stack(new_s[n]) for n in STATE_KEYS))
```

```python
import functools

import numpy as np
import jax
import jax.numpy as jnp
from jax import lax
from jax.experimental import pallas as pl
from jax.experimental.pallas import tpu as pltpu

F32 = jnp.float32
BF16 = jnp.bfloat16

HEAD_DIM = 64
N_HEADS = 4
GW = N_HEADS * HEAD_DIM
CHUNK = 64
BAND_ROWS = 512
REL_CLIP = 2 * CHUNK
EPS = 1e-6
NEG = -1e30
VMEM_LIMIT = 56 * 1024 * 1024
BAND_WIN = BAND_ROWS + 128
BAND_EXT = BAND_WIN + 128


def _bf(x):
    return x.astype(BF16)


def _dot(a, b):
    return jnp.dot(_bf(a), _bf(b), preferred_element_type=F32)


def _dot_nt(a, b):
    return lax.dot_general(_bf(a), _bf(b), (((1,), (1,)), ((), ())), preferred_element_type=F32)


def _split(x, n):
    parts = []
    for _ in range(n - 1):
        hi = _bf(x)
        parts.append(hi)
        x = x - hi.astype(F32)
    parts.append(_bf(x))
    return parts


def _dot_exact_rhs(x, m, n):
    return sum(jnp.dot(p, m, preferred_element_type=F32) for p in _split(x, n))


def _dot_exact_lhs(m, x, n):
    return sum(jnp.dot(m, p, preferred_element_type=F32) for p in _split(x, n))


def _sigmoid(x):
    return 1.0 / (1.0 + jnp.exp(-x))


def _silu(x):
    return x * _sigmoid(x)


def _softplus(x):
    return jnp.maximum(x, 0.0) + jnp.log(1.0 + jnp.exp(-jnp.abs(x)))


def _head_rms(o, bd, gain):
    ss = _dot_exact_rhs(o * o, bd, 2)
    return o * lax.rsqrt(ss * (1.0 / HEAD_DIM) + EPS) * gain


def _head_masks(width=GW):
    lane = lax.broadcasted_iota(jnp.int32, (1, width), 1)
    return [((lane & (GW - 1)) >> 6) == h for h in range(N_HEADS)]


def _params(sem, **kw):
    return pltpu.CompilerParams(dimension_semantics=sem, vmem_limit_bytes=VMEM_LIMIT, **kw)


def _np_bd(n, blk):
    i = np.arange(n)
    return (i[:, None] // blk == i[None, :] // blk)


def _const_bd256():
    return jnp.asarray(_np_bd(GW, HEAD_DIM), BF16)


def _const_tri_fwd(n):
    i = np.arange(n)
    return jnp.asarray(i[:, None] <= i[None, :], BF16)


def _const_tri_rev(n):
    i = np.arange(n)
    return jnp.asarray(i[:, None] >= i[None, :], BF16)


def _ada_kernel(c_ref, w_ref, b_ref, o_ref):
    s = _silu(c_ref[...])
    o_ref[0] = _dot(s, w_ref[0]) + b_ref[0]


def _ada_call(c_all, ada_w, ada_b):
    depth, d, n = ada_w.shape
    rows = c_all.shape[0]
    tn = 1536
    return pl.pallas_call(
        _ada_kernel,
        grid=(depth, n // tn),
        in_specs=[pl.BlockSpec((rows, d), lambda l, j: (0, 0)),
                  pl.BlockSpec((1, d, tn), lambda l, j: (l, 0, j)),
                  pl.BlockSpec((1, 1, tn), lambda l, j: (l, 0, j))],
        out_specs=pl.BlockSpec((1, rows, tn), lambda l, j: (l, 0, j)),
        out_shape=jax.ShapeDtypeStruct((depth, rows, n), F32),
        compiler_params=_params(("arbitrary", "arbitrary")),
        name="ada_modulation",
    )(c_all, ada_w, ada_b.reshape(depth, 1, n))


IN_GROUPS = 14
IN_COLS_PACKED = IN_GROUPS * GW


def _in_kernel(x_ref, sc_ref, sh_ref, g_ref, w_ref, gains_ref, bf_ref, bd_ref,
               gqkv_ref, gz_ref, small_ref, sbq_ref, sbk_ref, sbv_ref,
               fq_ref, fk_ref, fv_ref, logf_ref, bq_ref, bk_ref, bv_ref):
    x = x_ref[...]
    ms = jnp.mean(x * x, axis=-1, keepdims=True)
    h = x * lax.rsqrt(ms + EPS) * g_ref[...]
    h = _bf(h * (1.0 + sc_ref[0]) + sh_ref[0])
    bd = bd_ref[...]
    scale = HEAD_DIM ** -0.5

    def col(j, n=1):
        return jnp.dot(h, w_ref[:, j * GW:(j + n) * GW], preferred_element_type=F32)

    gqkv_ref[...] = col(0, 3)
    gz_ref[...] = col(3)
    sbq_ref[...] = _bf(col(4) * scale)
    sbk_ref[...] = col(5)
    sbv_ref[...] = col(6)
    fq_ref[...] = _bf(_head_rms(col(7), bd, gains_ref[0:1, :]) * scale)
    fk_ref[...] = _head_rms(col(8), bd, gains_ref[1:2, :])
    fv_ref[...] = col(9)
    bq_ref[...] = _bf(_head_rms(col(10), bd, gains_ref[2:3, :]) * scale)
    bk_ref[...] = _head_rms(col(11), bd, gains_ref[3:4, :])
    bv_ref[...] = col(12)
    sm = col(13)
    small_ref[...] = sm[:, :128]
    lf = sm[:, 128:] + bf_ref[...]
    logf = jnp.minimum(lf, 0.0) - jnp.log(1.0 + jnp.exp(-jnp.abs(lf)))
    logf_ref[...] = logf[:, :N_HEADS]


def _in_call(x2, sc, sh, g, w, gains, bfv, bd, tm, rows_per_mod):
    m, d = x2.shape
    r = sc.shape[1]
    tiles_per_mod = rows_per_mod // tm
    mod_spec = pl.BlockSpec((1, r, d), lambda i: (i // tiles_per_mod, 0, 0))
    full = lambda a: pl.BlockSpec(a.shape, lambda i: (0,) * a.ndim)
    row = lambda n: pl.BlockSpec((tm, n), lambda i: (i, 0))
    out_defs = [(3 * GW, F32), (GW, F32), (128, F32), (GW, BF16), (GW, F32), (GW, F32),
                (GW, BF16), (GW, F32), (GW, F32), (N_HEADS, F32), (GW, BF16), (GW, F32), (GW, F32)]
    return pl.pallas_call(
        _in_kernel,
        grid=(m // tm,),
        in_specs=[row(d), mod_spec, mod_spec, full(g), full(w), full(gains), full(bfv), full(bd)],
        out_specs=[row(n) for n, _ in out_defs],
        out_shape=[jax.ShapeDtypeStruct((m, n), dt) for n, dt in out_defs],
        compiler_params=_params(("arbitrary",)),
        name="in_projection",
    )(x2, sc, sh, g, w, gains, bfv, bd)


def _cumsum_kernel(x_ref, u_ref, o_ref):
    r, tk = x_ref.shape
    carry = jnp.zeros((r, 1), F32)
    for blk in range(tk // 256):
        o = _dot_exact_rhs(x_ref[:, blk * 256:(blk + 1) * 256], u_ref[...], 3) + carry
        o_ref[:, blk * 256:(blk + 1) * 256] = o
        carry = o[:, 255:256]


def _cumsum_call(x):
    return pl.pallas_call(
        _cumsum_kernel,
        out_shape=jax.ShapeDtypeStruct(x.shape, F32),
        compiler_params=pltpu.CompilerParams(vmem_limit_bytes=VMEM_LIMIT),
        name="logf_cumsum",
    )(x, _const_tri_fwd(256))


def _fill_kv(k_ref, v_ref, kb_ref, vm_ref, hmask, tk):
    for kbi in range(k_ref.shape[1] // tk):
        rows = slice(kbi * tk, (kbi + 1) * tk)
        kb_ref[rows, :] = _bf(k_ref[0, rows, :])
        vv = v_ref[0, rows, :]
        for h in range(N_HEADS):
            vm_ref[kbi, h * tk:(h + 1) * tk, :] = _bf(jnp.where(hmask[h], vv, 0.0))


def _stack_heads(q, hmask):
    return jnp.concatenate([jnp.where(hmask[h], q, jnp.zeros_like(q)) for h in range(N_HEADS)], axis=0)


def _heads_to_lanes(w, tq):
    return jnp.concatenate([w[h * tq:(h + 1) * tq] for h in range(N_HEADS)], axis=1)


def _stacked_positions(q0, kbi, tq, tk):
    rows = lax.broadcasted_iota(jnp.int32, (N_HEADS * tq, tk), 0) & (tq - 1)
    cols = lax.broadcasted_iota(jnp.int32, (N_HEADS * tq, tk), 1)
    return q0 + rows, kbi * tk + cols


def _sb_kernel(q_ref, k_ref, v_ref, u2_ref, bd_ref, mg_ref, o_ref,
               kb_ref, vm_ref, acc_ref, car_ref, *, tq, tk, pos0):
    qi = pl.program_id(1)
    hmask = _head_masks()

    @pl.when(qi == 0)
    def _():
        _fill_kv(k_ref, v_ref, kb_ref, vm_ref, hmask, tk)

    qs = _stack_heads(q_ref[0], hmask)
    acc_ref[...] = jnp.zeros_like(acc_ref)
    car_ref[...] = jnp.zeros_like(car_ref)
    q0 = pos0 + qi * tq
    n_full = q0 // tk

    def block(kbi, masked):
        start = pl.multiple_of(kbi * tk, tk)
        z = _dot_nt(qs, kb_ref[pl.ds(start, tk), :])
        r = -_softplus(z)
        if masked:
            rowp, colp = _stacked_positions(q0, kbi, tq, tk)
            valid = colp < rowp
            r = jnp.where(valid, r, 0.0)
        rc = jnp.dot(jnp.concatenate(_split(r, 2), axis=1), u2_ref[...], preferred_element_type=F32)
        car = car_ref[...]
        w = jnp.exp(z + rc + jnp.concatenate([car] * (tk // 128), axis=1))
        if masked:
            w = jnp.where(valid, w, 0.0)
        car_ref[...] = car + jnp.broadcast_to(rc[:, 0:1], car.shape)
        acc_ref[...] += jnp.dot(_heads_to_lanes(_bf(w), tq), vm_ref[kbi], preferred_element_type=F32)

    block(n_full, True)

    def body(i, c):
        block(n_full - 1 - i, False)
        return c

    lax.fori_loop(0, n_full, body, 0)
    o_ref[0] = _bf(_head_rms(acc_ref[...], bd_ref[...], mg_ref[...]))


def _fox_kernel(q_ref, k_ref, v_ref, fq_ref, fk_ref, bd_ref, mg_ref, o_ref,
                kb_ref, vm_ref, acc_ref, fqb_ref, *, tq, tk, pos0):
    qi = pl.program_id(1)
    hmask = _head_masks()

    @pl.when(qi == 0)
    def _():
        _fill_kv(k_ref, v_ref, kb_ref, vm_ref, hmask, tk)

    qs = _stack_heads(q_ref[0], hmask)
    fq = fq_ref[0]
    for h in range(N_HEADS):
        fqb_ref[h * tq:(h + 1) * tq, :] = jnp.broadcast_to(fq[:, h:h + 1], (tq, tk))
    acc_ref[...] = jnp.zeros_like(acc_ref)
    q0 = pos0 + qi * tq
    n_full = q0 // tk

    def block(kbi, masked, ml):
        m_old, l_old = ml
        start = pl.multiple_of(kbi * tk, tk)
        z = _dot_nt(qs, kb_ref[pl.ds(start, tk), :])
        fk = fk_ref[0, kbi]
        s = jnp.concatenate([z[h * tq:(h + 1) * tq] + fqb_ref[h * tq:(h + 1) * tq, :] - fk[h:h + 1, :]
                             for h in range(N_HEADS)], axis=0)
        if masked:
            rowp, colp = _stacked_positions(q0, kbi, tq, tk)
            s = jnp.where(colp <= rowp, s, NEG)
        m_new = jnp.maximum(m_old, jnp.max(s, axis=1, keepdims=True))
        alpha = jnp.exp(m_old - m_new)
        p = jnp.exp(s - m_new)
        l_new = alpha * l_old + jnp.sum(p, axis=1, keepdims=True)
        alpha_full = jnp.zeros((tq, GW), F32)
        for h in range(N_HEADS):
            alpha_full = alpha_full + jnp.where(hmask[h], alpha[h * tq:(h + 1) * tq], 0.0)
        pv = jnp.dot(_heads_to_lanes(_bf(p), tq), vm_ref[kbi], preferred_element_type=F32)
        acc_ref[...] = acc_ref[...] * alpha_full + pv
        return m_new, l_new

    ml0 = (jnp.full((N_HEADS * tq, 1), NEG, F32), jnp.zeros((N_HEADS * tq, 1), F32))
    ml = lax.fori_loop(0, n_full, lambda i, c: block(i, False, c), ml0)
    _, l_fin = block(n_full, True, ml)
    l_full = jnp.zeros((tq, GW), F32)
    for h in range(N_HEADS):
        l_full = l_full + jnp.where(hmask[h], l_fin[h * tq:(h + 1) * tq], 0.0)
    o = acc_ref[...] / l_full
    o_ref[0] = _bf(_head_rms(o, bd_ref[...], mg_ref[...]))


def _attn_specs(b, tq_all, tk_all, tq):
    qspec = pl.BlockSpec((1, tq, GW), lambda bi, qi: (bi, qi, 0))
    kspec = pl.BlockSpec((1, tk_all, GW), lambda bi, qi: (bi, 0, 0))
    return qspec, kspec


def _sb_call(q, k, v, mg, bd, tq, tk, pos0):
    b, tq_all, _ = q.shape
    tk_all = k.shape[1]
    qspec, kspec = _attn_specs(b, tq_all, tk_all, tq)
    full = lambda a: pl.BlockSpec(a.shape, lambda bi, qi: (0,) * a.ndim)
    u = _const_tri_rev(tk)
    u2 = jnp.concatenate([u, u], axis=0)
    return pl.pallas_call(
        functools.partial(_sb_kernel, tq=tq, tk=tk, pos0=pos0),
        grid=(b, tq_all // tq),
        in_specs=[qspec, kspec, kspec, full(u2), full(bd), full(mg)],
        out_specs=qspec,
        out_shape=jax.ShapeDtypeStruct((b, tq_all, GW), BF16),
        scratch_shapes=[pltpu.VMEM((tk_all, GW), BF16), pltpu.VMEM((tk_all // tk, N_HEADS * tk, GW), BF16),
                        pltpu.VMEM((tq, GW), F32), pltpu.VMEM((N_HEADS * tq, 128), F32)],
        compiler_params=_params(("arbitrary", "arbitrary")),
        name="stick_breaking_attention",
    )(q, k, v, u2, bd, mg)


def _fox_call(q, k, v, fq, fk, mg, bd, tq, tk, pos0):
    b, tq_all, _ = q.shape
    tk_all = k.shape[1]
    qspec, kspec = _attn_specs(b, tq_all, tk_all, tq)
    full = lambda a: pl.BlockSpec(a.shape, lambda bi, qi: (0,) * a.ndim)
    return pl.pallas_call(
        functools.partial(_fox_kernel, tq=tq, tk=tk, pos0=pos0),
        grid=(b, tq_all // tq),
        in_specs=[qspec, kspec, kspec,
                  pl.BlockSpec((1, tq, N_HEADS), lambda bi, qi: (bi, qi, 0)),
                  pl.BlockSpec((1, tk_all // tk, N_HEADS, tk), lambda bi, qi: (bi, 0, 0, 0)),
                  full(bd), full(mg)],
        out_specs=qspec,
        out_shape=jax.ShapeDtypeStruct((b, tq_all, GW), BF16),
        scratch_shapes=[pltpu.VMEM((tk_all, GW), BF16), pltpu.VMEM((tk_all // tk, N_HEADS * tk, GW), BF16),
                        pltpu.VMEM((tq, GW), F32), pltpu.VMEM((N_HEADS * tq, tk), F32)],
        compiler_params=_params(("arbitrary", "arbitrary")),
        name="forgetting_attention",
    )(q, k, v, fq, fk, bd, mg)


def _band_kernel(q_ref, k_ref, v_ref, hk_ref, hv_ref, ext_ref, bd_ref, mg_ref, o_ref,
                 kp_ref, vp_ref, bias_ref, *, lq, nch, has_hist):
    qi = pl.program_id(1)
    hmask = _head_masks()
    t = k_ref.shape[1]
    hist = hk_ref.shape[1]
    ext_w = ext_ref.shape[1]

    @pl.when(qi == 0)
    def _():
        def put(r0, kk, vv):
            n = kk.shape[0]
            kp_ref[r0:r0 + n, :] = _bf(kk)
            vp_ref[r0:r0 + n, :] = _bf(vv)
        step = min(256, t)
        for r in range(0, hist, 256):
            put(r, hk_ref[0, r:r + 256, :], hv_ref[0, r:r + 256, :])
        for r in range(0, t, step):
            put(hist + r, k_ref[0, r:r + step, :], v_ref[0, r:r + step, :])
        zeros = jnp.zeros((128, GW), F32)
        put(hist + t, zeros, zeros)
        for h in range(N_HEADS):
            e = jnp.broadcast_to(ext_ref[h:h + 1, :], (lq, ext_w))
            e = pltpu.roll(e, ext_w - (lq - 1), 1, stride=1, stride_axis=0)
            bias_ref[h * lq:(h + 1) * lq, :] = e[:, :BAND_WIN]

    col = lax.broadcasted_iota(jnp.int32, (N_HEADS * lq, BAND_WIN), 1)
    chunks = range(nch)
    starts = [pl.multiple_of((qi * nch + ci) * lq, lq) for ci in chunks]
    s_all = [_dot_nt(_stack_heads(q_ref[0, ci * lq:(ci + 1) * lq, :], hmask),
                     kp_ref[pl.ds(starts[ci], BAND_WIN), :]) for ci in chunks]
    p_all = []
    for ci in chunks:
        valid = col < hist + lq
        if not has_hist:
            valid = jnp.logical_and(valid, col + (qi * nch + ci) * lq >= hist)
        s = jnp.where(valid, s_all[ci] + bias_ref[...], NEG)
        p = jnp.exp(s - jnp.max(s, axis=1, keepdims=True))
        p_all.append(_bf(p / jnp.sum(p, axis=1, keepdims=True)))
    pv_all = [jnp.dot(p_all[ci], vp_ref[pl.ds(starts[ci], BAND_WIN), :], preferred_element_type=F32)
              for ci in chunks]
    for ci in chunks:
        acc = jnp.zeros((lq, GW), F32)
        for h in range(N_HEADS):
            acc = acc + jnp.where(hmask[h], pv_all[ci][h * lq:(h + 1) * lq], 0.0)
        o_ref[0, ci * lq:(ci + 1) * lq, :] = _bf(_head_rms(acc, bd_ref[...], mg_ref[...]))


def _band_call(q, k, v, hk, hv, ext, mg, bd, lq, nch, has_hist):
    b, t, _ = q.shape
    hist = hk.shape[1]
    tqb = lq * nch
    qspec = pl.BlockSpec((1, tqb, GW), lambda bi, qi: (bi, qi, 0))
    kspec = pl.BlockSpec((1, t, GW), lambda bi, qi: (bi, 0, 0))
    hspec = pl.BlockSpec((1, hist, GW), lambda bi, qi: (bi, 0, 0))
    full = lambda a: pl.BlockSpec(a.shape, lambda bi, qi: (0,) * a.ndim)
    rows = hist + t + 128
    return pl.pallas_call(
        functools.partial(_band_kernel, lq=lq, nch=nch, has_hist=has_hist),
        grid=(b, t // tqb),
        in_specs=[qspec, kspec, kspec, hspec, hspec, full(ext), full(bd), full(mg)],
        out_specs=qspec,
        out_shape=jax.ShapeDtypeStruct((b, t, GW), BF16),
        scratch_shapes=[pltpu.VMEM((rows, GW), BF16), pltpu.VMEM((rows, GW), BF16),
                        pltpu.VMEM((N_HEADS * lq, BAND_WIN), F32)],
        compiler_params=_params(("arbitrary", "arbitrary")),
        name="band_attention",
    )(q, k, v, hk, hv, ext, bd, mg)


def _tile4(x):
    return jnp.concatenate([x] * N_HEADS, axis=0)


def _gdn1_kernel(qkv_ref, small_ref, st_ref, cw_ref, alog_ref, dtb_ref,
                 lblk_ref, oblk_ref, eb_ref, eg_ref, bd_ref,
                 xu_ref, xwk_ref, qkd_ref, qg_ref, kdec_ref, egl_ref, conv_ref,
                 xbuf_ref, *, tg, t_valid, n_tiles):
    ti = pl.program_id(1)

    @pl.when(ti == 0)
    def _():
        xbuf_ref[0:8, :] = st_ref[0]

    xbuf_ref[8:8 + tg, :] = qkv_ref[0]
    y = cw_ref[3:4, :] * xbuf_ref[8:8 + tg, :]
    for i in range(3):
        y = y + cw_ref[i:i + 1, :] * xbuf_ref[5 + i:5 + i + tg, :]
    tv = t_valid - (n_tiles - 1) * tg

    @pl.when(ti == n_tiles - 1)
    def _():
        conv_ref[0] = xbuf_ref[8 + tv - 3:8 + tv, :]

    xbuf_ref[0:8, :] = xbuf_ref[tg:tg + 8, :]
    y = _silu(y)
    bd = bd_ref[...]
    q, k, v = y[:, :GW], y[:, GW:2 * GW], y[:, 2 * GW:]
    qn = q * lax.rsqrt(_dot_exact_rhs(q * q, bd, 2) + EPS) * (HEAD_DIM ** -0.5)
    kn = k * lax.rsqrt(_dot_exact_rhs(k * k, bd, 2) + EPS)

    sm = small_ref[0]
    rowv = ti * tg + lax.broadcasted_iota(jnp.int32, (tg, 1), 0) < t_valid
    beta_all = jnp.where(rowv, _sigmoid(sm), 0.0)
    g_all = jnp.where(rowv, -jnp.exp(alog_ref[...]) * _softplus(sm + dtb_ref[...]), 0.0)
    gc = _dot_exact_lhs(lblk_ref[...], g_all, 3)
    gl = _dot_exact_lhs(oblk_ref[...], g_all, 3)
    bexp = _dot_exact_rhs(beta_all, eb_ref[...], 3)
    gx = _dot_exact_rhs(gc, eg_ref[...], 3)
    glx = _dot_exact_rhs(gl, eg_ref[...], 3)
    eg = jnp.exp(gx)
    kb = kn * bexp
    vb = v * bexp
    kbg = kb * eg
    qg_ref[0] = _bf(qn * eg)
    kdec_ref[0] = _bf(kn * jnp.exp(glx - gx))
    egl = jnp.exp(glx)

    ri = lax.broadcasted_iota(jnp.int32, (CHUNK, GW), 0)
    li = lax.broadcasted_iota(jnp.int32, (CHUNK, GW), 1) & (CHUNK - 1)
    eye = ri == li
    incl = ri >= li
    strict = ri > li
    rb = lax.broadcasted_iota(jnp.int32, (GW, GW), 0) >> 6
    bdm = rb == (lax.broadcasted_iota(jnp.int32, (GW, GW), 1) >> 6)
    bdm2 = jnp.concatenate([bdm, bdm], axis=1)

    def bd_weights(m):
        return [jnp.where(bdm, _tile4(part), jnp.zeros((), BF16)) for part in _split(m, 2)]

    def times_bd(a, m):
        ah, al = _split(a, 2)
        mh, ml = bd_weights(m)
        n = a.shape[0]
        top = jnp.dot(jnp.concatenate([ah, al], axis=0), mh, preferred_element_type=F32)
        return top[:n] + top[n:] + jnp.dot(ah, ml, preferred_element_type=F32)

    chunks = range(tg // CHUNK)
    sls = [slice(c * CHUNK, (c + 1) * CHUNK) for c in chunks]
    kq = [_dot_nt(jnp.concatenate([kb[sl], qn[sl]], axis=0), jnp.where(bdm, _tile4(kn[sl]), 0.0))
          for sl in sls]
    nmat, tmat = [], []
    for c in chunks:
        gxc = gx[sls[c]]
        grow = jnp.sum(jnp.where(eye, gxc, 0.0), axis=0, keepdims=True)
        dec = jnp.where(incl, jnp.exp(gxc - grow), 0.0)
        nmat.append(jnp.where(strict, kq[c][:CHUNK] * dec, 0.0))
        qkd_ref[0, sls[c], :] = _bf(kq[c][CHUNK:] * dec)
        tmat.append(jnp.where(eye, 1.0, 0.0) - nmat[c])
        egl_ref[0, c] = egl[c * CHUNK:c * CHUNK + 8, :]
    pmat = [times_bd(nmat[c], nmat[c]) for c in chunks]
    for step in range(5):
        if step < 4:
            tp = [times_bd(jnp.concatenate([tmat[c], pmat[c]], axis=0), pmat[c]) for c in chunks]
            tmat = [tmat[c] + tp[c][:CHUNK] for c in chunks]
            pmat = [tp[c][CHUNK:] for c in chunks]
        else:
            tmat = [tmat[c] + times_bd(tmat[c], pmat[c]) for c in chunks]
    xs = [_dot(tmat[c], jnp.where(bdm2, _tile4(jnp.concatenate([vb[sls[c]], kbg[sls[c]]], axis=1)), 0.0))
          for c in chunks]
    for c in chunks:
        xu_ref[0, sls[c], :] = xs[c][:, :GW]
        xwk_ref[0, sls[c], :] = _bf(xs[c][:, GW:])


def _gdn2_kernel(xu_ref, xwk_ref, qkd_ref, qg_ref, kdec_ref, egl_ref, z_ref, s0_ref, ng_ref, bd_ref,
                 o_ref, sfin_ref, s_ref, *, bb, nchunk):
    ti = pl.program_id(1)

    @pl.when(ti == 0)
    def _():
        s_ref[...] = s0_ref[...]

    rb = lax.broadcasted_iota(jnp.int32, (GW, GW), 0) >> 6
    bdm = rb == (lax.broadcasted_iota(jnp.int32, (GW, GW), 1) >> 6)
    bd = bd_ref[...]
    ng = ng_ref[...]

    def chunk(c, carry):
        r0 = pl.multiple_of(c * CHUNK, CHUNK)
        rows = pl.ds(r0, CHUNK)
        for b in range(bb):
            s = s_ref[b]
            lhs = jnp.concatenate([xwk_ref[b, rows, :], qg_ref[b, rows, :]], axis=0)
            r = jnp.dot(lhs, _bf(s), preferred_element_type=F32)
            vnew = xu_ref[b, rows, :] - r[:CHUNK]
            vbd = jnp.where(bdm, _tile4(vnew), 0.0)
            o = r[CHUNK:] + jnp.dot(qkd_ref[b, rows, :], _bf(vbd), preferred_element_type=F32)
            kd_t = kdec_ref[b, rows, :].astype(F32).T
            upd = _dot(kd_t, vnew)
            s_ref[b] = egl_ref[b, c, 0:1, :] * s + jnp.where(bdm, upd, 0.0)
            zz = z_ref[b, rows, :]
            o_ref[b, rows, :] = _bf(_head_rms(o, bd, ng) * _silu(zz))
        return carry

    lax.fori_loop(0, nchunk, chunk, 0)
    sfin_ref[...] = s_ref[...]


def _gdn_consts(tg):
    i = np.arange(tg)
    same = i[:, None] // CHUNK == i[None, :] // CHUNK
    lblk = jnp.asarray(same & (i[:, None] >= i[None, :]), BF16)
    oblk = jnp.asarray(same, BF16)
    eb = np.zeros((128, GW), np.float32)
    eg = np.zeros((128, GW), np.float32)
    for h in range(N_HEADS):
        eb[h, h * HEAD_DIM:(h + 1) * HEAD_DIM] = 1
        eg[N_HEADS + h, h * HEAD_DIM:(h + 1) * HEAD_DIM] = 1
    return lblk, oblk, jnp.asarray(eb, BF16), jnp.asarray(eg, BF16)


def _gdn_call(qkv, small, z, st8, s0bd, cw, alog, dtb, ng, bd, tg, t_valid, bb):
    b, t, _ = qkv.shape
    n_tiles = t // tg
    lblk, oblk, eb, eg = _gdn_consts(tg)
    full = lambda a: pl.BlockSpec(a.shape, lambda bi, ti: (0,) * a.ndim)
    row = lambda n: pl.BlockSpec((1, tg, n), lambda bi, ti: (bi, ti, 0))
    nct = tg // CHUNK
    xu, xwk, qkd, qg, kdec, egl, conv_new = pl.pallas_call(
        functools.partial(_gdn1_kernel, tg=tg, t_valid=t_valid, n_tiles=n_tiles),
        grid=(b, n_tiles),
        in_specs=[row(3 * GW), row(128),
                  pl.BlockSpec((1, 8, 3 * GW), lambda bi, ti: (bi, 0, 0)),
                  full(cw), full(alog), full(dtb), full(lblk), full(oblk), full(eb), full(eg), full(bd)],
        out_specs=[row(GW), row(GW), row(GW), row(GW), row(GW),
                   pl.BlockSpec((1, nct, 8, GW), lambda bi, ti: (bi, ti, 0, 0)),
                   pl.BlockSpec((1, 3, 3 * GW), lambda bi, ti: (bi, 0, 0))],
        out_shape=[jax.ShapeDtypeStruct((b, t, GW), F32), jax.ShapeDtypeStruct((b, t, GW), BF16),
                   jax.ShapeDtypeStruct((b, t, GW), BF16), jax.ShapeDtypeStruct((b, t, GW), BF16),
                   jax.ShapeDtypeStruct((b, t, GW), BF16),
                   jax.ShapeDtypeStruct((b, t // CHUNK, 8, GW), F32),
                   jax.ShapeDtypeStruct((b, 3, 3 * GW), F32)],
        scratch_shapes=[pltpu.VMEM((8 + tg, 3 * GW), F32)],
        compiler_params=_params(("arbitrary", "arbitrary")),
        name="gdn_chunk_solve",
    )(qkv, small, st8, cw, alog, dtb, lblk, oblk, eb, eg, bd)

    full2 = lambda a: pl.BlockSpec(a.shape, lambda bi, ti: (0,) * a.ndim)
    rowb = pl.BlockSpec((bb, tg, GW), lambda bi, ti: (bi, ti, 0))
    sspec = pl.BlockSpec((bb, GW, GW), lambda bi, ti: (bi, 0, 0))
    o, sfin = pl.pallas_call(
        functools.partial(_gdn2_kernel, bb=bb, nchunk=nct),
        grid=(b // bb, n_tiles),
        in_specs=[rowb, rowb, rowb, rowb, rowb,
                  pl.BlockSpec((bb, nct, 8, GW), lambda bi, ti: (bi, ti, 0, 0)),
                  rowb, sspec, full2(ng), full2(bd)],
        out_specs=[rowb, sspec],
        out_shape=[jax.ShapeDtypeStruct((b, t, GW), BF16), jax.ShapeDtypeStruct((b, GW, GW), F32)],
        scratch_shapes=[pltpu.VMEM((bb, GW, GW), F32)],
        compiler_params=_params(("arbitrary", "arbitrary")),
        name="gdn_recurrence",
    )(xu, xwk, qkd, qg, kdec, egl, z, s0bd, ng, bd)
    return o, conv_new, sfin


def _out_kernel(x_ref, oa_ref, ob_ref, oc_ref, od_ref, ga1_ref, sc2_ref, sh2_ref, ga2_ref, g2_ref,
                wo_ref, wup_ref, cw_ref, wdn_ref, st_ref, y_ref, cnew_ref,
                ubuf_ref, car_ref, gs_ref, *, tm, tiles_per_seq, nff):
    i = pl.program_id(0)

    @pl.when(i % tiles_per_seq == 0)
    def _():
        car_ref[...] = st_ref[0]

    mix = jnp.dot(oa_ref[...], wo_ref[0:GW, :], preferred_element_type=F32)
    for n, ref in enumerate((ob_ref, oc_ref, od_ref)):
        mix = mix + jnp.dot(ref[...], wo_ref[(n + 1) * GW:(n + 2) * GW, :], preferred_element_type=F32)
    x1 = x_ref[...] + ga1_ref[0] * mix
    ms = jnp.mean(x1 * x1, axis=-1, keepdims=True)
    h = x1 * lax.rsqrt(ms + EPS) * g2_ref[...]
    h = _bf(h * (1.0 + sc2_ref[0]) + sh2_ref[0])
    dff = nff * GW
    for j in range(nff):
        halves = []
        for off in (0, dff):
            cols = slice(off + j * GW, off + (j + 1) * GW)
            u = jnp.dot(h, wup_ref[:, cols], preferred_element_type=F32)
            ubuf_ref[0:8, :] = car_ref[:, cols]
            ubuf_ref[8:8 + tm, :] = u
            halves.append(cw_ref[0:1, cols] * ubuf_ref[6:6 + tm, :] + cw_ref[1:2, cols] * ubuf_ref[7:7 + tm, :]
                          + cw_ref[2:3, cols] * u)
            car_ref[:, cols] = ubuf_ref[tm:tm + 8, :]
        gs_ref[:, j * GW:(j + 1) * GW] = _bf(_silu(halves[0]) * halves[1])
    y_ref[...] = x1 + ga2_ref[0] * jnp.dot(gs_ref[...], wdn_ref[...], preferred_element_type=F32)
    cnew_ref[0] = car_ref[...]


def _out_call(x2, oa, ob, oc, od, ga1, sc2, sh2, ga2, g2, wo, wup, cw, wdn, st8, tm, rows_per_seq):
    m, d = x2.shape
    dff2 = wup.shape[1]
    nff = dff2 // (2 * GW)
    tiles_per_seq = rows_per_seq // tm
    nseq = m // rows_per_seq
    r = ga1.shape[1]
    rows_per_mod = m // ga1.shape[0]
    tiles_per_mod = rows_per_mod // tm
    mod_spec = pl.BlockSpec((1, r, d), lambda i: (i // tiles_per_mod, 0, 0))
    full = lambda a: pl.BlockSpec(a.shape, lambda i: (0,) * a.ndim)
    row = lambda n: pl.BlockSpec((tm, n), lambda i: (i, 0))
    stspec = pl.BlockSpec((1, 8, dff2), lambda i: (i // tiles_per_seq, 0, 0))
    return pl.pallas_call(
        functools.partial(_out_kernel, tm=tm, tiles_per_seq=tiles_per_seq, nff=nff),
        grid=(m // tm,),
        in_specs=[row(d), row(GW), row(GW), row(GW), row(GW), mod_spec, mod_spec, mod_spec, mod_spec,
                  full(g2), full(wo), full(wup), full(cw), full(wdn), stspec],
        out_specs=[row(d), stspec],
        out_shape=[jax.ShapeDtypeStruct((m, d), F32), jax.ShapeDtypeStruct((nseq, 8, dff2), F32)],
        scratch_shapes=[pltpu.VMEM((8 + tm, GW), F32), pltpu.VMEM((8, dff2), F32),
                        pltpu.VMEM((tm, dff2 // 2), BF16)],
        compiler_params=_params(("arbitrary",)),
        name="out_projection_mlp",
    )(x2, oa, ob, oc, od, ga1, sc2, sh2, ga2, g2, wo, wup, cw, wdn, st8)


def _pack_in_weights(w_in):
    a = 4 * GW + 2 * N_HEADS
    off_b = a
    off_c = off_b + 3 * GW
    off_d = off_c + 3 * GW + N_HEADS
    dd = w_in.shape[:2]
    zeros = lambda n: jnp.zeros(dd + (n,), w_in.dtype)
    small = jnp.concatenate([w_in[..., 4 * GW:a], zeros(128 - 2 * N_HEADS),
                             w_in[..., off_c + 3 * GW:off_d], zeros(128 - N_HEADS)], axis=-1)
    packed = jnp.concatenate([w_in[..., :4 * GW], w_in[..., off_b:off_b + 3 * GW],
                              w_in[..., off_c:off_c + 3 * GW], w_in[..., off_d:off_d + 3 * GW], small],
                             axis=-1)
    return _bf(packed)


def _lane_vec(v, offset):
    depth = v.shape[0]
    out = jnp.zeros((depth, 1, 128), F32)
    return out.at[:, 0, offset:offset + N_HEADS].set(v.astype(F32))


def _band_ext(rel_table, lq, hist):
    c = np.arange(BAND_EXT)
    rel = np.clip(hist + lq - 1 - c, -REL_CLIP, REL_CLIP) + REL_CLIP
    return rel_table.astype(F32)[:, rel]


def _layer(x, mods_in, mods_out, wts, cache, cfg):
    b, t, d = x.shape
    m = b * t
    bd = wts["bd"]
    sc1, sh1 = mods_in
    (gqkv, gz, small, sbq, sbk, sbv, fq, fk, fv, logf, bq, bk, bv) = _in_call(
        x.reshape(m, d), sc1, sh1, wts["norm_mix_g"], wts["w_in"], wts["qk_gains"], wts["fox_b_f"], bd,
        cfg["tm_in"], cfg["rows_per_mod_in"])
    r3 = lambda a: a.reshape(b, t, a.shape[-1])
    new = {"sb_k": r3(sbk), "sb_v": r3(sbv), "fox_k": r3(fk), "fox_v": r3(fv), "fox_logf": r3(logf)}
    past = cache is not None

    tp = cfg["t_gdn"]
    pad_t = lambda a: jnp.pad(r3(a), ((0, 0), (0, tp - t), (0, 0)))
    if past:
        st8 = jnp.pad(cache["gdn_conv"], ((0, 0), (5, 0), (0, 0)))
        s0 = cache["gdn_state"]
        eye = jnp.eye(N_HEADS, dtype=F32)
        s0bd = jnp.einsum("bhkv,hg->bhkgv", s0, eye).reshape(b, GW, GW)
    else:
        st8 = jnp.zeros((b, 8, 3 * GW), F32)
        s0bd = jnp.zeros((b, GW, GW), F32)
    oa, conv_new, sfin = _gdn_call(pad_t(gqkv), pad_t(small), pad_t(gz), st8, s0bd,
                                   wts["gdn_conv_w"], wts["gdn_a_log"], wts["gdn_dt_bias"],
                                   wts["gdn_norm_g"], bd, cfg["tg"], t, cfg["bb"])
    oa = oa[:, :t].reshape(m, GW)
    sfin = sfin.reshape(b, N_HEADS, HEAD_DIM, N_HEADS, HEAD_DIM)
    new["gdn_state"] = jnp.stack([sfin[:, h, :, h, :] for h in range(N_HEADS)], axis=1)
    new["gdn_conv"] = conv_new

    tq, tk = cfg["tq"], cfg["tk"]
    if past:
        pos0 = cache["sb_k"].shape[1]
        tk_all = -(-(pos0 + t) // tk) * tk
        cat = lambda c, n: jnp.pad(jnp.concatenate([c.reshape(b, pos0, -1), r3(n)], axis=1),
                                   ((0, 0), (0, tk_all - pos0 - t), (0, 0)))
        ksb, vsb = cat(cache["sb_k"], sbk), cat(cache["sb_v"], sbv)
        kfx, vfx = cat(cache["fox_k"], fk), cat(cache["fox_v"], fv)
        lf_all = cat(cache["fox_logf"], logf)
    else:
        pos0, tk_all = 0, t
        ksb, vsb, kfx, vfx, lf_all = r3(sbk), r3(sbv), r3(fk), r3(fv), r3(logf)
    ob = _sb_call(r3(sbq), ksb, vsb, wts["merge_g"][0:1], bd, tq, tk, pos0)
    fcum = _cumsum_call(jnp.swapaxes(lf_all, 1, 2).reshape(b * N_HEADS, tk_all))
    fcum = fcum.reshape(b, N_HEADS, tk_all)
    f_q = jnp.swapaxes(fcum[:, :, pos0:pos0 + t], 1, 2)
    f_k = jnp.swapaxes(fcum.reshape(b, N_HEADS, tk_all // tk, tk), 1, 2)
    oc = _fox_call(r3(fq), kfx, vfx, f_q, f_k, wts["merge_g"][1:2], bd, tq, tk, pos0)

    if past:
        hk, hv = cache["band_k"].reshape(b, -1, GW), cache["band_v"].reshape(b, -1, GW)
        keep = hk.shape[1]
        new["band_k"] = jnp.concatenate([hk, r3(bk)], axis=1)[:, -keep:]
        new["band_v"] = jnp.concatenate([hv, r3(bv)], axis=1)[:, -keep:]
    else:
        hk = hv = jnp.zeros((b, BAND_ROWS, GW), F32)
        keep = min(BAND_ROWS, t)
        new["band_k"], new["band_v"] = r3(bk)[:, -keep:], r3(bv)[:, -keep:]
    od = _band_call(r3(bq), r3(bk), r3(bv), hk, hv, wts["band_ext_" + cfg["name"]],
                    wts["merge_g"][2:3], bd, cfg["lq"], cfg["nch"], past)

    if past:
        st_ffn = jnp.pad(cache["ffn_conv"], ((0, 0), (6, 0), (0, 0)))
    else:
        st_ffn = jnp.zeros((b, 8, wts["w_up"].shape[1]), F32)
    ga1, sc2, sh2, ga2 = mods_out
    y, cnew = _out_call(x.reshape(m, d), oa, ob.reshape(m, GW), oc.reshape(m, GW), od.reshape(m, GW),
                        ga1, sc2, sh2, ga2, wts["norm_ffn_g"], wts["w_o"], wts["w_up"],
                        wts["ffn_conv_w"], wts["w_down"], st_ffn, cfg["tm_out"], t)
    new["ffn_conv"] = cnew[:, 6:8]
    hd = lambda a: a.reshape(a.shape[0], a.shape[1], N_HEADS, HEAD_DIM)
    for n in ("sb_k", "sb_v", "fox_k", "fox_v", "band_k", "band_v"):
        new[n] = hd(new[n])
    return y.reshape(b, t, d), new


STATE_KEYS = ("gdn_conv", "gdn_state", "sb_k", "sb_v", "fox_k", "fox_v", "fox_logf",
              "band_k", "band_v", "ffn_conv")


def _group_cfg(name, b, t):
    if t % 512 == 0:
        return dict(name=name, tm_in=512, rows_per_mod_in=t, tm_out=512, t_gdn=t, tg=512,
                    bb=4 if b % 4 == 0 else 1, tq=256, tk=256, lq=CHUNK, nch=4)
    assert t % 16 == 0 and t <= CHUNK
    return dict(name=name, tm_in=b * t, rows_per_mod_in=b * t, tm_out=t, t_gdn=CHUNK, tg=CHUNK,
                bb=4 if b % 4 == 0 else 1, tq=t, tk=256, lq=t, nch=1)


def kernel(x_prompt, x_sample, c_prompt, c_sample, state_gdn_conv, state_gdn, cache_sb_k, cache_sb_v, cache_fox_k, cache_fox_v, cache_fox_logf, cache_band_k, cache_band_v, state_ffn_conv, ada_w, ada_b, norm_mix_g, w_in, gdn_conv_w, gdn_a_log, gdn_dt_bias, gdn_norm_g, fox_q_g, fox_k_g, fox_b_f, band_q_g, band_k_g, band_rel_bias, merge_g, w_o, norm_ffn_g, w_up, ffn_conv_w, w_down):
    depth = ada_w.shape[0]
    bp, tp, d = x_prompt.shape
    bs, ts, _ = x_sample.shape
    cfg_p = _group_cfg("p", bp, tp)
    cfg_s = _group_cfg("s", bs, ts)
    hist_s = cache_band_k.shape[2]

    mod = _ada_call(jnp.concatenate([c_prompt, c_sample], axis=0), ada_w, ada_b)
    mod = mod.reshape(depth, bp + bs, 6, d)

    tile_h = lambda g: jnp.tile(g.astype(F32), (1, N_HEADS))[:, None, :]
    w_in_p = _pack_in_weights(w_in)
    w_o_b, w_up_b, w_dn_b = _bf(w_o), _bf(w_up), _bf(w_down)
    ffn_cw = jnp.pad(ffn_conv_w.astype(F32), ((0, 0), (0, 5), (0, 0)))
    gdn_cw = jnp.pad(gdn_conv_w.astype(F32), ((0, 0), (0, 4), (0, 0)))
    qk_gains = jnp.concatenate([tile_h(fox_q_g), tile_h(fox_k_g), tile_h(band_q_g), tile_h(band_k_g),
                                jnp.zeros((depth, 4, GW), F32)], axis=1)
    bd = _const_bd256()

    y_p, y_s = x_prompt, x_sample
    new_p = {n: [] for n in STATE_KEYS}
    new_s = {n: [] for n in STATE_KEYS}
    for l in range(depth):
        wts = {
            "bd": bd, "norm_mix_g": norm_mix_g[l][None].astype(F32), "w_in": w_in_p[l],
            "qk_gains": qk_gains[l], "fox_b_f": _lane_vec(fox_b_f, 0)[l],
            "gdn_conv_w": gdn_cw[l], "gdn_a_log": _lane_vec(gdn_a_log, N_HEADS)[l],
            "gdn_dt_bias": _lane_vec(gdn_dt_bias, N_HEADS)[l], "gdn_norm_g": tile_h(gdn_norm_g)[l],
            "merge_g": merge_g[l].reshape(3, GW).astype(F32),
            "band_ext_p": _band_ext(band_rel_bias[l], cfg_p["lq"], BAND_ROWS),
            "band_ext_s": _band_ext(band_rel_bias[l], cfg_s["lq"], hist_s),
            "w_o": w_o_b[l], "norm_ffn_g": norm_ffn_g[l][None].astype(F32), "w_up": w_up_b[l],
            "ffn_conv_w": ffn_cw[l], "w_down": w_dn_b[l],
        }
        mp = mod[l, :bp]
        ms = mod[l, bp:]
        pm = lambda i: mp[:, i][:, None, :]
        mods_in_p = (pm(1), pm(0))
        mods_out_p = (pm(2), pm(4), pm(3), pm(5))
        srow = lambda i: jnp.repeat(ms[:, i], ts, axis=0)[None]
        sm1 = lambda i: ms[:, i][:, None, :]
        mods_in_s = (srow(1), srow(0))
        mods_out_s = (sm1(2), sm1(4), sm1(3), sm1(5))
        cache = {"gdn_conv": state_gdn_conv[l], "gdn_state": state_gdn[l],
                 "sb_k": cache_sb_k[l], "sb_v": cache_sb_v[l], "fox_k": cache_fox_k[l],
                 "fox_v": cache_fox_v[l], "fox_logf": cache_fox_logf[l],
                 "band_k": cache_band_k[l], "band_v": cache_band_v[l], "ffn_conv": state_ffn_conv[l]}
        y_p, st_p = _layer(y_p, mods_in_p, mods_out_p, wts, None, cfg_p)
        y_s, st_s = _layer(y_s, mods_in_s, mods_out_s, wts, cache, cfg_s)
        for n in STATE_KEYS:
            new_p[n].append(st_p[n])
            new_s[n].append(st_s[n])
    return ((y_p, y_s) + tuple(jnp.stack(new_p[n]) for n in STATE_KEYS)
            + tuple(jnp.stack(new_s[n]) for n in STATE_KEYS))
```

```python
import functools

import numpy as np
import jax
import jax.numpy as jnp
from jax import lax
from jax.experimental import pallas as pl
from jax.experimental.pallas import tpu as pltpu

F32 = jnp.float32
BF16 = jnp.bfloat16

HEAD_DIM = 64
N_HEADS = 4
GW = N_HEADS * HEAD_DIM
CHUNK = 64
BAND_ROWS = 512
REL_CLIP = 2 * CHUNK
EPS = 1e-6
NEG = -1e30
LOG2E = 1.4426950408889634
VMEM_LIMIT = 56 * 1024 * 1024
BAND_WIN = BAND_ROWS + 128
BAND_EXT = BAND_WIN + 128


def _bf(x):
    return x.astype(BF16)


def _dot(a, b):
    return jnp.dot(_bf(a), _bf(b), preferred_element_type=F32)


def _dot_nt(a, b):
    return lax.dot_general(_bf(a), _bf(b), (((1,), (1,)), ((), ())), preferred_element_type=F32)


def _split(x, n):
    parts = []
    for _ in range(n - 1):
        hi = _bf(x)
        parts.append(hi)
        x = x - hi.astype(F32)
    parts.append(_bf(x))
    return parts


def _dot_exact_rhs(x, m, n):
    return sum(jnp.dot(p, m, preferred_element_type=F32) for p in _split(x, n))


def _dot_exact_lhs(m, x, n):
    return sum(jnp.dot(m, p, preferred_element_type=F32) for p in _split(x, n))


def _sigmoid(x):
    return 1.0 / (1.0 + jnp.exp(-x))


def _silu(x):
    return x * _sigmoid(x)


def _softplus(x):
    return jnp.maximum(x, 0.0) + jnp.log(1.0 + jnp.exp(-jnp.abs(x)))


def _head_rms(o, bd, gain):
    ss = _dot_exact_rhs(o * o, bd, 2)
    return o * lax.rsqrt(ss * (1.0 / HEAD_DIM) + EPS) * gain


def _head_masks(width=GW):
    lane = lax.broadcasted_iota(jnp.int32, (1, width), 1)
    return [((lane & (GW - 1)) >> 6) == h for h in range(N_HEADS)]


def _params(sem, **kw):
    return pltpu.CompilerParams(dimension_semantics=sem, vmem_limit_bytes=VMEM_LIMIT, **kw)


def _np_bd(n, blk):
    i = np.arange(n)
    return (i[:, None] // blk == i[None, :] // blk)


def _const_bd256():
    return jnp.asarray(_np_bd(GW, HEAD_DIM), BF16)


def _const_tri_fwd(n):
    i = np.arange(n)
    return jnp.asarray(i[:, None] <= i[None, :], BF16)


def _const_tri_rev(n):
    i = np.arange(n)
    return jnp.asarray(i[:, None] >= i[None, :], BF16)


def _ada_kernel(c_ref, w_ref, b_ref, o_ref):
    s = _silu(c_ref[...])
    o_ref[0] = _dot(s, w_ref[0]) + b_ref[0]


def _ada_call(c_all, ada_w, ada_b):
    depth, d, n = ada_w.shape
    rows = c_all.shape[0]
    tn = 1536
    return pl.pallas_call(
        _ada_kernel,
        grid=(depth, n // tn),
        in_specs=[pl.BlockSpec((rows, d), lambda l, j: (0, 0)),
                  pl.BlockSpec((1, d, tn), lambda l, j: (l, 0, j)),
                  pl.BlockSpec((1, 1, tn), lambda l, j: (l, 0, j))],
        out_specs=pl.BlockSpec((1, rows, tn), lambda l, j: (l, 0, j)),
        out_shape=jax.ShapeDtypeStruct((depth, rows, n), F32),
        compiler_params=_params(("arbitrary", "arbitrary")),
        name="ada_modulation",
    )(c_all, ada_w, ada_b.reshape(depth, 1, n))


IN_GROUPS = 14
IN_COLS_PACKED = IN_GROUPS * GW


def _in_kernel(x_ref, sc_ref, sh_ref, g_ref, w_ref, gains_ref, bf_ref, bd_ref,
               gqkv_ref, gz_ref, small_ref, sbq_ref, sbk_ref, sbv_ref,
               fq_ref, fk_ref, fv_ref, logf_ref, bq_ref, bk_ref, bv_ref):
    x = x_ref[...]
    ms = jnp.mean(x * x, axis=-1, keepdims=True)
    h = x * lax.rsqrt(ms + EPS) * g_ref[...]
    h = _bf(h * (1.0 + sc_ref[0]) + sh_ref[0])
    bd = bd_ref[...]
    scale = HEAD_DIM ** -0.5

    def col(j, n=1):
        return jnp.dot(h, w_ref[:, j * GW:(j + n) * GW], preferred_element_type=F32)

    gqkv_ref[...] = col(0, 3)
    gz_ref[...] = col(3)
    sbq_ref[...] = _bf(col(4) * scale)
    sbk_ref[...] = col(5)
    sbv_ref[...] = col(6)
    fq_ref[...] = _bf(_head_rms(col(7), bd, gains_ref[0:1, :]) * scale)
    fk_ref[...] = _head_rms(col(8), bd, gains_ref[1:2, :])
    fv_ref[...] = col(9)
    bq_ref[...] = _bf(_head_rms(col(10), bd, gains_ref[2:3, :]) * scale)
    bk_ref[...] = _head_rms(col(11), bd, gains_ref[3:4, :])
    bv_ref[...] = col(12)
    sm = col(13)
    small_ref[...] = sm[:, :128]
    lf = sm[:, 128:] + bf_ref[...]
    logf = jnp.minimum(lf, 0.0) - jnp.log(1.0 + jnp.exp(-jnp.abs(lf)))
    logf_ref[...] = logf[:, :N_HEADS]


STACKED_OUTS = (4, 5, 7, 8)


def _in_kernel_aliased(*refs):
    n_in = 8
    _in_kernel(*refs[:n_in], *refs[n_in + len(STACKED_OUTS):])


def _in_call(x2, sc, sh, g, w, gains, bfv, bd, tm, rows_per_mod, stack=None):
    m, d = x2.shape
    r = sc.shape[1]
    tiles_per_mod = rows_per_mod // tm
    mod_spec = pl.BlockSpec((1, r, d), lambda i: (i // tiles_per_mod, 0, 0))
    full = lambda a: pl.BlockSpec(a.shape, lambda i: (0,) * a.ndim)
    row = lambda n: pl.BlockSpec((tm, n), lambda i: (i, 0))
    out_defs = [(3 * GW, F32), (GW, F32), (128, F32), (GW, BF16), (GW, F32), (GW, F32),
                (GW, BF16), (GW, F32), (GW, F32), (N_HEADS, F32), (GW, BF16), (GW, F32), (GW, F32)]
    out_specs = [row(n) for n, _ in out_defs]
    out_shape = [jax.ShapeDtypeStruct((m, n), dt) for n, dt in out_defs]
    in_specs = [row(d), mod_spec, mod_spec, full(g), full(w), full(gains), full(bfv), full(bd)]
    args = [x2, sc, sh, g, w, gains, bfv, bd]
    body, aliases = _in_kernel, {}
    if stack is not None:
        depth, layer, bufs = stack
        for o in STACKED_OUTS:
            out_specs[o] = pl.BlockSpec((None, tm, GW), lambda i: (layer, i, 0))
            out_shape[o] = jax.ShapeDtypeStruct((depth, m, GW), F32)
        if bufs is not None:
            body = _in_kernel_aliased
            in_specs += [pl.BlockSpec(memory_space=pl.ANY)] * len(STACKED_OUTS)
            aliases = {len(args) + n: o for n, o in enumerate(STACKED_OUTS)}
            args += list(bufs)
    return pl.pallas_call(
        body,
        grid=(m // tm,),
        in_specs=in_specs,
        out_specs=out_specs,
        out_shape=out_shape,
        input_output_aliases=aliases,
        compiler_params=_params(("arbitrary",)),
        name="in_projection",
    )(*args)


def _cumsum_kernel(x_ref, u_ref, o_ref):
    r, tk = x_ref.shape
    carry = jnp.zeros((r, 1), F32)
    for blk in range(tk // 256):
        o = _dot_exact_rhs(x_ref[:, blk * 256:(blk + 1) * 256], u_ref[...], 3) + carry
        o_ref[:, blk * 256:(blk + 1) * 256] = o
        carry = o[:, 255:256]


def _cumsum_call(x):
    return pl.pallas_call(
        _cumsum_kernel,
        out_shape=jax.ShapeDtypeStruct(x.shape, F32),
        compiler_params=pltpu.CompilerParams(vmem_limit_bytes=VMEM_LIMIT),
        name="logf_cumsum",
    )(x, _const_tri_fwd(256))


def _fill_kv(k_ref, v_ref, kb_ref, vm_ref, hmask, tk):
    for kbi in range(k_ref.shape[1] // tk):
        rows = slice(kbi * tk, (kbi + 1) * tk)
        kb_ref[rows, :] = _bf(k_ref[0, rows, :])
        vv = v_ref[0, rows, :]
        for h in range(N_HEADS):
            vm_ref[kbi, h * tk:(h + 1) * tk, :] = _bf(jnp.where(hmask[h], vv, 0.0))


def _stack_heads(q, hmask):
    return jnp.concatenate([jnp.where(hmask[h], q, jnp.zeros_like(q)) for h in range(N_HEADS)], axis=0)


def _heads_to_lanes(w, tq):
    return jnp.concatenate([w[h * tq:(h + 1) * tq] for h in range(N_HEADS)], axis=1)


def _spread_heads(col, hmask, tq):
    out = col[(N_HEADS - 1) * tq:]
    for h in range(N_HEADS - 2, -1, -1):
        out = jnp.where(hmask[h], col[h * tq:(h + 1) * tq], out)
    return jnp.broadcast_to(out, (tq, GW))


def _stacked_positions(q0, kbi, tq, tk):
    rows = lax.broadcasted_iota(jnp.int32, (N_HEADS * tq, tk), 0) & (tq - 1)
    cols = lax.broadcasted_iota(jnp.int32, (N_HEADS * tq, tk), 1)
    return q0 + rows, kbi * tk + cols


def _sb_kernel(q_ref, k_ref, v_ref, u2_ref, bd_ref, mg_ref, o_ref,
               kb_ref, vm_ref, acc_ref, car_ref, *, tq, tk, pos0):
    qi = pl.program_id(1)
    hmask = _head_masks()

    @pl.when(qi == 0)
    def _():
        _fill_kv(k_ref, v_ref, kb_ref, vm_ref, hmask, tk)

    nqs = _stack_heads(-q_ref[0], hmask)
    acc_ref[...] = jnp.zeros_like(acc_ref)
    car_ref[...] = jnp.zeros_like(car_ref)
    q0 = pos0 + qi * tq
    n_full = q0 // tk

    def scores(kbi):
        start = pl.multiple_of(kbi * tk, tk)
        return _dot_nt(nqs, kb_ref[pl.ds(start, tk), :]) * LOG2E

    def log_rest(kbi, y, masked):
        neg_abs = pltpu.bitcast(pltpu.bitcast(y, jnp.uint32) | jnp.uint32(0x80000000), F32)
        r = jnp.minimum(y, 0.0) - jnp.log2(1.0 + jnp.exp2(neg_abs))
        valid = None
        if masked:
            rowp, colp = _stacked_positions(q0, kbi, tq, tk)
            valid = colp < rowp
            r = jnp.where(valid, r, 0.0)
        rc = jnp.dot(jnp.concatenate(_split(r, 2), axis=1), u2_ref[...], preferred_element_type=F32)
        return rc, valid

    def weights(kbi, y, rc, valid):
        car = car_ref[...]
        w = jnp.exp2(rc + jnp.concatenate([car] * (tk // 128), axis=1) - y)
        if valid is not None:
            w = jnp.where(valid, w, 0.0)
        car_ref[...] = car + jnp.broadcast_to(rc[:, 0:1], car.shape)
        acc_ref[...] += jnp.dot(_heads_to_lanes(_bf(w), tq), vm_ref[kbi], preferred_element_type=F32)

    def blocks(kbis, masked):
        ys = [scores(kbi) for kbi in kbis]
        pend = None
        for kbi, y in zip(kbis, ys):
            rc, valid = log_rest(kbi, y, masked)
            if pend is not None:
                weights(*pend)
            pend = (kbi, y, rc, valid)
        weights(*pend)

    blocks([n_full], True)

    def body(i, c):
        hi = n_full - 1 - 2 * i
        blocks([hi, hi - 1], False)
        return c

    lax.fori_loop(0, n_full // 2, body, 0)

    @pl.when(n_full % 2 == 1)
    def _():
        blocks([0], False)

    o_ref[0] = _bf(_head_rms(acc_ref[...], bd_ref[...], mg_ref[...]))


def _fox_kernel(q_ref, k_ref, v_ref, fq_ref, fk_ref, bd_ref, mg_ref, o_ref,
                kb_ref, vm_ref, acc_ref, fqb_ref, *, tq, tk, pos0):
    qi = pl.program_id(1)
    hmask = _head_masks()

    @pl.when(qi == 0)
    def _():
        _fill_kv(k_ref, v_ref, kb_ref, vm_ref, hmask, tk)

    qs = _stack_heads(q_ref[0], hmask)
    fq = fq_ref[0]
    for h in range(N_HEADS):
        fqb_ref[h * tq:(h + 1) * tq, :] = jnp.broadcast_to(fq[:, h:h + 1], (tq, tk))
    acc_ref[...] = jnp.zeros_like(acc_ref)
    q0 = pos0 + qi * tq
    n_full = q0 // tk

    def scores(kbi):
        start = pl.multiple_of(kbi * tk, tk)
        return _dot_nt(qs, kb_ref[pl.ds(start, tk), :])

    def update(kbi, z, masked, ml):
        m_old, l_old = ml
        fk = fk_ref[0, kbi]
        s = jnp.concatenate([z[h * tq:(h + 1) * tq] + fqb_ref[h * tq:(h + 1) * tq, :] - fk[h:h + 1, :]
                             for h in range(N_HEADS)], axis=0)
        if masked:
            rowp, colp = _stacked_positions(q0, kbi, tq, tk)
            s = jnp.where(colp <= rowp, s, NEG)
        m_new = jnp.maximum(m_old, jnp.max(s, axis=1, keepdims=True))
        alpha = jnp.exp(m_old - m_new)
        p = jnp.exp(s - m_new)
        l_new = alpha * l_old + jnp.sum(p, axis=1, keepdims=True)
        pv = jnp.dot(_heads_to_lanes(_bf(p), tq), vm_ref[kbi], preferred_element_type=F32)
        acc_ref[...] = acc_ref[...] * _spread_heads(alpha, hmask, tq) + pv
        return m_new, l_new

    def pair(i, ml):
        za, zb = scores(2 * i), scores(2 * i + 1)
        return update(2 * i + 1, zb, False, update(2 * i, za, False, ml))

    ml = (jnp.full((N_HEADS * tq, 1), NEG, F32), jnp.zeros((N_HEADS * tq, 1), F32))
    ml = lax.fori_loop(0, n_full // 2, pair, ml)
    last = n_full - 1
    ml = lax.cond(n_full % 2 == 1, lambda c: update(last, scores(last), False, c), lambda c: c, ml)
    _, l_fin = update(n_full, scores(n_full), True, ml)
    o = acc_ref[...] / _spread_heads(l_fin, hmask, tq)
    o_ref[0] = _bf(_head_rms(o, bd_ref[...], mg_ref[...]))


def _attn_specs(tk_all, tq, layer):
    qspec = pl.BlockSpec((1, tq, GW), lambda bi, qi: (bi, qi, 0))
    kspec = pl.BlockSpec((None, 1, tk_all, GW), lambda bi, qi: (layer, bi, 0, 0))
    return qspec, kspec


def _sb_call(q, k, v, layer, mg, bd, tq, tk, pos0):
    b, tq_all, _ = q.shape
    tk_all = k.shape[2]
    qspec, kspec = _attn_specs(tk_all, tq, layer)
    full = lambda a: pl.BlockSpec(a.shape, lambda bi, qi: (0,) * a.ndim)
    u = _const_tri_rev(tk)
    u2 = jnp.concatenate([u, u], axis=0)
    return pl.pallas_call(
        functools.partial(_sb_kernel, tq=tq, tk=tk, pos0=pos0),
        grid=(b, tq_all // tq),
        in_specs=[qspec, kspec, kspec, full(u2), full(bd), full(mg)],
        out_specs=qspec,
        out_shape=jax.ShapeDtypeStruct((b, tq_all, GW), BF16),
        scratch_shapes=[pltpu.VMEM((tk_all, GW), BF16), pltpu.VMEM((tk_all // tk, N_HEADS * tk, GW), BF16),
                        pltpu.VMEM((tq, GW), F32), pltpu.VMEM((N_HEADS * tq, 128), F32)],
        compiler_params=_params(("arbitrary", "arbitrary")),
        name="stick_breaking_attention",
    )(q, k, v, u2, bd, mg)


def _fox_call(q, k, v, layer, fq, fk, mg, bd, tq, tk, pos0):
    b, tq_all, _ = q.shape
    tk_all = k.shape[2]
    qspec, kspec = _attn_specs(tk_all, tq, layer)
    full = lambda a: pl.BlockSpec(a.shape, lambda bi, qi: (0,) * a.ndim)
    return pl.pallas_call(
        functools.partial(_fox_kernel, tq=tq, tk=tk, pos0=pos0),
        grid=(b, tq_all // tq),
        in_specs=[qspec, kspec, kspec,
                  pl.BlockSpec((1, tq, N_HEADS), lambda bi, qi: (bi, qi, 0)),
                  pl.BlockSpec((1, tk_all // tk, N_HEADS, tk), lambda bi, qi: (bi, 0, 0, 0)),
                  full(bd), full(mg)],
        out_specs=qspec,
        out_shape=jax.ShapeDtypeStruct((b, tq_all, GW), BF16),
        scratch_shapes=[pltpu.VMEM((tk_all, GW), BF16), pltpu.VMEM((tk_all // tk, N_HEADS * tk, GW), BF16),
                        pltpu.VMEM((tq, GW), F32), pltpu.VMEM((N_HEADS * tq, tk), F32)],
        compiler_params=_params(("arbitrary", "arbitrary")),
        name="forgetting_attention",
    )(q, k, v, fq, fk, bd, mg)


def _band_kernel(q_ref, k_ref, v_ref, hk_ref, hv_ref, ext_ref, bd_ref, mg_ref, o_ref,
                 kp_ref, vp_ref, bias_ref, *, lq, nch, has_hist):
    qi = pl.program_id(1)
    hmask = _head_masks()
    t = k_ref.shape[1]
    hist = hk_ref.shape[1]
    ext_w = ext_ref.shape[1]

    @pl.when(qi == 0)
    def _():
        def put(r0, kk, vv):
            n = kk.shape[0]
            kp_ref[r0:r0 + n, :] = _bf(kk)
            vp_ref[r0:r0 + n, :] = _bf(vv)
        step = min(256, t)
        for r in range(0, hist, 256):
            put(r, hk_ref[0, r:r + 256, :], hv_ref[0, r:r + 256, :])
        for r in range(0, t, step):
            put(hist + r, k_ref[0, r:r + step, :], v_ref[0, r:r + step, :])
        zeros = jnp.zeros((128, GW), F32)
        put(hist + t, zeros, zeros)
        for h in range(N_HEADS):
            e = jnp.broadcast_to(ext_ref[h:h + 1, :], (lq, ext_w))
            e = pltpu.roll(e, ext_w - (lq - 1), 1, stride=1, stride_axis=0)
            bias_ref[h * lq:(h + 1) * lq, :] = e[:, :BAND_WIN]

    col = lax.broadcasted_iota(jnp.int32, (N_HEADS * lq, BAND_WIN), 1)
    chunks = range(nch)
    starts = [pl.multiple_of((qi * nch + ci) * lq, lq) for ci in chunks]
    s_all = [_dot_nt(_stack_heads(q_ref[0, ci * lq:(ci + 1) * lq, :], hmask),
                     kp_ref[pl.ds(starts[ci], BAND_WIN), :]) for ci in chunks]
    p_all = []
    for ci in chunks:
        valid = col < hist + lq
        if not has_hist:
            valid = jnp.logical_and(valid, col + (qi * nch + ci) * lq >= hist)
        s = jnp.where(valid, s_all[ci] + bias_ref[...], NEG)
        p = jnp.exp(s - jnp.max(s, axis=1, keepdims=True))
        p_all.append(_bf(p / jnp.sum(p, axis=1, keepdims=True)))
    pv_all = [jnp.dot(p_all[ci], vp_ref[pl.ds(starts[ci], BAND_WIN), :], preferred_element_type=F32)
              for ci in chunks]
    for ci in chunks:
        acc = jnp.zeros((lq, GW), F32)
        for h in range(N_HEADS):
            acc = acc + jnp.where(hmask[h], pv_all[ci][h * lq:(h + 1) * lq], 0.0)
        o_ref[0, ci * lq:(ci + 1) * lq, :] = _bf(_head_rms(acc, bd_ref[...], mg_ref[...]))


def _band_call(q, k, v, hk, hv, ext, mg, bd, lq, nch, has_hist):
    b, t, _ = q.shape
    hist = hk.shape[1]
    tqb = lq * nch
    qspec = pl.BlockSpec((1, tqb, GW), lambda bi, qi: (bi, qi, 0))
    kspec = pl.BlockSpec((1, t, GW), lambda bi, qi: (bi, 0, 0))
    hspec = pl.BlockSpec((1, hist, GW), lambda bi, qi: (bi, 0, 0))
    full = lambda a: pl.BlockSpec(a.shape, lambda bi, qi: (0,) * a.ndim)
    rows = hist + t + 128
    return pl.pallas_call(
        functools.partial(_band_kernel, lq=lq, nch=nch, has_hist=has_hist),
        grid=(b, t // tqb),
        in_specs=[qspec, kspec, kspec, hspec, hspec, full(ext), full(bd), full(mg)],
        out_specs=qspec,
        out_shape=jax.ShapeDtypeStruct((b, t, GW), BF16),
        scratch_shapes=[pltpu.VMEM((rows, GW), BF16), pltpu.VMEM((rows, GW), BF16),
                        pltpu.VMEM((N_HEADS * lq, BAND_WIN), F32)],
        compiler_params=_params(("arbitrary", "arbitrary")),
        name="band_attention",
    )(q, k, v, hk, hv, ext, bd, mg)


def _tile4(x):
    return jnp.concatenate([x] * N_HEADS, axis=0)


def _gdn1_kernel(qkv_ref, small_ref, st_ref, cw_ref, alog_ref, dtb_ref,
                 lblk_ref, eb_ref, eg_ref, bd_ref,
                 xu_ref, xwk_ref, qkd_ref, qg_ref, kdec_ref, egl_ref, conv_ref,
                 xbuf_ref, *, tg, t_valid, n_tiles):
    ti = pl.program_id(1)

    @pl.when(ti == 0)
    def _():
        xbuf_ref[0:8, :] = st_ref[0]

    xbuf_ref[8:8 + tg, :] = qkv_ref[0]
    y = cw_ref[3:4, :] * xbuf_ref[8:8 + tg, :]
    for i in range(3):
        y = y + cw_ref[i:i + 1, :] * xbuf_ref[5 + i:5 + i + tg, :]
    tv = t_valid - (n_tiles - 1) * tg

    @pl.when(ti == n_tiles - 1)
    def _():
        conv_ref[0] = xbuf_ref[8 + tv - 3:8 + tv, :]

    xbuf_ref[0:8, :] = xbuf_ref[tg:tg + 8, :]
    y = _silu(y)
    bd = bd_ref[...]
    q, k, v = y[:, :GW], y[:, GW:2 * GW], y[:, 2 * GW:]
    qn = q * lax.rsqrt(_dot_exact_rhs(q * q, bd, 2) + EPS) * (HEAD_DIM ** -0.5)
    kn = k * lax.rsqrt(_dot_exact_rhs(k * k, bd, 2) + EPS)

    sm = small_ref[0]
    rowv = ti * tg + lax.broadcasted_iota(jnp.int32, (tg, 1), 0) < t_valid
    beta_all = jnp.where(rowv, _sigmoid(sm), 0.0)
    g_all = jnp.where(rowv, -jnp.exp(alog_ref[...]) * _softplus(sm + dtb_ref[...]), 0.0)
    gc = _dot_exact_lhs(lblk_ref[...], g_all, 3)
    gl = jnp.concatenate([jnp.broadcast_to(gc[c * CHUNK + CHUNK - 1:(c + 1) * CHUNK, :], (CHUNK, 128))
                          for c in range(tg // CHUNK)], axis=0)
    bexp = _dot_exact_rhs(beta_all, eb_ref[...], 3)
    gx = _dot_exact_rhs(gc, eg_ref[...], 3)
    glx = _dot_exact_rhs(gl, eg_ref[...], 3)
    eg = jnp.exp(gx)
    kb = kn * bexp
    vb = v * bexp
    kbg = kb * eg
    qg_ref[0] = _bf(qn * eg)
    kdec_ref[0] = _bf(kn * jnp.exp(glx - gx))
    egl = jnp.exp(glx)

    ri = lax.broadcasted_iota(jnp.int32, (CHUNK, GW), 0)
    li = lax.broadcasted_iota(jnp.int32, (CHUNK, GW), 1) & (CHUNK - 1)
    eye = ri == li
    incl = ri >= li
    strict = ri > li
    rb = lax.broadcasted_iota(jnp.int32, (GW, GW), 0) >> 6
    bdm = rb == (lax.broadcasted_iota(jnp.int32, (GW, GW), 1) >> 6)
    bdm2 = jnp.concatenate([bdm, bdm], axis=1)

    def bd_weights(m):
        return [jnp.where(bdm, _tile4(part), jnp.zeros((), BF16)) for part in _split(m, 2)]

    def times_bd(a, m):
        ah, al = _split(a, 2)
        mh, ml = bd_weights(m)
        n = a.shape[0]
        top = jnp.dot(jnp.concatenate([ah, al], axis=0), mh, preferred_element_type=F32)
        return top[:n] + top[n:] + jnp.dot(ah, ml, preferred_element_type=F32)

    chunks = range(tg // CHUNK)
    sls = [slice(c * CHUNK, (c + 1) * CHUNK) for c in chunks]
    kq = [_dot_nt(jnp.concatenate([kb[sl], qn[sl]], axis=0), jnp.where(bdm, _tile4(kn[sl]), 0.0))
          for sl in sls]
    nmat, tmat = [], []
    for c in chunks:
        gxc = gx[sls[c]]
        grow = jnp.sum(jnp.where(eye, gxc, 0.0), axis=0, keepdims=True)
        dec = jnp.where(incl, jnp.exp(gxc - grow), 0.0)
        nmat.append(jnp.where(strict, kq[c][:CHUNK] * dec, 0.0))
        qkd_ref[0, sls[c], :] = _bf(kq[c][CHUNK:] * dec)
        tmat.append(jnp.where(eye, 1.0, 0.0) - nmat[c])
        egl_ref[0, c] = egl[c * CHUNK:c * CHUNK + 8, :]
    pmat = [times_bd(nmat[c], nmat[c]) for c in chunks]
    for step in range(5):
        if step < 4:
            tp = [times_bd(jnp.concatenate([tmat[c], pmat[c]], axis=0), pmat[c]) for c in chunks]
            tmat = [tmat[c] + tp[c][:CHUNK] for c in chunks]
            pmat = [tp[c][CHUNK:] for c in chunks]
        else:
            tmat = [tmat[c] + times_bd(tmat[c], pmat[c]) for c in chunks]
    xs = [_dot(tmat[c], jnp.where(bdm2, _tile4(jnp.concatenate([vb[sls[c]], kbg[sls[c]]], axis=1)), 0.0))
          for c in chunks]
    for c in chunks:
        xu_ref[0, sls[c], :] = xs[c][:, :GW]
        xwk_ref[0, sls[c], :] = _bf(xs[c][:, GW:])


def _gdn2_kernel(xu_ref, xwk_ref, qkd_ref, qg_ref, kdec_ref, egl_ref, z_ref, s0_ref, ng_ref, bd_ref,
                 o_ref, sfin_ref, s_ref, *, bb, nchunk):
    ti = pl.program_id(1)

    @pl.when(ti == 0)
    def _():
        s_ref[...] = s0_ref[...]

    rb = lax.broadcasted_iota(jnp.int32, (GW, GW), 0) >> 6
    bdm = rb == (lax.broadcasted_iota(jnp.int32, (GW, GW), 1) >> 6)
    bd = bd_ref[...]
    ng = ng_ref[...]

    def chunk(c, carry):
        r0 = pl.multiple_of(c * CHUNK, CHUNK)
        rows = pl.ds(r0, CHUNK)
        for b in range(bb):
            s = s_ref[b]
            lhs = jnp.concatenate([xwk_ref[b, rows, :], qg_ref[b, rows, :]], axis=0)
            r = jnp.dot(lhs, _bf(s), preferred_element_type=F32)
            vnew = xu_ref[b, rows, :] - r[:CHUNK]
            vbd = jnp.where(bdm, _tile4(vnew), 0.0)
            o = r[CHUNK:] + jnp.dot(qkd_ref[b, rows, :], _bf(vbd), preferred_element_type=F32)
            kd_t = kdec_ref[b, rows, :].astype(F32).T
            upd = _dot(kd_t, vnew)
            s_ref[b] = egl_ref[b, c, 0:1, :] * s + jnp.where(bdm, upd, 0.0)
            zz = z_ref[b, rows, :]
            o_ref[b, rows, :] = _bf(_head_rms(o, bd, ng) * _silu(zz))
        return carry

    lax.fori_loop(0, nchunk, chunk, 0)
    sfin_ref[...] = s_ref[...]


def _gdn_consts(tg):
    i = np.arange(tg)
    same = i[:, None] // CHUNK == i[None, :] // CHUNK
    lblk = jnp.asarray(same & (i[:, None] >= i[None, :]), BF16)
    eb = np.zeros((128, GW), np.float32)
    eg = np.zeros((128, GW), np.float32)
    for h in range(N_HEADS):
        eb[h, h * HEAD_DIM:(h + 1) * HEAD_DIM] = 1
        eg[N_HEADS + h, h * HEAD_DIM:(h + 1) * HEAD_DIM] = 1
    return lblk, jnp.asarray(eb, BF16), jnp.asarray(eg, BF16)


def _gdn_call(qkv, small, z, st8, s0bd, cw, alog, dtb, ng, bd, tg, t_valid, bb):
    b, t, _ = qkv.shape
    n_tiles = t // tg
    lblk, eb, eg = _gdn_consts(tg)
    full = lambda a: pl.BlockSpec(a.shape, lambda bi, ti: (0,) * a.ndim)
    row = lambda n: pl.BlockSpec((1, tg, n), lambda bi, ti: (bi, ti, 0))
    nct = tg // CHUNK
    xu, xwk, qkd, qg, kdec, egl, conv_new = pl.pallas_call(
        functools.partial(_gdn1_kernel, tg=tg, t_valid=t_valid, n_tiles=n_tiles),
        grid=(b, n_tiles),
        in_specs=[row(3 * GW), row(128),
                  pl.BlockSpec((1, 8, 3 * GW), lambda bi, ti: (bi, 0, 0)),
                  full(cw), full(alog), full(dtb), full(lblk), full(eb), full(eg), full(bd)],
        out_specs=[row(GW), row(GW), row(GW), row(GW), row(GW),
                   pl.BlockSpec((1, nct, 8, GW), lambda bi, ti: (bi, ti, 0, 0)),
                   pl.BlockSpec((1, 3, 3 * GW), lambda bi, ti: (bi, 0, 0))],
        out_shape=[jax.ShapeDtypeStruct((b, t, GW), F32), jax.ShapeDtypeStruct((b, t, GW), BF16),
                   jax.ShapeDtypeStruct((b, t, GW), BF16), jax.ShapeDtypeStruct((b, t, GW), BF16),
                   jax.ShapeDtypeStruct((b, t, GW), BF16),
                   jax.ShapeDtypeStruct((b, t // CHUNK, 8, GW), F32),
                   jax.ShapeDtypeStruct((b, 3, 3 * GW), F32)],
        scratch_shapes=[pltpu.VMEM((8 + tg, 3 * GW), F32)],
        compiler_params=_params(("arbitrary", "arbitrary")),
        name="gdn_chunk_solve",
    )(qkv, small, st8, cw, alog, dtb, lblk, eb, eg, bd)

    full2 = lambda a: pl.BlockSpec(a.shape, lambda bi, ti: (0,) * a.ndim)
    rowb = pl.BlockSpec((bb, tg, GW), lambda bi, ti: (bi, ti, 0))
    sspec = pl.BlockSpec((bb, GW, GW), lambda bi, ti: (bi, 0, 0))
    o, sfin = pl.pallas_call(
        functools.partial(_gdn2_kernel, bb=bb, nchunk=nct),
        grid=(b // bb, n_tiles),
        in_specs=[rowb, rowb, rowb, rowb, rowb,
                  pl.BlockSpec((bb, nct, 8, GW), lambda bi, ti: (bi, ti, 0, 0)),
                  rowb, sspec, full2(ng), full2(bd)],
        out_specs=[rowb, sspec],
        out_shape=[jax.ShapeDtypeStruct((b, t, GW), BF16), jax.ShapeDtypeStruct((b, GW, GW), F32)],
        scratch_shapes=[pltpu.VMEM((bb, GW, GW), F32)],
        compiler_params=_params(("arbitrary", "arbitrary")),
        name="gdn_recurrence",
    )(xu, xwk, qkd, qg, kdec, egl, z, s0bd, ng, bd)
    return o, conv_new, sfin


def _out_kernel(x_ref, oa_ref, ob_ref, oc_ref, od_ref, ga1_ref, sc2_ref, sh2_ref, ga2_ref, g2_ref,
                wo_ref, wup_ref, cw_ref, wdn_ref, st_ref, y_ref, cnew_ref,
                ubuf_ref, car_ref, gs_ref, *, tm, tiles_per_seq, nff):
    i = pl.program_id(0)

    @pl.when(i % tiles_per_seq == 0)
    def _():
        car_ref[...] = st_ref[0]

    mix = jnp.dot(oa_ref[...], wo_ref[0:GW, :], preferred_element_type=F32)
    for n, ref in enumerate((ob_ref, oc_ref, od_ref)):
        mix = mix + jnp.dot(ref[...], wo_ref[(n + 1) * GW:(n + 2) * GW, :], preferred_element_type=F32)
    x1 = x_ref[...] + ga1_ref[0] * mix
    ms = jnp.mean(x1 * x1, axis=-1, keepdims=True)
    h = x1 * lax.rsqrt(ms + EPS) * g2_ref[...]
    h = _bf(h * (1.0 + sc2_ref[0]) + sh2_ref[0])
    dff = nff * GW
    for j in range(nff):
        halves = []
        for off in (0, dff):
            cols = slice(off + j * GW, off + (j + 1) * GW)
            u = jnp.dot(h, wup_ref[:, cols], preferred_element_type=F32)
            ubuf_ref[0:8, :] = car_ref[:, cols]
            ubuf_ref[8:8 + tm, :] = u
            halves.append(cw_ref[0:1, cols] * ubuf_ref[6:6 + tm, :] + cw_ref[1:2, cols] * ubuf_ref[7:7 + tm, :]
                          + cw_ref[2:3, cols] * u)
            car_ref[:, cols] = ubuf_ref[tm:tm + 8, :]
        gs_ref[:, j * GW:(j + 1) * GW] = _bf(_silu(halves[0]) * halves[1])
    y_ref[...] = x1 + ga2_ref[0] * jnp.dot(gs_ref[...], wdn_ref[...], preferred_element_type=F32)
    cnew_ref[0] = car_ref[...]


def _out_call(x2, oa, ob, oc, od, ga1, sc2, sh2, ga2, g2, wo, wup, cw, wdn, st8, tm, rows_per_seq):
    m, d = x2.shape
    dff2 = wup.shape[1]
    nff = dff2 // (2 * GW)
    tiles_per_seq = rows_per_seq // tm
    nseq = m // rows_per_seq
    r = ga1.shape[1]
    rows_per_mod = m // ga1.shape[0]
    tiles_per_mod = rows_per_mod // tm
    mod_spec = pl.BlockSpec((1, r, d), lambda i: (i // tiles_per_mod, 0, 0))
    full = lambda a: pl.BlockSpec(a.shape, lambda i: (0,) * a.ndim)
    row = lambda n: pl.BlockSpec((tm, n), lambda i: (i, 0))
    stspec = pl.BlockSpec((1, 8, dff2), lambda i: (i // tiles_per_seq, 0, 0))
    return pl.pallas_call(
        functools.partial(_out_kernel, tm=tm, tiles_per_seq=tiles_per_seq, nff=nff),
        grid=(m // tm,),
        in_specs=[row(d), row(GW), row(GW), row(GW), row(GW), mod_spec, mod_spec, mod_spec, mod_spec,
                  full(g2), full(wo), full(wup), full(cw), full(wdn), stspec],
        out_specs=[row(d), stspec],
        out_shape=[jax.ShapeDtypeStruct((m, d), F32), jax.ShapeDtypeStruct((nseq, 8, dff2), F32)],
        scratch_shapes=[pltpu.VMEM((8 + tm, GW), F32), pltpu.VMEM((8, dff2), F32),
                        pltpu.VMEM((tm, dff2 // 2), BF16)],
        compiler_params=_params(("arbitrary",)),
        name="out_projection_mlp",
    )(x2, oa, ob, oc, od, ga1, sc2, sh2, ga2, g2, wo, wup, cw, wdn, st8)


def _pack_in_weights(w_in):
    a = 4 * GW + 2 * N_HEADS
    off_b = a
    off_c = off_b + 3 * GW
    off_d = off_c + 3 * GW + N_HEADS
    dd = w_in.shape[:2]
    zeros = lambda n: jnp.zeros(dd + (n,), w_in.dtype)
    small = jnp.concatenate([w_in[..., 4 * GW:a], zeros(128 - 2 * N_HEADS),
                             w_in[..., off_c + 3 * GW:off_d], zeros(128 - N_HEADS)], axis=-1)
    packed = jnp.concatenate([w_in[..., :4 * GW], w_in[..., off_b:off_b + 3 * GW],
                              w_in[..., off_c:off_c + 3 * GW], w_in[..., off_d:off_d + 3 * GW], small],
                             axis=-1)
    return _bf(packed)


def _lane_vec(v, offset):
    depth = v.shape[0]
    out = jnp.zeros((depth, 1, 128), F32)
    return out.at[:, 0, offset:offset + N_HEADS].set(v.astype(F32))


def _band_ext(rel_table, lq, hist):
    c = np.arange(BAND_EXT)
    rel = np.clip(hist + lq - 1 - c, -REL_CLIP, REL_CLIP) + REL_CLIP
    return rel_table.astype(F32)[:, rel]


def _layer(x, mods_in, mods_out, wts, cache, cfg, stack=None):
    b, t, d = x.shape
    m = b * t
    bd = wts["bd"]
    sc1, sh1 = mods_in
    (gqkv, gz, small, sbq, sbk, sbv, fq, fk, fv, logf, bq, bk, bv) = _in_call(
        x.reshape(m, d), sc1, sh1, wts["norm_mix_g"], wts["w_in"], wts["qk_gains"], wts["fox_b_f"], bd,
        cfg["tm_in"], cfg["rows_per_mod_in"], stack)
    r3 = lambda a: a.reshape(b, t, a.shape[-1])
    new = {"fox_logf": r3(logf)}
    if stack is None:
        new.update({"sb_k": r3(sbk), "sb_v": r3(sbv), "fox_k": r3(fk), "fox_v": r3(fv)})
    past = cache is not None

    tp = cfg["t_gdn"]
    pad_t = lambda a: jnp.pad(r3(a), ((0, 0), (0, tp - t), (0, 0)))
    if past:
        st8 = jnp.pad(cache["gdn_conv"], ((0, 0), (5, 0), (0, 0)))
        s0 = cache["gdn_state"]
        eye = jnp.eye(N_HEADS, dtype=F32)
        s0bd = jnp.einsum("bhkv,hg->bhkgv", s0, eye).reshape(b, GW, GW)
    else:
        st8 = jnp.zeros((b, 8, 3 * GW), F32)
        s0bd = jnp.zeros((b, GW, GW), F32)
    oa, conv_new, sfin = _gdn_call(pad_t(gqkv), pad_t(small), pad_t(gz), st8, s0bd,
                                   wts["gdn_conv_w"], wts["gdn_a_log"], wts["gdn_dt_bias"],
                                   wts["gdn_norm_g"], bd, cfg["tg"], t, cfg["bb"])
    oa = oa[:, :t].reshape(m, GW)
    sfin = sfin.reshape(b, N_HEADS, HEAD_DIM, N_HEADS, HEAD_DIM)
    new["gdn_state"] = jnp.stack([sfin[:, h, :, h, :] for h in range(N_HEADS)], axis=1)
    new["gdn_conv"] = conv_new

    tq, tk, tkf = cfg["tq"], cfg["tk"], cfg["tk_fox"]
    if past:
        pos0 = cache["sb_k"].shape[1]
        tk_all = -(-(pos0 + t) // tk) * tk
        cat = lambda c, n: jnp.pad(jnp.concatenate([c.reshape(b, pos0, -1), r3(n)], axis=1),
                                   ((0, 0), (0, tk_all - pos0 - t), (0, 0)))
        ksb, vsb = cat(cache["sb_k"], sbk)[None], cat(cache["sb_v"], sbv)[None]
        kfx, vfx = cat(cache["fox_k"], fk)[None], cat(cache["fox_v"], fv)[None]
        lf_all = cat(cache["fox_logf"], logf)
        layer = 0
    else:
        pos0, tk_all = 0, t
        layer = stack[1]
        ksb, vsb, kfx, vfx = (a.reshape(a.shape[0], b, t, GW) for a in (sbk, sbv, fk, fv))
        lf_all = r3(logf)
    ob = _sb_call(r3(sbq), ksb, vsb, layer, wts["merge_g"][0:1], bd, tq, tk, pos0)
    fcum = _cumsum_call(jnp.swapaxes(lf_all, 1, 2).reshape(b * N_HEADS, tk_all))
    fcum = fcum.reshape(b, N_HEADS, tk_all)
    f_q = jnp.swapaxes(fcum[:, :, pos0:pos0 + t], 1, 2)
    f_k = jnp.swapaxes(fcum.reshape(b, N_HEADS, tk_all // tkf, tkf), 1, 2)
    oc = _fox_call(r3(fq), kfx, vfx, layer, f_q, f_k, wts["merge_g"][1:2], bd, tq, tkf, pos0)

    if past:
        hk, hv = cache["band_k"].reshape(b, -1, GW), cache["band_v"].reshape(b, -1, GW)
        keep = hk.shape[1]
        new["band_k"] = jnp.concatenate([hk, r3(bk)], axis=1)[:, -keep:]
        new["band_v"] = jnp.concatenate([hv, r3(bv)], axis=1)[:, -keep:]
    else:
        hk = hv = jnp.zeros((b, BAND_ROWS, GW), F32)
        keep = min(BAND_ROWS, t)
        new["band_k"], new["band_v"] = r3(bk)[:, -keep:], r3(bv)[:, -keep:]
    od = _band_call(r3(bq), r3(bk), r3(bv), hk, hv, wts["band_ext_" + cfg["name"]],
                    wts["merge_g"][2:3], bd, cfg["lq"], cfg["nch"], past)

    if past:
        st_ffn = jnp.pad(cache["ffn_conv"], ((0, 0), (6, 0), (0, 0)))
    else:
        st_ffn = jnp.zeros((b, 8, wts["w_up"].shape[1]), F32)
    ga1, sc2, sh2, ga2 = mods_out
    y, cnew = _out_call(x.reshape(m, d), oa, ob.reshape(m, GW), oc.reshape(m, GW), od.reshape(m, GW),
                        ga1, sc2, sh2, ga2, wts["norm_ffn_g"], wts["w_o"], wts["w_up"],
                        wts["ffn_conv_w"], wts["w_down"], st_ffn, cfg["tm_out"], t)
    new["ffn_conv"] = cnew[:, 6:8]
    hd = lambda a: a.reshape(a.shape[0], a.shape[1], N_HEADS, HEAD_DIM)
    for n in ("sb_k", "sb_v", "fox_k", "fox_v", "band_k", "band_v"):
        if n in new:
            new[n] = hd(new[n])
    return y.reshape(b, t, d), new, (sbk, sbv, fk, fv)


STATE_KEYS = ("gdn_conv", "gdn_state", "sb_k", "sb_v", "fox_k", "fox_v", "fox_logf",
              "band_k", "band_v", "ffn_conv")


def _group_cfg(name, b, t):
    if t % 512 == 0:
        return dict(name=name, tm_in=512, rows_per_mod_in=t, tm_out=512, t_gdn=t, tg=512,
                    bb=4 if b % 4 == 0 else 1, tq=256, tk=256, tk_fox=256, lq=CHUNK, nch=4)
    assert t % 16 == 0 and t <= CHUNK
    return dict(name=name, tm_in=b * t, rows_per_mod_in=b * t, tm_out=t, t_gdn=CHUNK, tg=CHUNK,
                bb=4 if b % 4 == 0 else 1, tq=t, tk=256, tk_fox=256, lq=t, nch=1)


def kernel(x_prompt, x_sample, c_prompt, c_sample, state_gdn_conv, state_gdn, cache_sb_k, cache_sb_v, cache_fox_k, cache_fox_v, cache_fox_logf, cache_band_k, cache_band_v, state_ffn_conv, ada_w, ada_b, norm_mix_g, w_in, gdn_conv_w, gdn_a_log, gdn_dt_bias, gdn_norm_g, fox_q_g, fox_k_g, fox_b_f, band_q_g, band_k_g, band_rel_bias, merge_g, w_o, norm_ffn_g, w_up, ffn_conv_w, w_down):
    depth = ada_w.shape[0]
    bp, tp, d = x_prompt.shape
    bs, ts, _ = x_sample.shape
    cfg_p = _group_cfg("p", bp, tp)
    cfg_s = _group_cfg("s", bs, ts)
    hist_s = cache_band_k.shape[2]

    mod = _ada_call(jnp.concatenate([c_prompt, c_sample], axis=0), ada_w, ada_b)
    mod = mod.reshape(depth, bp + bs, 6, d)

    tile_h = lambda g: jnp.tile(g.astype(F32), (1, N_HEADS))[:, None, :]
    w_in_p = _pack_in_weights(w_in)
    w_o_b, w_up_b, w_dn_b = _bf(w_o), _bf(w_up), _bf(w_down)
    ffn_cw = jnp.pad(ffn_conv_w.astype(F32), ((0, 0), (0, 5), (0, 0)))
    gdn_cw = jnp.pad(gdn_conv_w.astype(F32), ((0, 0), (0, 4), (0, 0)))
    qk_gains = jnp.concatenate([tile_h(fox_q_g), tile_h(fox_k_g), tile_h(band_q_g), tile_h(band_k_g),
                                jnp.zeros((depth, 4, GW), F32)], axis=1)
    bd = _const_bd256()

    y_p, y_s = x_prompt, x_sample
    kv_p = None
    new_p = {n: [] for n in STATE_KEYS}
    new_s = {n: [] for n in STATE_KEYS}
    for l in range(depth):
        wts = {
            "bd": bd, "norm_mix_g": norm_mix_g[l][None].astype(F32), "w_in": w_in_p[l],
            "qk_gains": qk_gains[l], "fox_b_f": _lane_vec(fox_b_f, 0)[l],
            "gdn_conv_w": gdn_cw[l], "gdn_a_log": _lane_vec(gdn_a_log, N_HEADS)[l],
            "gdn_dt_bias": _lane_vec(gdn_dt_bias, N_HEADS)[l], "gdn_norm_g": tile_h(gdn_norm_g)[l],
            "merge_g": merge_g[l].reshape(3, GW).astype(F32),
            "band_ext_p": _band_ext(band_rel_bias[l], cfg_p["lq"], BAND_ROWS),
            "band_ext_s": _band_ext(band_rel_bias[l], cfg_s["lq"], hist_s),
            "w_o": w_o_b[l], "norm_ffn_g": norm_ffn_g[l][None].astype(F32), "w_up": w_up_b[l],
            "ffn_conv_w": ffn_cw[l], "w_down": w_dn_b[l],
        }
        mp = mod[l, :bp]
        ms = mod[l, bp:]
        pm = lambda i: mp[:, i][:, None, :]
        mods_in_p = (pm(1), pm(0))
        mods_out_p = (pm(2), pm(4), pm(3), pm(5))
        srow = lambda i: jnp.repeat(ms[:, i], ts, axis=0)[None]
        sm1 = lambda i: ms[:, i][:, None, :]
        mods_in_s = (srow(1), srow(0))
        mods_out_s = (sm1(2), sm1(4), sm1(3), sm1(5))
        cache = {"gdn_conv": state_gdn_conv[l], "gdn_state": state_gdn[l],
                 "sb_k": cache_sb_k[l], "sb_v": cache_sb_v[l], "fox_k": cache_fox_k[l],
                 "fox_v": cache_fox_v[l], "fox_logf": cache_fox_logf[l],
                 "band_k": cache_band_k[l], "band_v": cache_band_v[l], "ffn_conv": state_ffn_conv[l]}
        y_p, st_p, kv_p = _layer(y_p, mods_in_p, mods_out_p, wts, None, cfg_p, (depth, l, kv_p))
        y_s, st_s, _ = _layer(y_s, mods_in_s, mods_out_s, wts, cache, cfg_s)
        for n in STATE_KEYS:
            if n in st_p:
                new_p[n].append(st_p[n])
            new_s[n].append(st_s[n])
    out_p = {n: jnp.stack(v) for n, v in new_p.items() if v}
    for n, a in zip(("sb_k", "sb_v", "fox_k", "fox_v"), kv_p):
        out_p[n] = a.reshape(depth, bp, tp, N_HEADS, HEAD_DIM)
    return ((y_p, y_s) + tuple(out_p[n] for n in STATE_KEYS)
            + tuple(jnp.stack(new_s[n]) for n in STATE_KEYS))
```

```python
import functools

import numpy as np
import jax
import jax.numpy as jnp
from jax import lax
from jax.experimental import pallas as pl
from jax.experimental.pallas import tpu as pltpu

F32 = jnp.float32
BF16 = jnp.bfloat16

HEAD_DIM = 64
N_HEADS = 4
GW = N_HEADS * HEAD_DIM
CHUNK = 64
BAND_ROWS = 512
REL_CLIP = 2 * CHUNK
EPS = 1e-6
NEG = -1e30
VMEM_LIMIT = 56 * 1024 * 1024
BAND_WIN = BAND_ROWS + 128
BAND_EXT = BAND_WIN + 128


def _bf(x):
    return x.astype(BF16)


def _dot(a, b):
    return jnp.dot(_bf(a), _bf(b), preferred_element_type=F32)


def _dot_nt(a, b):
    return lax.dot_general(_bf(a), _bf(b), (((1,), (1,)), ((), ())), preferred_element_type=F32)


def _split(x, n):
    parts = []
    for _ in range(n - 1):
        hi = _bf(x)
        parts.append(hi)
        x = x - hi.astype(F32)
    parts.append(_bf(x))
    return parts


def _dot_exact_rhs(x, m, n):
    return sum(jnp.dot(p, m, preferred_element_type=F32) for p in _split(x, n))


def _dot_exact_lhs(m, x, n):
    return sum(jnp.dot(m, p, preferred_element_type=F32) for p in _split(x, n))


def _sigmoid(x):
    return 1.0 / (1.0 + jnp.exp(-x))


def _silu(x):
    return x * _sigmoid(x)


def _softplus(x):
    return jnp.maximum(x, 0.0) + jnp.log(1.0 + jnp.exp(-jnp.abs(x)))


def _head_rms(o, bd, gain):
    ss = _dot_exact_rhs(o * o, bd, 2)
    return o * lax.rsqrt(ss * (1.0 / HEAD_DIM) + EPS) * gain


def _head_masks(width=GW):
    lane = lax.broadcasted_iota(jnp.int32, (1, width), 1)
    return [((lane & (GW - 1)) >> 6) == h for h in range(N_HEADS)]


def _params(sem, **kw):
    return pltpu.CompilerParams(dimension_semantics=sem, vmem_limit_bytes=VMEM_LIMIT, **kw)


def _np_bd(n, blk):
    i = np.arange(n)
    return (i[:, None] // blk == i[None, :] // blk)


def _const_bd256():
    return jnp.asarray(_np_bd(GW, HEAD_DIM), BF16)


def _const_tri_fwd(n):
    i = np.arange(n)
    return jnp.asarray(i[:, None] <= i[None, :], BF16)


def _const_tri_rev(n):
    i = np.arange(n)
    return jnp.asarray(i[:, None] >= i[None, :], BF16)


def _ada_kernel(c_ref, w_ref, b_ref, o_ref):
    s = _silu(c_ref[...])
    o_ref[0] = _dot(s, w_ref[0]) + b_ref[0]


def _ada_call(c_all, ada_w, ada_b):
    depth, d, n = ada_w.shape
    rows = c_all.shape[0]
    tn = 1536
    return pl.pallas_call(
        _ada_kernel,
        grid=(depth, n // tn),
        in_specs=[pl.BlockSpec((rows, d), lambda l, j: (0, 0)),
                  pl.BlockSpec((1, d, tn), lambda l, j: (l, 0, j)),
                  pl.BlockSpec((1, 1, tn), lambda l, j: (l, 0, j))],
        out_specs=pl.BlockSpec((1, rows, tn), lambda l, j: (l, 0, j)),
        out_shape=jax.ShapeDtypeStruct((depth, rows, n), F32),
        compiler_params=_params(("arbitrary", "arbitrary")),
        name="ada_modulation",
    )(c_all, ada_w, ada_b.reshape(depth, 1, n))


IN_GROUPS = 14
IN_COLS_PACKED = IN_GROUPS * GW


def _in_kernel(x_ref, sc_ref, sh_ref, g_ref, w_ref, gains_ref, bf_ref, bd_ref,
               gqkv_ref, gz_ref, small_ref, sbq_ref, sbk_ref, sbv_ref,
               fq_ref, fk_ref, fv_ref, logf_ref, bq_ref, bk_ref, bv_ref, *t_refs):
    x = x_ref[...]
    ms = jnp.mean(x * x, axis=-1, keepdims=True)
    h = x * lax.rsqrt(ms + EPS) * g_ref[...]
    h = _bf(h * (1.0 + sc_ref[0]) + sh_ref[0])
    bd = bd_ref[...]
    scale = HEAD_DIM ** -0.5

    def col(j, n=1):
        return jnp.dot(h, w_ref[:, j * GW:(j + n) * GW], preferred_element_type=F32)

    gqkv_ref[...] = col(0, 3)
    gz_ref[...] = col(3)
    sbq_ref[...] = _bf(col(4) * scale)
    def keep(ref, n, val):
        ref[...] = val
        if t_refs:
            t_refs[n][...] = val.T

    keep(sbk_ref, 0, col(5))
    keep(sbv_ref, 1, col(6))
    fq_ref[...] = _bf(_head_rms(col(7), bd, gains_ref[0:1, :]) * scale)
    keep(fk_ref, 2, _head_rms(col(8), bd, gains_ref[1:2, :]))
    keep(fv_ref, 3, col(9))
    bq_ref[...] = _bf(_head_rms(col(10), bd, gains_ref[2:3, :]) * scale)
    bk_ref[...] = _head_rms(col(11), bd, gains_ref[3:4, :])
    bv_ref[...] = col(12)
    sm = col(13)
    small_ref[...] = sm[:, :128]
    lf = sm[:, 128:] + bf_ref[...]
    logf = jnp.minimum(lf, 0.0) - jnp.log(1.0 + jnp.exp(-jnp.abs(lf)))
    logf_ref[...] = logf[:, :N_HEADS]


N_STACKED = 4


def _in_kernel_aliased(*refs):
    n_in = 8
    _in_kernel(*refs[:n_in], *refs[n_in + N_STACKED:])


def _in_call(x2, sc, sh, g, w, gains, bfv, bd, tm, rows_per_mod, stack=None):
    m, d = x2.shape
    r = sc.shape[1]
    tiles_per_mod = rows_per_mod // tm
    mod_spec = pl.BlockSpec((1, r, d), lambda i: (i // tiles_per_mod, 0, 0))
    full = lambda a: pl.BlockSpec(a.shape, lambda i: (0,) * a.ndim)
    row = lambda n: pl.BlockSpec((tm, n), lambda i: (i, 0))
    out_defs = [(3 * GW, F32), (GW, F32), (128, F32), (GW, BF16), (GW, F32), (GW, F32),
                (GW, BF16), (GW, F32), (GW, F32), (N_HEADS, F32), (GW, BF16), (GW, F32), (GW, F32)]
    out_specs = [row(n) for n, _ in out_defs]
    out_shape = [jax.ShapeDtypeStruct((m, n), dt) for n, dt in out_defs]
    in_specs = [row(d), mod_spec, mod_spec, full(g), full(w), full(gains), full(bfv), full(bd)]
    args = [x2, sc, sh, g, w, gains, bfv, bd]
    body, aliases = _in_kernel, {}
    if stack is not None:
        depth, layer, t, bufs = stack
        tps = t // tm
        tspec = pl.BlockSpec((None, None, GW, tm), lambda i: (layer, i // tps, 0, i % tps))
        if bufs is not None:
            body = _in_kernel_aliased
            in_specs += [pl.BlockSpec(memory_space=pl.ANY)] * N_STACKED
            aliases = {len(args) + n: len(out_specs) + n for n in range(N_STACKED)}
            args += list(bufs)
        out_specs += [tspec] * N_STACKED
        out_shape += [jax.ShapeDtypeStruct((depth, m // t, GW, t), F32)] * N_STACKED
    return pl.pallas_call(
        body,
        grid=(m // tm,),
        in_specs=in_specs,
        out_specs=out_specs,
        out_shape=out_shape,
        input_output_aliases=aliases,
        compiler_params=_params(("arbitrary",)),
        name="in_projection",
    )(*args)


def _cumsum_kernel(x_ref, u_ref, o_ref):
    r, tk = x_ref.shape
    carry = jnp.zeros((r, 1), F32)
    for blk in range(tk // 256):
        o = _dot_exact_rhs(x_ref[:, blk * 256:(blk + 1) * 256], u_ref[...], 3) + carry
        o_ref[:, blk * 256:(blk + 1) * 256] = o
        carry = o[:, 255:256]


def _cumsum_call(x):
    return pl.pallas_call(
        _cumsum_kernel,
        out_shape=jax.ShapeDtypeStruct(x.shape, F32),
        compiler_params=pltpu.CompilerParams(vmem_limit_bytes=VMEM_LIMIT),
        name="logf_cumsum",
    )(x, _const_tri_fwd(256))


def _fill_kv(k_ref, v_ref, kb_ref, vm_ref, hmask, tk):
    for kbi in range(k_ref.shape[1] // tk):
        rows = slice(kbi * tk, (kbi + 1) * tk)
        kb_ref[rows, :] = _bf(k_ref[0, rows, :])
        vv = v_ref[0, rows, :]
        for h in range(N_HEADS):
            vm_ref[kbi, h * tk:(h + 1) * tk, :] = _bf(jnp.where(hmask[h], vv, 0.0))


def _stack_heads(q, hmask):
    return jnp.concatenate([jnp.where(hmask[h], q, jnp.zeros_like(q)) for h in range(N_HEADS)], axis=0)


def _heads_to_lanes(w, tq):
    return jnp.concatenate([w[h * tq:(h + 1) * tq] for h in range(N_HEADS)], axis=1)


def _spread_heads(col, hmask, tq):
    out = col[(N_HEADS - 1) * tq:]
    for h in range(N_HEADS - 2, -1, -1):
        out = jnp.where(hmask[h], col[h * tq:(h + 1) * tq], out)
    return jnp.broadcast_to(out, (tq, GW))


def _stacked_positions(q0, kbi, tq, tk):
    rows = lax.broadcasted_iota(jnp.int32, (N_HEADS * tq, tk), 0) & (tq - 1)
    cols = lax.broadcasted_iota(jnp.int32, (N_HEADS * tq, tk), 1)
    return q0 + rows, kbi * tk + cols


def _sb_kernel(q_ref, k_ref, v_ref, u2_ref, bd_ref, mg_ref, o_ref,
               kb_ref, vm_ref, acc_ref, car_ref, *, tq, tk, pos0):
    qi = pl.program_id(1)
    hmask = _head_masks()

    @pl.when(qi == 0)
    def _():
        _fill_kv(k_ref, v_ref, kb_ref, vm_ref, hmask, tk)

    nqs = _stack_heads(-q_ref[0], hmask)
    acc_ref[...] = jnp.zeros_like(acc_ref)
    car_ref[...] = jnp.zeros_like(car_ref)
    q0 = pos0 + qi * tq
    n_full = q0 // tk

    def scores(kbi):
        start = pl.multiple_of(kbi * tk, tk)
        return _dot_nt(nqs, kb_ref[pl.ds(start, tk), :])

    def log_rest(kbi, y, masked):
        neg_abs = pltpu.bitcast(pltpu.bitcast(y, jnp.uint32) | jnp.uint32(0x80000000), F32)
        r = jnp.minimum(y, 0.0) - jnp.log(1.0 + jnp.exp(neg_abs))
        valid = None
        if masked:
            rowp, colp = _stacked_positions(q0, kbi, tq, tk)
            valid = colp < rowp
            r = jnp.where(valid, r, 0.0)
        rc = jnp.dot(jnp.concatenate(_split(r, 2), axis=1), u2_ref[...], preferred_element_type=F32)
        return rc, valid

    def weights(kbi, y, rc, valid):
        car = car_ref[...]
        w = jnp.exp(rc + jnp.concatenate([car] * (tk // 128), axis=1) - y)
        if valid is not None:
            w = jnp.where(valid, w, 0.0)
        car_ref[...] = car + jnp.broadcast_to(rc[:, 0:1], car.shape)
        acc_ref[...] += jnp.dot(_heads_to_lanes(_bf(w), tq), vm_ref[kbi], preferred_element_type=F32)

    def blocks(kbis, masked):
        ys = [scores(kbi) for kbi in kbis]
        pend = None
        for kbi, y, mk in zip(kbis, ys, masked):
            rc, valid = log_rest(kbi, y, mk)
            if pend is not None:
                weights(*pend)
            pend = (kbi, y, rc, valid)
        weights(*pend)

    @pl.when(n_full == 0)
    def _():
        blocks([n_full], [True])

    @pl.when(n_full > 0)
    def _():
        blocks([n_full, n_full - 1], [True, False])

    rest = jnp.maximum(n_full - 1, 0)

    def body(i, c):
        hi = rest - 1 - 2 * i
        blocks([hi, hi - 1], [False, False])
        return c

    lax.fori_loop(0, rest // 2, body, 0)

    @pl.when(rest % 2 == 1)
    def _():
        blocks([0], [False])

    o_ref[0] = _bf(_head_rms(acc_ref[...], bd_ref[...], mg_ref[...]))


def _fox_kernel(q_ref, k_ref, v_ref, fq_ref, fk_ref, bd_ref, mg_ref, o_ref,
                kb_ref, vm_ref, acc_ref, fqb_ref, *, tq, tk, pos0):
    qi = pl.program_id(1)
    hmask = _head_masks()

    @pl.when(qi == 0)
    def _():
        _fill_kv(k_ref, v_ref, kb_ref, vm_ref, hmask, tk)

    qs = _stack_heads(q_ref[0], hmask)
    fq = fq_ref[0]
    for h in range(N_HEADS):
        fqb_ref[h * tq:(h + 1) * tq, :] = jnp.broadcast_to(fq[:, h:h + 1], (tq, tk))
    acc_ref[...] = jnp.zeros_like(acc_ref)
    q0 = pos0 + qi * tq
    n_full = q0 // tk

    def scores(kbi):
        start = pl.multiple_of(kbi * tk, tk)
        return _dot_nt(qs, kb_ref[pl.ds(start, tk), :])

    def update(kbi, z, masked, ml):
        m_old, l_old = ml
        fk = fk_ref[0, kbi]
        s = jnp.concatenate([z[h * tq:(h + 1) * tq] + fqb_ref[h * tq:(h + 1) * tq, :] - fk[h:h + 1, :]
                             for h in range(N_HEADS)], axis=0)
        if masked:
            rowp, colp = _stacked_positions(q0, kbi, tq, tk)
            s = jnp.where(colp <= rowp, s, NEG)
        m_new = jnp.maximum(m_old, jnp.max(s, axis=1, keepdims=True))
        alpha = jnp.exp(m_old - m_new)
        p = jnp.exp(s - m_new)
        l_new = alpha * l_old + jnp.sum(p, axis=1, keepdims=True)
        pv = jnp.dot(_heads_to_lanes(_bf(p), tq), vm_ref[kbi], preferred_element_type=F32)
        acc_ref[...] = acc_ref[...] * _spread_heads(alpha, hmask, tq) + pv
        return m_new, l_new

    def pair(i, ml):
        za, zb = scores(2 * i), scores(2 * i + 1)
        return update(2 * i + 1, zb, False, update(2 * i, za, False, ml))

    ml = (jnp.full((N_HEADS * tq, 1), NEG, F32), jnp.zeros((N_HEADS * tq, 1), F32))
    rest = jnp.maximum(n_full - 1, 0)
    ml = lax.fori_loop(0, rest // 2, pair, ml)
    ml = lax.cond(rest % 2 == 1, lambda c: update(rest - 1, scores(rest - 1), False, c), lambda c: c, ml)

    def last_two(c):
        za, zb = scores(n_full - 1), scores(n_full)
        return update(n_full, zb, True, update(n_full - 1, za, False, c))

    _, l_fin = lax.cond(n_full > 0, last_two, lambda c: update(n_full, scores(n_full), True, c), ml)
    o = acc_ref[...] / _spread_heads(l_fin, hmask, tq)
    o_ref[0] = _bf(_head_rms(o, bd_ref[...], mg_ref[...]))


def _attn_specs(tk_all, tq):
    qspec = pl.BlockSpec((1, tq, GW), lambda bi, qi: (bi, qi, 0))
    kspec = pl.BlockSpec((1, tk_all, GW), lambda bi, qi: (bi, 0, 0))
    return qspec, kspec


def _sb_call(q, k, v, mg, bd, tq, tk, pos0):
    b, tq_all, _ = q.shape
    tk_all = k.shape[1]
    qspec, kspec = _attn_specs(tk_all, tq)
    full = lambda a: pl.BlockSpec(a.shape, lambda bi, qi: (0,) * a.ndim)
    u = _const_tri_rev(tk)
    u2 = jnp.concatenate([u, u], axis=0)
    return pl.pallas_call(
        functools.partial(_sb_kernel, tq=tq, tk=tk, pos0=pos0),
        grid=(b, tq_all // tq),
        in_specs=[qspec, kspec, kspec, full(u2), full(bd), full(mg)],
        out_specs=qspec,
        out_shape=jax.ShapeDtypeStruct((b, tq_all, GW), BF16),
        scratch_shapes=[pltpu.VMEM((tk_all, GW), BF16), pltpu.VMEM((tk_all // tk, N_HEADS * tk, GW), BF16),
                        pltpu.VMEM((tq, GW), F32), pltpu.VMEM((N_HEADS * tq, 128), F32)],
        compiler_params=_params(("arbitrary", "arbitrary")),
        name="stick_breaking_attention",
    )(q, k, v, u2, bd, mg)


def _fox_call(q, k, v, fq, fk, mg, bd, tq, tk, pos0):
    b, tq_all, _ = q.shape
    tk_all = k.shape[1]
    qspec, kspec = _attn_specs(tk_all, tq)
    full = lambda a: pl.BlockSpec(a.shape, lambda bi, qi: (0,) * a.ndim)
    return pl.pallas_call(
        functools.partial(_fox_kernel, tq=tq, tk=tk, pos0=pos0),
        grid=(b, tq_all // tq),
        in_specs=[qspec, kspec, kspec,
                  pl.BlockSpec((1, tq, N_HEADS), lambda bi, qi: (bi, qi, 0)),
                  pl.BlockSpec((1, tk_all // tk, N_HEADS, tk), lambda bi, qi: (bi, 0, 0, 0)),
                  full(bd), full(mg)],
        out_specs=qspec,
        out_shape=jax.ShapeDtypeStruct((b, tq_all, GW), BF16),
        scratch_shapes=[pltpu.VMEM((tk_all, GW), BF16), pltpu.VMEM((tk_all // tk, N_HEADS * tk, GW), BF16),
                        pltpu.VMEM((tq, GW), F32), pltpu.VMEM((N_HEADS * tq, tk), F32)],
        compiler_params=_params(("arbitrary", "arbitrary")),
        name="forgetting_attention",
    )(q, k, v, fq, fk, bd, mg)


def _band_kernel(q_ref, k_ref, v_ref, hk_ref, hv_ref, ext_ref, bd_ref, mg_ref, o_ref,
                 kp_ref, vp_ref, bias_ref, *, lq, nch, has_hist):
    qi = pl.program_id(1)
    hmask = _head_masks()
    t = k_ref.shape[1]
    hist = hk_ref.shape[1]
    ext_w = ext_ref.shape[1]

    @pl.when(qi == 0)
    def _():
        def put(r0, kk, vv):
            n = kk.shape[0]
            kp_ref[r0:r0 + n, :] = _bf(kk)
            vp_ref[r0:r0 + n, :] = _bf(vv)
        step = min(256, t)
        for r in range(0, hist, 256):
            put(r, hk_ref[0, r:r + 256, :], hv_ref[0, r:r + 256, :])
        for r in range(0, t, step):
            put(hist + r, k_ref[0, r:r + step, :], v_ref[0, r:r + step, :])
        zeros = jnp.zeros((128, GW), F32)
        put(hist + t, zeros, zeros)
        for h in range(N_HEADS):
            e = jnp.broadcast_to(ext_ref[h:h + 1, :], (lq, ext_w))
            e = pltpu.roll(e, ext_w - (lq - 1), 1, stride=1, stride_axis=0)
            bias_ref[h * lq:(h + 1) * lq, :] = e[:, :BAND_WIN]

    col = lax.broadcasted_iota(jnp.int32, (N_HEADS * lq, BAND_WIN), 1)
    chunks = range(nch)
    starts = [pl.multiple_of((qi * nch + ci) * lq, lq) for ci in chunks]
    s_all = [_dot_nt(_stack_heads(q_ref[0, ci * lq:(ci + 1) * lq, :], hmask),
                     kp_ref[pl.ds(starts[ci], BAND_WIN), :]) for ci in chunks]
    p_all = []
    for ci in chunks:
        valid = col < hist + lq
        if not has_hist:
            valid = jnp.logical_and(valid, col + (qi * nch + ci) * lq >= hist)
        s = jnp.where(valid, s_all[ci] + bias_ref[...], NEG)
        p = jnp.exp(s - jnp.max(s, axis=1, keepdims=True))
        p_all.append(_bf(p / jnp.sum(p, axis=1, keepdims=True)))
    pv_all = [jnp.dot(p_all[ci], vp_ref[pl.ds(starts[ci], BAND_WIN), :], preferred_element_type=F32)
              for ci in chunks]
    for ci in chunks:
        acc = jnp.zeros((lq, GW), F32)
        for h in range(N_HEADS):
            acc = acc + jnp.where(hmask[h], pv_all[ci][h * lq:(h + 1) * lq], 0.0)
        o_ref[0, ci * lq:(ci + 1) * lq, :] = _bf(_head_rms(acc, bd_ref[...], mg_ref[...]))


def _band_call(q, k, v, hk, hv, ext, mg, bd, lq, nch, has_hist):
    b, t, _ = q.shape
    hist = hk.shape[1]
    tqb = lq * nch
    qspec = pl.BlockSpec((1, tqb, GW), lambda bi, qi: (bi, qi, 0))
    kspec = pl.BlockSpec((1, t, GW), lambda bi, qi: (bi, 0, 0))
    hspec = pl.BlockSpec((1, hist, GW), lambda bi, qi: (bi, 0, 0))
    full = lambda a: pl.BlockSpec(a.shape, lambda bi, qi: (0,) * a.ndim)
    rows = hist + t + 128
    return pl.pallas_call(
        functools.partial(_band_kernel, lq=lq, nch=nch, has_hist=has_hist),
        grid=(b, t // tqb),
        in_specs=[qspec, kspec, kspec, hspec, hspec, full(ext), full(bd), full(mg)],
        out_specs=qspec,
        out_shape=jax.ShapeDtypeStruct((b, t, GW), BF16),
        scratch_shapes=[pltpu.VMEM((rows, GW), BF16), pltpu.VMEM((rows, GW), BF16),
                        pltpu.VMEM((N_HEADS * lq, BAND_WIN), F32)],
        compiler_params=_params(("arbitrary", "arbitrary")),
        name="band_attention",
    )(q, k, v, hk, hv, ext, bd, mg)


def _tile4(x):
    return jnp.concatenate([x] * N_HEADS, axis=0)


def _gdn1_kernel(qkv_ref, small_ref, st_ref, cw_ref, alog_ref, dtb_ref,
                 lblk_ref, eb_ref, eg_ref, bd_ref,
                 xu_ref, xwk_ref, qkd_ref, qg_ref, kdec_ref, egl_ref, conv_ref,
                 xbuf_ref, *, tg, t_valid, n_tiles):
    ti = pl.program_id(1)

    @pl.when(ti == 0)
    def _():
        xbuf_ref[0:8, :] = st_ref[0]

    xbuf_ref[8:8 + tg, :] = qkv_ref[0]
    y = cw_ref[3:4, :] * xbuf_ref[8:8 + tg, :]
    for i in range(3):
        y = y + cw_ref[i:i + 1, :] * xbuf_ref[5 + i:5 + i + tg, :]
    tv = t_valid - (n_tiles - 1) * tg

    @pl.when(ti == n_tiles - 1)
    def _():
        conv_ref[0] = xbuf_ref[8 + tv - 3:8 + tv, :]

    xbuf_ref[0:8, :] = xbuf_ref[tg:tg + 8, :]
    y = _silu(y)
    bd = bd_ref[...]
    q, k, v = y[:, :GW], y[:, GW:2 * GW], y[:, 2 * GW:]
    qn = q * lax.rsqrt(_dot_exact_rhs(q * q, bd, 2) + EPS) * (HEAD_DIM ** -0.5)
    kn = k * lax.rsqrt(_dot_exact_rhs(k * k, bd, 2) + EPS)

    sm = small_ref[0]
    rowv = ti * tg + lax.broadcasted_iota(jnp.int32, (tg, 1), 0) < t_valid
    beta_all = jnp.where(rowv, _sigmoid(sm), 0.0)
    g_all = jnp.where(rowv, -jnp.exp(alog_ref[...]) * _softplus(sm + dtb_ref[...]), 0.0)
    gc = _dot_exact_lhs(lblk_ref[...], g_all, 3)
    gl = jnp.concatenate([jnp.broadcast_to(gc[c * CHUNK + CHUNK - 1:(c + 1) * CHUNK, :], (CHUNK, 128))
                          for c in range(tg // CHUNK)], axis=0)
    bexp = _dot_exact_rhs(beta_all, eb_ref[...], 3)
    gx = _dot_exact_rhs(gc, eg_ref[...], 3)
    glx = _dot_exact_rhs(gl, eg_ref[...], 3)
    eg = jnp.exp(gx)
    kb = kn * bexp
    vb = v * bexp
    kbg = kb * eg
    qg_ref[0] = _bf(qn * eg)
    kdec_ref[0] = _bf(kn * jnp.exp(glx - gx))
    egl = jnp.exp(glx)

    ri = lax.broadcasted_iota(jnp.int32, (CHUNK, GW), 0)
    li = lax.broadcasted_iota(jnp.int32, (CHUNK, GW), 1) & (CHUNK - 1)
    eye = ri == li
    incl = ri >= li
    strict = ri > li
    rb = lax.broadcasted_iota(jnp.int32, (GW, GW), 0) >> 6
    bdm = rb == (lax.broadcasted_iota(jnp.int32, (GW, GW), 1) >> 6)
    bdm2 = jnp.concatenate([bdm, bdm], axis=1)

    def bd_weights(m):
        return [jnp.where(bdm, _tile4(part), jnp.zeros((), BF16)) for part in _split(m, 2)]

    def times_bd(a, m):
        ah, al = _split(a, 2)
        mh, ml = bd_weights(m)
        n = a.shape[0]
        top = jnp.dot(jnp.concatenate([ah, al], axis=0), mh, preferred_element_type=F32)
        return top[:n] + top[n:] + jnp.dot(ah, ml, preferred_element_type=F32)

    chunks = range(tg // CHUNK)
    sls = [slice(c * CHUNK, (c + 1) * CHUNK) for c in chunks]
    kq = [_dot_nt(jnp.concatenate([kb[sl], qn[sl]], axis=0), jnp.where(bdm, _tile4(kn[sl]), 0.0))
          for sl in sls]
    nmat, tmat = [], []
    for c in chunks:
        gxc = gx[sls[c]]
        grow = jnp.sum(jnp.where(eye, gxc, 0.0), axis=0, keepdims=True)
        dec = jnp.where(incl, jnp.exp(gxc - grow), 0.0)
        nmat.append(jnp.where(strict, kq[c][:CHUNK] * dec, 0.0))
        qkd_ref[0, sls[c], :] = _bf(kq[c][CHUNK:] * dec)
        tmat.append(jnp.where(eye, 1.0, 0.0) - nmat[c])
        egl_ref[0, c] = egl[c * CHUNK:c * CHUNK + 8, :]
    pmat = [times_bd(nmat[c], nmat[c]) for c in chunks]
    for step in range(5):
        if step < 4:
            tp = [times_bd(jnp.concatenate([tmat[c], pmat[c]], axis=0), pmat[c]) for c in chunks]
            tmat = [tmat[c] + tp[c][:CHUNK] for c in chunks]
            pmat = [tp[c][CHUNK:] for c in chunks]
        else:
            tmat = [tmat[c] + times_bd(tmat[c], pmat[c]) for c in chunks]
    xs = [_dot(tmat[c], jnp.where(bdm2, _tile4(jnp.concatenate([vb[sls[c]], kbg[sls[c]]], axis=1)), 0.0))
          for c in chunks]
    for c in chunks:
        xu_ref[0, sls[c], :] = xs[c][:, :GW]
        xwk_ref[0, sls[c], :] = _bf(xs[c][:, GW:])


def _gdn2_kernel(xu_ref, xwk_ref, qkd_ref, qg_ref, kdec_ref, egl_ref, z_ref, s0_ref, ng_ref, bd_ref,
                 o_ref, sfin_ref, s_ref, *, bb, nchunk):
    ti = pl.program_id(1)

    @pl.when(ti == 0)
    def _():
        s_ref[...] = s0_ref[...]

    rb = lax.broadcasted_iota(jnp.int32, (GW, GW), 0) >> 6
    bdm = rb == (lax.broadcasted_iota(jnp.int32, (GW, GW), 1) >> 6)
    bd = bd_ref[...]
    ng = ng_ref[...]

    def chunk(c, carry):
        r0 = pl.multiple_of(c * CHUNK, CHUNK)
        rows = pl.ds(r0, CHUNK)
        bs = range(bb)
        s_old = [s_ref[b] for b in bs]
        r = [jnp.dot(jnp.concatenate([xwk_ref[b, rows, :], qg_ref[b, rows, :]], axis=0), _bf(s_old[b]),
                     preferred_element_type=F32) for b in bs]
        vnew = [xu_ref[b, rows, :] - r[b][:CHUNK] for b in bs]
        o = [r[b][CHUNK:] + jnp.dot(qkd_ref[b, rows, :], _bf(jnp.where(bdm, _tile4(vnew[b]), 0.0)),
                                    preferred_element_type=F32) for b in bs]
        upd = [_dot(kdec_ref[b, rows, :].astype(F32).T, vnew[b]) for b in bs]
        for b in bs:
            s_ref[b] = egl_ref[b, c, 0:1, :] * s_old[b] + jnp.where(bdm, upd[b], 0.0)
            o_ref[b, rows, :] = _bf(_head_rms(o[b], bd, ng) * _silu(z_ref[b, rows, :]))
        return carry

    lax.fori_loop(0, nchunk, chunk, 0)
    sfin_ref[...] = s_ref[...]


def _gdn_consts(tg):
    i = np.arange(tg)
    same = i[:, None] // CHUNK == i[None, :] // CHUNK
    lblk = jnp.asarray(same & (i[:, None] >= i[None, :]), BF16)
    eb = np.zeros((128, GW), np.float32)
    eg = np.zeros((128, GW), np.float32)
    for h in range(N_HEADS):
        eb[h, h * HEAD_DIM:(h + 1) * HEAD_DIM] = 1
        eg[N_HEADS + h, h * HEAD_DIM:(h + 1) * HEAD_DIM] = 1
    return lblk, jnp.asarray(eb, BF16), jnp.asarray(eg, BF16)


def _gdn_call(qkv, small, z, st8, s0bd, cw, alog, dtb, ng, bd, tg, t_valid, bb):
    b, t, _ = qkv.shape
    n_tiles = t // tg
    lblk, eb, eg = _gdn_consts(tg)
    full = lambda a: pl.BlockSpec(a.shape, lambda bi, ti: (0,) * a.ndim)
    row = lambda n: pl.BlockSpec((1, tg, n), lambda bi, ti: (bi, ti, 0))
    nct = tg // CHUNK
    xu, xwk, qkd, qg, kdec, egl, conv_new = pl.pallas_call(
        functools.partial(_gdn1_kernel, tg=tg, t_valid=t_valid, n_tiles=n_tiles),
        grid=(b, n_tiles),
        in_specs=[row(3 * GW), row(128),
                  pl.BlockSpec((1, 8, 3 * GW), lambda bi, ti: (bi, 0, 0)),
                  full(cw), full(alog), full(dtb), full(lblk), full(eb), full(eg), full(bd)],
        out_specs=[row(GW), row(GW), row(GW), row(GW), row(GW),
                   pl.BlockSpec((1, nct, 8, GW), lambda bi, ti: (bi, ti, 0, 0)),
                   pl.BlockSpec((1, 3, 3 * GW), lambda bi, ti: (bi, 0, 0))],
        out_shape=[jax.ShapeDtypeStruct((b, t, GW), F32), jax.ShapeDtypeStruct((b, t, GW), BF16),
                   jax.ShapeDtypeStruct((b, t, GW), BF16), jax.ShapeDtypeStruct((b, t, GW), BF16),
                   jax.ShapeDtypeStruct((b, t, GW), BF16),
                   jax.ShapeDtypeStruct((b, t // CHUNK, 8, GW), F32),
                   jax.ShapeDtypeStruct((b, 3, 3 * GW), F32)],
        scratch_shapes=[pltpu.VMEM((8 + tg, 3 * GW), F32)],
        compiler_params=_params(("arbitrary", "arbitrary")),
        name="gdn_chunk_solve",
    )(qkv, small, st8, cw, alog, dtb, lblk, eb, eg, bd)

    full2 = lambda a: pl.BlockSpec(a.shape, lambda bi, ti: (0,) * a.ndim)
    rowb = pl.BlockSpec((bb, tg, GW), lambda bi, ti: (bi, ti, 0))
    sspec = pl.BlockSpec((bb, GW, GW), lambda bi, ti: (bi, 0, 0))
    o, sfin = pl.pallas_call(
        functools.partial(_gdn2_kernel, bb=bb, nchunk=nct),
        grid=(b // bb, n_tiles),
        in_specs=[rowb, rowb, rowb, rowb, rowb,
                  pl.BlockSpec((bb, nct, 8, GW), lambda bi, ti: (bi, ti, 0, 0)),
                  rowb, sspec, full2(ng), full2(bd)],
        out_specs=[rowb, sspec],
        out_shape=[jax.ShapeDtypeStruct((b, t, GW), BF16), jax.ShapeDtypeStruct((b, GW, GW), F32)],
        scratch_shapes=[pltpu.VMEM((bb, GW, GW), F32)],
        compiler_params=_params(("arbitrary", "arbitrary")),
        name="gdn_recurrence",
    )(xu, xwk, qkd, qg, kdec, egl, z, s0bd, ng, bd)
    return o, conv_new, sfin


def _out_kernel(x_ref, oa_ref, ob_ref, oc_ref, od_ref, ga1_ref, sc2_ref, sh2_ref, ga2_ref, g2_ref,
                wo_ref, wup_ref, cw_ref, wdn_ref, st_ref, y_ref, cnew_ref,
                ubuf_ref, car_ref, gs_ref, *, tm, tiles_per_seq, nff):
    i = pl.program_id(0)

    @pl.when(i % tiles_per_seq == 0)
    def _():
        car_ref[...] = st_ref[0]

    mix = jnp.dot(oa_ref[...], wo_ref[0:GW, :], preferred_element_type=F32)
    for n, ref in enumerate((ob_ref, oc_ref, od_ref)):
        mix = mix + jnp.dot(ref[...], wo_ref[(n + 1) * GW:(n + 2) * GW, :], preferred_element_type=F32)
    x1 = x_ref[...] + ga1_ref[0] * mix
    ms = jnp.mean(x1 * x1, axis=-1, keepdims=True)
    h = x1 * lax.rsqrt(ms + EPS) * g2_ref[...]
    h = _bf(h * (1.0 + sc2_ref[0]) + sh2_ref[0])
    dff = nff * GW
    for j in range(nff):
        halves = []
        for off in (0, dff):
            cols = slice(off + j * GW, off + (j + 1) * GW)
            u = jnp.dot(h, wup_ref[:, cols], preferred_element_type=F32)
            ubuf_ref[0:8, :] = car_ref[:, cols]
            ubuf_ref[8:8 + tm, :] = u
            halves.append(cw_ref[0:1, cols] * ubuf_ref[6:6 + tm, :] + cw_ref[1:2, cols] * ubuf_ref[7:7 + tm, :]
                          + cw_ref[2:3, cols] * u)
            car_ref[:, cols] = ubuf_ref[tm:tm + 8, :]
        gs_ref[:, j * GW:(j + 1) * GW] = _bf(_silu(halves[0]) * halves[1])
    y_ref[...] = x1 + ga2_ref[0] * jnp.dot(gs_ref[...], wdn_ref[...], preferred_element_type=F32)
    cnew_ref[0] = car_ref[...]


def _out_call(x2, oa, ob, oc, od, ga1, sc2, sh2, ga2, g2, wo, wup, cw, wdn, st8, tm, rows_per_seq):
    m, d = x2.shape
    dff2 = wup.shape[1]
    nff = dff2 // (2 * GW)
    tiles_per_seq = rows_per_seq // tm
    nseq = m // rows_per_seq
    r = ga1.shape[1]
    rows_per_mod = m // ga1.shape[0]
    tiles_per_mod = rows_per_mod // tm
    mod_spec = pl.BlockSpec((1, r, d), lambda i: (i // tiles_per_mod, 0, 0))
    full = lambda a: pl.BlockSpec(a.shape, lambda i: (0,) * a.ndim)
    row = lambda n: pl.BlockSpec((tm, n), lambda i: (i, 0))
    stspec = pl.BlockSpec((1, 8, dff2), lambda i: (i // tiles_per_seq, 0, 0))
    return pl.pallas_call(
        functools.partial(_out_kernel, tm=tm, tiles_per_seq=tiles_per_seq, nff=nff),
        grid=(m // tm,),
        in_specs=[row(d), row(GW), row(GW), row(GW), row(GW), mod_spec, mod_spec, mod_spec, mod_spec,
                  full(g2), full(wo), full(wup), full(cw), full(wdn), stspec],
        out_specs=[row(d), stspec],
        out_shape=[jax.ShapeDtypeStruct((m, d), F32), jax.ShapeDtypeStruct((nseq, 8, dff2), F32)],
        scratch_shapes=[pltpu.VMEM((8 + tm, GW), F32), pltpu.VMEM((8, dff2), F32),
                        pltpu.VMEM((tm, dff2 // 2), BF16)],
        compiler_params=_params(("arbitrary",)),
        name="out_projection_mlp",
    )(x2, oa, ob, oc, od, ga1, sc2, sh2, ga2, g2, wo, wup, cw, wdn, st8)


def _pack_in_weights(w_in):
    a = 4 * GW + 2 * N_HEADS
    off_b = a
    off_c = off_b + 3 * GW
    off_d = off_c + 3 * GW + N_HEADS
    dd = w_in.shape[:2]
    zeros = lambda n: jnp.zeros(dd + (n,), w_in.dtype)
    small = jnp.concatenate([w_in[..., 4 * GW:a], zeros(128 - 2 * N_HEADS),
                             w_in[..., off_c + 3 * GW:off_d], zeros(128 - N_HEADS)], axis=-1)
    packed = jnp.concatenate([w_in[..., :4 * GW], w_in[..., off_b:off_b + 3 * GW],
                              w_in[..., off_c:off_c + 3 * GW], w_in[..., off_d:off_d + 3 * GW], small],
                             axis=-1)
    return _bf(packed)


def _lane_vec(v, offset):
    depth = v.shape[0]
    out = jnp.zeros((depth, 1, 128), F32)
    return out.at[:, 0, offset:offset + N_HEADS].set(v.astype(F32))


def _band_ext(rel_table, lq, hist):
    c = np.arange(BAND_EXT)
    rel = np.clip(hist + lq - 1 - c, -REL_CLIP, REL_CLIP) + REL_CLIP
    return rel_table.astype(F32)[:, rel]


def _layer(x, mods_in, mods_out, wts, cache, cfg, stack=None):
    b, t, d = x.shape
    m = b * t
    bd = wts["bd"]
    sc1, sh1 = mods_in
    outs = _in_call(x.reshape(m, d), sc1, sh1, wts["norm_mix_g"], wts["w_in"], wts["qk_gains"],
                    wts["fox_b_f"], bd, cfg["tm_in"], cfg["rows_per_mod_in"], stack)
    (gqkv, gz, small, sbq, sbk, sbv, fq, fk, fv, logf, bq, bk, bv), kv_t = outs[:13], tuple(outs[13:])
    r3 = lambda a: a.reshape(b, t, a.shape[-1])
    new = {"fox_logf": r3(logf)}
    if stack is None:
        new.update({"sb_k": r3(sbk), "sb_v": r3(sbv), "fox_k": r3(fk), "fox_v": r3(fv)})
    past = cache is not None

    tp = cfg["t_gdn"]
    pad_t = lambda a: jnp.pad(r3(a), ((0, 0), (0, tp - t), (0, 0)))
    if past:
        st8 = jnp.pad(cache["gdn_conv"], ((0, 0), (5, 0), (0, 0)))
        s0 = cache["gdn_state"]
        eye = jnp.eye(N_HEADS, dtype=F32)
        s0bd = jnp.einsum("bhkv,hg->bhkgv", s0, eye).reshape(b, GW, GW)
    else:
        st8 = jnp.zeros((b, 8, 3 * GW), F32)
        s0bd = jnp.zeros((b, GW, GW), F32)
    oa, conv_new, sfin = _gdn_call(pad_t(gqkv), pad_t(small), pad_t(gz), st8, s0bd,
                                   wts["gdn_conv_w"], wts["gdn_a_log"], wts["gdn_dt_bias"],
                                   wts["gdn_norm_g"], bd, cfg["tg"], t, cfg["bb"])
    oa = oa[:, :t].reshape(m, GW)
    sfin = sfin.reshape(b, N_HEADS, HEAD_DIM, N_HEADS, HEAD_DIM)
    new["gdn_state"] = jnp.stack([sfin[:, h, :, h, :] for h in range(N_HEADS)], axis=1)
    new["gdn_conv"] = conv_new

    tq, tk, tkf = cfg["tq"], cfg["tk"], cfg["tk_fox"]
    if past:
        pos0 = cache["sb_k"].shape[1]
        tk_all = -(-(pos0 + t) // tk) * tk
        cat = lambda c, n: jnp.pad(jnp.concatenate([c.reshape(b, pos0, -1), r3(n)], axis=1),
                                   ((0, 0), (0, tk_all - pos0 - t), (0, 0)))
        ksb, vsb = cat(cache["sb_k"], sbk), cat(cache["sb_v"], sbv)
        kfx, vfx = cat(cache["fox_k"], fk), cat(cache["fox_v"], fv)
        lf_all = cat(cache["fox_logf"], logf)
    else:
        pos0, tk_all = 0, t
        ksb, vsb, kfx, vfx, lf_all = r3(sbk), r3(sbv), r3(fk), r3(fv), r3(logf)
    ob = _sb_call(r3(sbq), ksb, vsb, wts["merge_g"][0:1], bd, tq, tk, pos0)
    fcum = _cumsum_call(jnp.swapaxes(lf_all, 1, 2).reshape(b * N_HEADS, tk_all))
    fcum = fcum.reshape(b, N_HEADS, tk_all)
    f_q = jnp.swapaxes(fcum[:, :, pos0:pos0 + t], 1, 2)
    f_k = jnp.swapaxes(fcum.reshape(b, N_HEADS, tk_all // tkf, tkf), 1, 2)
    oc = _fox_call(r3(fq), kfx, vfx, f_q, f_k, wts["merge_g"][1:2], bd, tq, tkf, pos0)

    if past:
        hk, hv = cache["band_k"].reshape(b, -1, GW), cache["band_v"].reshape(b, -1, GW)
        keep = hk.shape[1]
        new["band_k"] = jnp.concatenate([hk, r3(bk)], axis=1)[:, -keep:]
        new["band_v"] = jnp.concatenate([hv, r3(bv)], axis=1)[:, -keep:]
    else:
        hk = hv = jnp.zeros((b, BAND_ROWS, GW), F32)
        keep = min(BAND_ROWS, t)
        new["band_k"], new["band_v"] = r3(bk)[:, -keep:], r3(bv)[:, -keep:]
    od = _band_call(r3(bq), r3(bk), r3(bv), hk, hv, wts["band_ext_" + cfg["name"]],
                    wts["merge_g"][2:3], bd, cfg["lq"], cfg["nch"], past)

    if past:
        st_ffn = jnp.pad(cache["ffn_conv"], ((0, 0), (6, 0), (0, 0)))
    else:
        st_ffn = jnp.zeros((b, 8, wts["w_up"].shape[1]), F32)
    ga1, sc2, sh2, ga2 = mods_out
    y, cnew = _out_call(x.reshape(m, d), oa, ob.reshape(m, GW), oc.reshape(m, GW), od.reshape(m, GW),
                        ga1, sc2, sh2, ga2, wts["norm_ffn_g"], wts["w_o"], wts["w_up"],
                        wts["ffn_conv_w"], wts["w_down"], st_ffn, cfg["tm_out"], t)
    new["ffn_conv"] = cnew[:, 6:8]
    hd = lambda a: a.reshape(a.shape[0], a.shape[1], N_HEADS, HEAD_DIM)
    for n in ("sb_k", "sb_v", "fox_k", "fox_v", "band_k", "band_v"):
        if n in new:
            new[n] = hd(new[n])
    return y.reshape(b, t, d), new, kv_t


STATE_KEYS = ("gdn_conv", "gdn_state", "sb_k", "sb_v", "fox_k", "fox_v", "fox_logf",
              "band_k", "band_v", "ffn_conv")


def _gdn_batch(b):
    return next(n for n in (8, 4, 2, 1) if b % n == 0)


def _group_cfg(name, b, t):
    if t % 512 == 0:
        return dict(name=name, tm_in=512, rows_per_mod_in=t, tm_out=512, t_gdn=t, tg=512,
                    bb=_gdn_batch(b), tq=256, tk=256, tk_fox=256, lq=CHUNK, nch=4)
    assert t % 16 == 0 and t <= CHUNK
    return dict(name=name, tm_in=b * t, rows_per_mod_in=b * t, tm_out=t, t_gdn=CHUNK, tg=CHUNK,
                bb=_gdn_batch(b), tq=t, tk=256, tk_fox=256, lq=t, nch=1)


def kernel(x_prompt, x_sample, c_prompt, c_sample, state_gdn_conv, state_gdn, cache_sb_k, cache_sb_v, cache_fox_k, cache_fox_v, cache_fox_logf, cache_band_k, cache_band_v, state_ffn_conv, ada_w, ada_b, norm_mix_g, w_in, gdn_conv_w, gdn_a_log, gdn_dt_bias, gdn_norm_g, fox_q_g, fox_k_g, fox_b_f, band_q_g, band_k_g, band_rel_bias, merge_g, w_o, norm_ffn_g, w_up, ffn_conv_w, w_down):
    depth = ada_w.shape[0]
    bp, tp, d = x_prompt.shape
    bs, ts, _ = x_sample.shape
    cfg_p = _group_cfg("p", bp, tp)
    cfg_s = _group_cfg("s", bs, ts)
    hist_s = cache_band_k.shape[2]

    mod = _ada_call(jnp.concatenate([c_prompt, c_sample], axis=0), ada_w, ada_b)
    mod = mod.reshape(depth, bp + bs, 6, d)

    tile_h = lambda g: jnp.tile(g.astype(F32), (1, N_HEADS))[:, None, :]
    w_in_p = _pack_in_weights(w_in)
    w_o_b, w_up_b, w_dn_b = _bf(w_o), _bf(w_up), _bf(w_down)
    ffn_cw = jnp.pad(ffn_conv_w.astype(F32), ((0, 0), (0, 5), (0, 0)))
    gdn_cw = jnp.pad(gdn_conv_w.astype(F32), ((0, 0), (0, 4), (0, 0)))
    qk_gains = jnp.concatenate([tile_h(fox_q_g), tile_h(fox_k_g), tile_h(band_q_g), tile_h(band_k_g),
                                jnp.zeros((depth, 4, GW), F32)], axis=1)
    bd = _const_bd256()

    y_p, y_s = x_prompt, x_sample
    kv_p = None
    new_p = {n: [] for n in STATE_KEYS}
    new_s = {n: [] for n in STATE_KEYS}
    for l in range(depth):
        wts = {
            "bd": bd, "norm_mix_g": norm_mix_g[l][None].astype(F32), "w_in": w_in_p[l],
            "qk_gains": qk_gains[l], "fox_b_f": _lane_vec(fox_b_f, 0)[l],
            "gdn_conv_w": gdn_cw[l], "gdn_a_log": _lane_vec(gdn_a_log, N_HEADS)[l],
            "gdn_dt_bias": _lane_vec(gdn_dt_bias, N_HEADS)[l], "gdn_norm_g": tile_h(gdn_norm_g)[l],
            "merge_g": merge_g[l].reshape(3, GW).astype(F32),
            "band_ext_p": _band_ext(band_rel_bias[l], cfg_p["lq"], BAND_ROWS),
            "band_ext_s": _band_ext(band_rel_bias[l], cfg_s["lq"], hist_s),
            "w_o": w_o_b[l], "norm_ffn_g": norm_ffn_g[l][None].astype(F32), "w_up": w_up_b[l],
            "ffn_conv_w": ffn_cw[l], "w_down": w_dn_b[l],
        }
        mp = mod[l, :bp]
        ms = mod[l, bp:]
        pm = lambda i: mp[:, i][:, None, :]
        mods_in_p = (pm(1), pm(0))
        mods_out_p = (pm(2), pm(4), pm(3), pm(5))
        srow = lambda i: jnp.repeat(ms[:, i], ts, axis=0)[None]
        sm1 = lambda i: ms[:, i][:, None, :]
        mods_in_s = (srow(1), srow(0))
        mods_out_s = (sm1(2), sm1(4), sm1(3), sm1(5))
        cache = {"gdn_conv": state_gdn_conv[l], "gdn_state": state_gdn[l],
                 "sb_k": cache_sb_k[l], "sb_v": cache_sb_v[l], "fox_k": cache_fox_k[l],
                 "fox_v": cache_fox_v[l], "fox_logf": cache_fox_logf[l],
                 "band_k": cache_band_k[l], "band_v": cache_band_v[l], "ffn_conv": state_ffn_conv[l]}
        y_p, st_p, kv_p = _layer(y_p, mods_in_p, mods_out_p, wts, None, cfg_p, (depth, l, tp, kv_p or None))
        y_s, st_s, _ = _layer(y_s, mods_in_s, mods_out_s, wts, cache, cfg_s)
        for n in STATE_KEYS:
            if n in st_p:
                new_p[n].append(st_p[n])
            new_s[n].append(st_s[n])
    out_p = {n: jnp.stack(v) for n, v in new_p.items() if v}
    for n, a in zip(("sb_k", "sb_v", "fox_k", "fox_v"), kv_p):
        out_p[n] = jnp.transpose(a.reshape(depth, bp, N_HEADS, HEAD_DIM, tp), (0, 1, 4, 2, 3))
    return ((y_p, y_s) + tuple(out_p[n] for n in STATE_KEYS)
            + tuple(jnp.stack(new_s[n]) for n in STATE_KEYS))
```

```python
import functools

import numpy as np
import jax
import jax.numpy as jnp
from jax import lax
from jax.experimental import pallas as pl
from jax.experimental.pallas import tpu as pltpu

F32 = jnp.float32
BF16 = jnp.bfloat16

HEAD_DIM = 64
N_HEADS = 4
GW = N_HEADS * HEAD_DIM
CHUNK = 64
BAND_ROWS = 512
REL_CLIP = 2 * CHUNK
EPS = 1e-6
NEG = -1e30
VMEM_LIMIT = 56 * 1024 * 1024
BAND_WIN = BAND_ROWS + 128
BAND_EXT = BAND_WIN + 128


def _bf(x):
    return x.astype(BF16)


def _dot(a, b):
    return jnp.dot(_bf(a), _bf(b), preferred_element_type=F32)


def _dot_nt(a, b):
    return lax.dot_general(_bf(a), _bf(b), (((1,), (1,)), ((), ())), preferred_element_type=F32)


def _split(x, n):
    parts = []
    for _ in range(n - 1):
        hi = _bf(x)
        parts.append(hi)
        x = x - hi.astype(F32)
    parts.append(_bf(x))
    return parts


def _dot_exact_rhs(x, m, n):
    return sum(jnp.dot(p, m, preferred_element_type=F32) for p in _split(x, n))


def _dot_exact_lhs(m, x, n):
    return sum(jnp.dot(m, p, preferred_element_type=F32) for p in _split(x, n))


def _sigmoid(x):
    return 1.0 / (1.0 + jnp.exp(-x))


def _silu(x):
    return x * _sigmoid(x)


def _softplus(x):
    return jnp.maximum(x, 0.0) + jnp.log(1.0 + jnp.exp(-jnp.abs(x)))


def _head_rms(o, bd, gain):
    ss = _dot_exact_rhs(o * o, bd, 2)
    return o * lax.rsqrt(ss * (1.0 / HEAD_DIM) + EPS) * gain


def _head_masks(width=GW):
    lane = lax.broadcasted_iota(jnp.int32, (1, width), 1)
    return [((lane & (GW - 1)) >> 6) == h for h in range(N_HEADS)]


def _params(sem, **kw):
    return pltpu.CompilerParams(dimension_semantics=sem, vmem_limit_bytes=VMEM_LIMIT, **kw)


def _np_bd(n, blk):
    i = np.arange(n)
    return (i[:, None] // blk == i[None, :] // blk)


def _const_bd256():
    return jnp.asarray(_np_bd(GW, HEAD_DIM), BF16)


def _const_tri_fwd(n):
    i = np.arange(n)
    return jnp.asarray(i[:, None] <= i[None, :], BF16)


def _const_tri_rev(n):
    i = np.arange(n)
    return jnp.asarray(i[:, None] >= i[None, :], BF16)


def _ada_kernel(c_ref, w_ref, b_ref, o_ref):
    s = _silu(c_ref[...])
    o_ref[0] = _dot(s, w_ref[0]) + b_ref[0]


def _ada_call(c_all, ada_w, ada_b):
    depth, d, n = ada_w.shape
    rows = c_all.shape[0]
    tn = 1536
    return pl.pallas_call(
        _ada_kernel,
        grid=(depth, n // tn),
        in_specs=[pl.BlockSpec((rows, d), lambda l, j: (0, 0)),
                  pl.BlockSpec((1, d, tn), lambda l, j: (l, 0, j)),
                  pl.BlockSpec((1, 1, tn), lambda l, j: (l, 0, j))],
        out_specs=pl.BlockSpec((1, rows, tn), lambda l, j: (l, 0, j)),
        out_shape=jax.ShapeDtypeStruct((depth, rows, n), F32),
        compiler_params=_params(("arbitrary", "arbitrary")),
        name="ada_modulation",
    )(c_all, ada_w, ada_b.reshape(depth, 1, n))


IN_GROUPS = 14
IN_COLS_PACKED = IN_GROUPS * GW


def _in_kernel(x_ref, sc_ref, sh_ref, g_ref, w_ref, gains_ref, bf_ref, bd_ref,
               gqkv_ref, gz_ref, small_ref, sbq_ref, sbk_ref, sbv_ref,
               fq_ref, fk_ref, fv_ref, logf_ref, bq_ref, bk_ref, bv_ref, *t_refs):
    x = x_ref[...]
    ms = jnp.mean(x * x, axis=-1, keepdims=True)
    h = x * lax.rsqrt(ms + EPS) * g_ref[...]
    h = _bf(h * (1.0 + sc_ref[0]) + sh_ref[0])
    bd = bd_ref[...]
    scale = HEAD_DIM ** -0.5

    def col(j, n=1):
        return jnp.dot(h, w_ref[:, j * GW:(j + n) * GW], preferred_element_type=F32)

    gqkv_ref[...] = col(0, 3)
    gz_ref[...] = col(3)
    sbq_ref[...] = _bf(col(4) * scale)
    def keep(ref, n, val):
        ref[...] = val
        if t_refs:
            t_refs[n][...] = val.T

    keep(sbk_ref, 0, col(5))
    keep(sbv_ref, 1, col(6))
    fq_ref[...] = _bf(_head_rms(col(7), bd, gains_ref[0:1, :]) * scale)
    keep(fk_ref, 2, _head_rms(col(8), bd, gains_ref[1:2, :]))
    keep(fv_ref, 3, col(9))
    bq_ref[...] = _bf(_head_rms(col(10), bd, gains_ref[2:3, :]) * scale)
    bk_ref[...] = _head_rms(col(11), bd, gains_ref[3:4, :])
    bv_ref[...] = col(12)
    sm = col(13)
    small_ref[...] = sm[:, :128]
    lf = sm[:, 128:] + bf_ref[...]
    logf = jnp.minimum(lf, 0.0) - jnp.log(1.0 + jnp.exp(-jnp.abs(lf)))
    logf_ref[...] = logf[:, :N_HEADS]


N_STACKED = 4


def _in_kernel_aliased(*refs):
    n_in = 8
    _in_kernel(*refs[:n_in], *refs[n_in + N_STACKED:])


def _in_call(x2, sc, sh, g, w, gains, bfv, bd, tm, rows_per_mod, stack=None):
    m, d = x2.shape
    r = sc.shape[1]
    tiles_per_mod = rows_per_mod // tm
    mod_spec = pl.BlockSpec((1, r, d), lambda i: (i // tiles_per_mod, 0, 0))
    full = lambda a: pl.BlockSpec(a.shape, lambda i: (0,) * a.ndim)
    row = lambda n: pl.BlockSpec((tm, n), lambda i: (i, 0))
    out_defs = [(3 * GW, F32), (GW, F32), (128, F32), (GW, BF16), (GW, F32), (GW, F32),
                (GW, BF16), (GW, F32), (GW, F32), (N_HEADS, F32), (GW, BF16), (GW, F32), (GW, F32)]
    out_specs = [row(n) for n, _ in out_defs]
    out_shape = [jax.ShapeDtypeStruct((m, n), dt) for n, dt in out_defs]
    in_specs = [row(d), mod_spec, mod_spec, full(g), full(w), full(gains), full(bfv), full(bd)]
    args = [x2, sc, sh, g, w, gains, bfv, bd]
    body, aliases = _in_kernel, {}
    if stack is not None:
        depth, layer, t, bufs = stack
        tps = t // tm
        tspec = pl.BlockSpec((None, None, GW, tm), lambda i: (layer, i // tps, 0, i % tps))
        if bufs is not None:
            body = _in_kernel_aliased
            in_specs += [pl.BlockSpec(memory_space=pl.ANY)] * N_STACKED
            aliases = {len(args) + n: len(out_specs) + n for n in range(N_STACKED)}
            args += list(bufs)
        out_specs += [tspec] * N_STACKED
        out_shape += [jax.ShapeDtypeStruct((depth, m // t, GW, t), F32)] * N_STACKED
    return pl.pallas_call(
        body,
        grid=(m // tm,),
        in_specs=in_specs,
        out_specs=out_specs,
        out_shape=out_shape,
        input_output_aliases=aliases,
        compiler_params=_params(("arbitrary",)),
        name="in_projection",
    )(*args)


def _cumsum_kernel(x_ref, u_ref, o_ref):
    r, tk = x_ref.shape
    carry = jnp.zeros((r, 1), F32)
    for blk in range(tk // 256):
        o = _dot_exact_rhs(x_ref[:, blk * 256:(blk + 1) * 256], u_ref[...], 3) + carry
        o_ref[:, blk * 256:(blk + 1) * 256] = o
        carry = o[:, 255:256]


def _cumsum_call(x):
    return pl.pallas_call(
        _cumsum_kernel,
        out_shape=jax.ShapeDtypeStruct(x.shape, F32),
        compiler_params=pltpu.CompilerParams(vmem_limit_bytes=VMEM_LIMIT),
        name="logf_cumsum",
    )(x, _const_tri_fwd(256))


def _fill_kv(k_ref, v_ref, kb_ref, vm_ref, hmask, tk):
    for kbi in range(k_ref.shape[1] // tk):
        rows = slice(kbi * tk, (kbi + 1) * tk)
        kb_ref[rows, :] = _bf(k_ref[0, rows, :])
        vv = v_ref[0, rows, :]
        for h in range(N_HEADS):
            vm_ref[kbi, h * tk:(h + 1) * tk, :] = _bf(jnp.where(hmask[h], vv, 0.0))


def _stack_heads(q, hmask):
    return jnp.concatenate([jnp.where(hmask[h], q, jnp.zeros_like(q)) for h in range(N_HEADS)], axis=0)


def _heads_to_lanes(w, tq):
    return jnp.concatenate([w[h * tq:(h + 1) * tq] for h in range(N_HEADS)], axis=1)


def _spread_heads(col, hmask, tq):
    out = col[(N_HEADS - 1) * tq:]
    for h in range(N_HEADS - 2, -1, -1):
        out = jnp.where(hmask[h], col[h * tq:(h + 1) * tq], out)
    return jnp.broadcast_to(out, (tq, GW))


def _stacked_positions(q0, kbi, tq, tk):
    rows = lax.broadcasted_iota(jnp.int32, (N_HEADS * tq, tk), 0) & (tq - 1)
    cols = lax.broadcasted_iota(jnp.int32, (N_HEADS * tq, tk), 1)
    return q0 + rows, kbi * tk + cols


def _sb_kernel(q_ref, k_ref, v_ref, u_ref, bd_ref, mg_ref, o_ref,
               kb_ref, vm_ref, acc_ref, car_ref, *, tq, tk, pos0):
    qi = pl.program_id(1)
    hmask = _head_masks()

    @pl.when(qi == 0)
    def _():
        _fill_kv(k_ref, v_ref, kb_ref, vm_ref, hmask, tk)

    nqs = _stack_heads(-q_ref[0], hmask)
    acc_ref[...] = jnp.zeros_like(acc_ref)
    car_ref[...] = jnp.zeros_like(car_ref)
    q0 = pos0 + qi * tq
    n_full = q0 // tk

    def scores(kbi):
        start = pl.multiple_of(kbi * tk, tk)
        return _dot_nt(nqs, kb_ref[pl.ds(start, tk), :])

    def log_rest(kbi, y, masked):
        neg_abs = pltpu.bitcast(pltpu.bitcast(y, jnp.uint32) | jnp.uint32(0x80000000), F32)
        r = jnp.minimum(y, 0.0) - jnp.log(1.0 + jnp.exp(neg_abs))
        valid = None
        if masked:
            rowp, colp = _stacked_positions(q0, kbi, tq, tk)
            valid = colp < rowp
            r = jnp.where(valid, r, 0.0)
        rc = jnp.dot(_bf(r), u_ref[...], preferred_element_type=F32)
        return rc, valid

    def weights(kbi, y, rc, valid):
        car = car_ref[...]
        w = jnp.exp(rc + jnp.concatenate([car] * (tk // 128), axis=1) - y)
        if valid is not None:
            w = jnp.where(valid, w, 0.0)
        car_ref[...] = car + jnp.broadcast_to(rc[:, 0:1], car.shape)
        acc_ref[...] += jnp.dot(_heads_to_lanes(_bf(w), tq), vm_ref[kbi], preferred_element_type=F32)

    def blocks(kbis, masked):
        ys = [scores(kbi) for kbi in kbis]
        pend = None
        for kbi, y, mk in zip(kbis, ys, masked):
            rc, valid = log_rest(kbi, y, mk)
            if pend is not None:
                weights(*pend)
            pend = (kbi, y, rc, valid)
        weights(*pend)

    @pl.when(n_full == 0)
    def _():
        blocks([n_full], [True])

    @pl.when(n_full > 0)
    def _():
        blocks([n_full, n_full - 1], [True, False])

    rest = jnp.maximum(n_full - 1, 0)

    def body(i, c):
        hi = rest - 1 - 2 * i
        blocks([hi, hi - 1], [False, False])
        return c

    lax.fori_loop(0, rest // 2, body, 0)

    @pl.when(rest % 2 == 1)
    def _():
        blocks([0], [False])

    o_ref[0] = _bf(_head_rms(acc_ref[...], bd_ref[...], mg_ref[...]))


def _fox_kernel(q_ref, k_ref, v_ref, fq_ref, fk_ref, bd_ref, mg_ref, o_ref,
                kb_ref, vm_ref, acc_ref, fqb_ref, *, tq, tk, pos0):
    qi = pl.program_id(1)
    hmask = _head_masks()

    @pl.when(qi == 0)
    def _():
        _fill_kv(k_ref, v_ref, kb_ref, vm_ref, hmask, tk)

    qs = _stack_heads(q_ref[0], hmask)
    fq = fq_ref[0]
    for h in range(N_HEADS):
        fqb_ref[h * tq:(h + 1) * tq, :] = jnp.broadcast_to(fq[:, h:h + 1], (tq, tk))
    acc_ref[...] = jnp.zeros_like(acc_ref)
    q0 = pos0 + qi * tq
    n_full = q0 // tk

    def scores(kbi):
        start = pl.multiple_of(kbi * tk, tk)
        return _dot_nt(qs, kb_ref[pl.ds(start, tk), :])

    def update(kbi, z, masked, ml):
        m_old, l_old = ml
        fk = fk_ref[0, kbi]
        s = jnp.concatenate([z[h * tq:(h + 1) * tq] + fqb_ref[h * tq:(h + 1) * tq, :] - fk[h:h + 1, :]
                             for h in range(N_HEADS)], axis=0)
        if masked:
            rowp, colp = _stacked_positions(q0, kbi, tq, tk)
            s = jnp.where(colp <= rowp, s, NEG)
        m_new = jnp.maximum(m_old, jnp.max(s, axis=1, keepdims=True))
        alpha = jnp.exp(m_old - m_new)
        p = jnp.exp(s - m_new)
        l_new = alpha * l_old + jnp.sum(p, axis=1, keepdims=True)
        pv = jnp.dot(_heads_to_lanes(_bf(p), tq), vm_ref[kbi], preferred_element_type=F32)
        acc_ref[...] = acc_ref[...] * _spread_heads(alpha, hmask, tq) + pv
        return m_new, l_new

    def pair(i, ml):
        za, zb = scores(2 * i), scores(2 * i + 1)
        return update(2 * i + 1, zb, False, update(2 * i, za, False, ml))

    ml = (jnp.full((N_HEADS * tq, 1), NEG, F32), jnp.zeros((N_HEADS * tq, 1), F32))
    rest = jnp.maximum(n_full - 1, 0)
    ml = lax.fori_loop(0, rest // 2, pair, ml)
    ml = lax.cond(rest % 2 == 1, lambda c: update(rest - 1, scores(rest - 1), False, c), lambda c: c, ml)

    def last_two(c):
        za, zb = scores(n_full - 1), scores(n_full)
        return update(n_full, zb, True, update(n_full - 1, za, False, c))

    _, l_fin = lax.cond(n_full > 0, last_two, lambda c: update(n_full, scores(n_full), True, c), ml)
    o = acc_ref[...] / _spread_heads(l_fin, hmask, tq)
    o_ref[0] = _bf(_head_rms(o, bd_ref[...], mg_ref[...]))


def _attn_specs(tk_all, tq):
    qspec = pl.BlockSpec((1, tq, GW), lambda bi, qi: (bi, qi, 0))
    kspec = pl.BlockSpec((1, tk_all, GW), lambda bi, qi: (bi, 0, 0))
    return qspec, kspec


def _sb_call(q, k, v, mg, bd, tq, tk, pos0):
    b, tq_all, _ = q.shape
    tk_all = k.shape[1]
    qspec, kspec = _attn_specs(tk_all, tq)
    full = lambda a: pl.BlockSpec(a.shape, lambda bi, qi: (0,) * a.ndim)
    u = _const_tri_rev(tk)
    return pl.pallas_call(
        functools.partial(_sb_kernel, tq=tq, tk=tk, pos0=pos0),
        grid=(b, tq_all // tq),
        in_specs=[qspec, kspec, kspec, full(u), full(bd), full(mg)],
        out_specs=qspec,
        out_shape=jax.ShapeDtypeStruct((b, tq_all, GW), BF16),
        scratch_shapes=[pltpu.VMEM((tk_all, GW), BF16), pltpu.VMEM((tk_all // tk, N_HEADS * tk, GW), BF16),
                        pltpu.VMEM((tq, GW), F32), pltpu.VMEM((N_HEADS * tq, 128), F32)],
        compiler_params=_params(("arbitrary", "arbitrary")),
        name="stick_breaking_attention",
    )(q, k, v, u, bd, mg)


def _fox_call(q, k, v, fq, fk, mg, bd, tq, tk, pos0):
    b, tq_all, _ = q.shape
    tk_all = k.shape[1]
    qspec, kspec = _attn_specs(tk_all, tq)
    full = lambda a: pl.BlockSpec(a.shape, lambda bi, qi: (0,) * a.ndim)
    return pl.pallas_call(
        functools.partial(_fox_kernel, tq=tq, tk=tk, pos0=pos0),
        grid=(b, tq_all // tq),
        in_specs=[qspec, kspec, kspec,
                  pl.BlockSpec((1, tq, N_HEADS), lambda bi, qi: (bi, qi, 0)),
                  pl.BlockSpec((1, tk_all // tk, N_HEADS, tk), lambda bi, qi: (bi, 0, 0, 0)),
                  full(bd), full(mg)],
        out_specs=qspec,
        out_shape=jax.ShapeDtypeStruct((b, tq_all, GW), BF16),
        scratch_shapes=[pltpu.VMEM((tk_all, GW), BF16), pltpu.VMEM((tk_all // tk, N_HEADS * tk, GW), BF16),
                        pltpu.VMEM((tq, GW), F32), pltpu.VMEM((N_HEADS * tq, tk), F32)],
        compiler_params=_params(("arbitrary", "arbitrary")),
        name="forgetting_attention",
    )(q, k, v, fq, fk, bd, mg)


def _band_kernel(q_ref, k_ref, v_ref, hk_ref, hv_ref, ext_ref, bd_ref, mg_ref, o_ref,
                 kp_ref, vp_ref, bias_ref, *, lq, nch, has_hist):
    qi = pl.program_id(1)
    hmask = _head_masks()
    t = k_ref.shape[1]
    hist = hk_ref.shape[1]
    ext_w = ext_ref.shape[1]

    @pl.when(qi == 0)
    def _():
        def put(r0, kk, vv):
            n = kk.shape[0]
            kp_ref[r0:r0 + n, :] = _bf(kk)
            vp_ref[r0:r0 + n, :] = _bf(vv)
        step = min(256, t)
        for r in range(0, hist, 256):
            put(r, hk_ref[0, r:r + 256, :], hv_ref[0, r:r + 256, :])
        for r in range(0, t, step):
            put(hist + r, k_ref[0, r:r + step, :], v_ref[0, r:r + step, :])
        zeros = jnp.zeros((128, GW), F32)
        put(hist + t, zeros, zeros)
        for h in range(N_HEADS):
            e = jnp.broadcast_to(ext_ref[h:h + 1, :], (lq, ext_w))
            e = pltpu.roll(e, ext_w - (lq - 1), 1, stride=1, stride_axis=0)
            bias_ref[h * lq:(h + 1) * lq, :] = e[:, :BAND_WIN]

    col = lax.broadcasted_iota(jnp.int32, (N_HEADS * lq, BAND_WIN), 1)
    chunks = range(nch)
    starts = [pl.multiple_of((qi * nch + ci) * lq, lq) for ci in chunks]
    s_all = [_dot_nt(_stack_heads(q_ref[0, ci * lq:(ci + 1) * lq, :], hmask),
                     kp_ref[pl.ds(starts[ci], BAND_WIN), :]) for ci in chunks]
    p_all = []
    for ci in chunks:
        valid = col < hist + lq
        if not has_hist:
            valid = jnp.logical_and(valid, col + (qi * nch + ci) * lq >= hist)
        s = jnp.where(valid, s_all[ci] + bias_ref[...], NEG)
        p = jnp.exp(s - jnp.max(s, axis=1, keepdims=True))
        p_all.append(_bf(p / jnp.sum(p, axis=1, keepdims=True)))
    pv_all = [jnp.dot(p_all[ci], vp_ref[pl.ds(starts[ci], BAND_WIN), :], preferred_element_type=F32)
              for ci in chunks]
    for ci in chunks:
        acc = jnp.zeros((lq, GW), F32)
        for h in range(N_HEADS):
            acc = acc + jnp.where(hmask[h], pv_all[ci][h * lq:(h + 1) * lq], 0.0)
        o_ref[0, ci * lq:(ci + 1) * lq, :] = _bf(_head_rms(acc, bd_ref[...], mg_ref[...]))


def _band_call(q, k, v, hk, hv, ext, mg, bd, lq, nch, has_hist):
    b, t, _ = q.shape
    hist = hk.shape[1]
    tqb = lq * nch
    qspec = pl.BlockSpec((1, tqb, GW), lambda bi, qi: (bi, qi, 0))
    kspec = pl.BlockSpec((1, t, GW), lambda bi, qi: (bi, 0, 0))
    hspec = pl.BlockSpec((1, hist, GW), lambda bi, qi: (bi, 0, 0))
    full = lambda a: pl.BlockSpec(a.shape, lambda bi, qi: (0,) * a.ndim)
    rows = hist + t + 128
    return pl.pallas_call(
        functools.partial(_band_kernel, lq=lq, nch=nch, has_hist=has_hist),
        grid=(b, t // tqb),
        in_specs=[qspec, kspec, kspec, hspec, hspec, full(ext), full(bd), full(mg)],
        out_specs=qspec,
        out_shape=jax.ShapeDtypeStruct((b, t, GW), BF16),
        scratch_shapes=[pltpu.VMEM((rows, GW), BF16), pltpu.VMEM((rows, GW), BF16),
                        pltpu.VMEM((N_HEADS * lq, BAND_WIN), F32)],
        compiler_params=_params(("arbitrary", "arbitrary")),
        name="band_attention",
    )(q, k, v, hk, hv, ext, bd, mg)


def _tile4(x):
    return jnp.concatenate([x] * N_HEADS, axis=0)


def _gdn1_kernel(qkv_ref, small_ref, st_ref, cw_ref, alog_ref, dtb_ref,
                 lblk_ref, eb_ref, eg_ref, bd_ref,
                 xu_ref, xwk_ref, qkd_ref, qg_ref, kdec_ref, egl_ref, conv_ref,
                 xbuf_ref, *, tg, t_valid, n_tiles):
    ti = pl.program_id(1)

    @pl.when(ti == 0)
    def _():
        xbuf_ref[0:8, :] = st_ref[0]

    xbuf_ref[8:8 + tg, :] = qkv_ref[0]
    y = cw_ref[3:4, :] * xbuf_ref[8:8 + tg, :]
    for i in range(3):
        y = y + cw_ref[i:i + 1, :] * xbuf_ref[5 + i:5 + i + tg, :]
    tv = t_valid - (n_tiles - 1) * tg

    @pl.when(ti == n_tiles - 1)
    def _():
        conv_ref[0] = xbuf_ref[8 + tv - 3:8 + tv, :]

    xbuf_ref[0:8, :] = xbuf_ref[tg:tg + 8, :]
    y = _silu(y)
    bd = bd_ref[...]
    q, k, v = y[:, :GW], y[:, GW:2 * GW], y[:, 2 * GW:]
    qn = q * lax.rsqrt(_dot_exact_rhs(q * q, bd, 2) + EPS) * (HEAD_DIM ** -0.5)
    kn = k * lax.rsqrt(_dot_exact_rhs(k * k, bd, 2) + EPS)

    sm = small_ref[0]
    rowv = ti * tg + lax.broadcasted_iota(jnp.int32, (tg, 1), 0) < t_valid
    beta_all = jnp.where(rowv, _sigmoid(sm), 0.0)
    g_all = jnp.where(rowv, -jnp.exp(alog_ref[...]) * _softplus(sm + dtb_ref[...]), 0.0)
    gc = _dot_exact_lhs(lblk_ref[...], g_all, 3)
    gl = jnp.concatenate([jnp.broadcast_to(gc[c * CHUNK + CHUNK - 1:(c + 1) * CHUNK, :], (CHUNK, 128))
                          for c in range(tg // CHUNK)], axis=0)
    bexp = _dot_exact_rhs(beta_all, eb_ref[...], 3)
    gx = _dot_exact_rhs(gc, eg_ref[...], 3)
    glx = _dot_exact_rhs(gl, eg_ref[...], 3)
    eg = jnp.exp(gx)
    kb = kn * bexp
    vb = v * bexp
    kbg = kb * eg
    qg_ref[0] = _bf(qn * eg)
    kdec_ref[0] = _bf(kn * jnp.exp(glx - gx))
    egl = jnp.exp(glx)

    ri = lax.broadcasted_iota(jnp.int32, (CHUNK, GW), 0)
    li = lax.broadcasted_iota(jnp.int32, (CHUNK, GW), 1) & (CHUNK - 1)
    eye = ri == li
    incl = ri >= li
    strict = ri > li
    rb = lax.broadcasted_iota(jnp.int32, (GW, GW), 0) >> 6
    bdm = rb == (lax.broadcasted_iota(jnp.int32, (GW, GW), 1) >> 6)
    bdm2 = jnp.concatenate([bdm, bdm], axis=1)

    def bd_weights(m):
        return [jnp.where(bdm, _tile4(part), jnp.zeros((), BF16)) for part in _split(m, 2)]

    def times_bd(a, m):
        ah, al = _split(a, 2)
        mh, ml = bd_weights(m)
        n = a.shape[0]
        top = jnp.dot(jnp.concatenate([ah, al], axis=0), mh, preferred_element_type=F32)
        return top[:n] + top[n:] + jnp.dot(ah, ml, preferred_element_type=F32)

    chunks = range(tg // CHUNK)
    sls = [slice(c * CHUNK, (c + 1) * CHUNK) for c in chunks]
    kq = [_dot_nt(jnp.concatenate([kb[sl], qn[sl]], axis=0), jnp.where(bdm, _tile4(kn[sl]), 0.0))
          for sl in sls]
    nmat, tmat = [], []
    for c in chunks:
        gxc = gx[sls[c]]
        grow = jnp.sum(jnp.where(eye, gxc, 0.0), axis=0, keepdims=True)
        dec = jnp.where(incl, jnp.exp(gxc - grow), 0.0)
        nmat.append(jnp.where(strict, kq[c][:CHUNK] * dec, 0.0))
        qkd_ref[0, sls[c], :] = _bf(kq[c][CHUNK:] * dec)
        tmat.append(jnp.where(eye, 1.0, 0.0) - nmat[c])
        egl_ref[0, c] = egl[c * CHUNK:c * CHUNK + 8, :]
    pmat = [times_bd(nmat[c], nmat[c]) for c in chunks]
    for step in range(5):
        if step < 4:
            tp = [times_bd(jnp.concatenate([tmat[c], pmat[c]], axis=0), pmat[c]) for c in chunks]
            tmat = [tmat[c] + tp[c][:CHUNK] for c in chunks]
            pmat = [tp[c][CHUNK:] for c in chunks]
        else:
            tmat = [tmat[c] + times_bd(tmat[c], pmat[c]) for c in chunks]
    xs = [_dot(tmat[c], jnp.where(bdm2, _tile4(jnp.concatenate([vb[sls[c]], kbg[sls[c]]], axis=1)), 0.0))
          for c in chunks]
    for c in chunks:
        xu_ref[0, sls[c], :] = xs[c][:, :GW]
        xwk_ref[0, sls[c], :] = _bf(xs[c][:, GW:])


def _gdn2_kernel(xu_ref, xwk_ref, qkd_ref, qg_ref, kdec_ref, egl_ref, z_ref, s0_ref, ng_ref, bd_ref,
                 o_ref, sfin_ref, s_ref, *, bb, nchunk):
    ti = pl.program_id(1)

    @pl.when(ti == 0)
    def _():
        s_ref[...] = s0_ref[...]

    rb = lax.broadcasted_iota(jnp.int32, (GW, GW), 0) >> 6
    bdm = rb == (lax.broadcasted_iota(jnp.int32, (GW, GW), 1) >> 6)
    bd = bd_ref[...]
    ng = ng_ref[...]

    def chunk(c, carry):
        r0 = pl.multiple_of(c * CHUNK, CHUNK)
        rows = pl.ds(r0, CHUNK)
        bs = range(bb)
        s_old = [s_ref[b] for b in bs]
        r = [jnp.dot(jnp.concatenate([xwk_ref[b, rows, :], qg_ref[b, rows, :]], axis=0), _bf(s_old[b]),
                     preferred_element_type=F32) for b in bs]
        vnew = [xu_ref[b, rows, :] - r[b][:CHUNK] for b in bs]
        o = [r[b][CHUNK:] + jnp.dot(qkd_ref[b, rows, :], _bf(jnp.where(bdm, _tile4(vnew[b]), 0.0)),
                                    preferred_element_type=F32) for b in bs]
        upd = [_dot(kdec_ref[b, rows, :].astype(F32).T, vnew[b]) for b in bs]
        for b in bs:
            s_ref[b] = egl_ref[b, c, 0:1, :] * s_old[b] + jnp.where(bdm, upd[b], 0.0)
            o_ref[b, rows, :] = _bf(_head_rms(o[b], bd, ng) * _silu(z_ref[b, rows, :]))
        return carry

    lax.fori_loop(0, nchunk, chunk, 0)
    sfin_ref[...] = s_ref[...]


def _gdn_consts(tg):
    i = np.arange(tg)
    same = i[:, None] // CHUNK == i[None, :] // CHUNK
    lblk = jnp.asarray(same & (i[:, None] >= i[None, :]), BF16)
    eb = np.zeros((128, GW), np.float32)
    eg = np.zeros((128, GW), np.float32)
    for h in range(N_HEADS):
        eb[h, h * HEAD_DIM:(h + 1) * HEAD_DIM] = 1
        eg[N_HEADS + h, h * HEAD_DIM:(h + 1) * HEAD_DIM] = 1
    return lblk, jnp.asarray(eb, BF16), jnp.asarray(eg, BF16)


def _gdn_call(qkv, small, z, st8, s0bd, cw, alog, dtb, ng, bd, tg, t_valid, bb):
    b, t, _ = qkv.shape
    n_tiles = t // tg
    lblk, eb, eg = _gdn_consts(tg)
    full = lambda a: pl.BlockSpec(a.shape, lambda bi, ti: (0,) * a.ndim)
    row = lambda n: pl.BlockSpec((1, tg, n), lambda bi, ti: (bi, ti, 0))
    nct = tg // CHUNK
    xu, xwk, qkd, qg, kdec, egl, conv_new = pl.pallas_call(
        functools.partial(_gdn1_kernel, tg=tg, t_valid=t_valid, n_tiles=n_tiles),
        grid=(b, n_tiles),
        in_specs=[row(3 * GW), row(128),
                  pl.BlockSpec((1, 8, 3 * GW), lambda bi, ti: (bi, 0, 0)),
                  full(cw), full(alog), full(dtb), full(lblk), full(eb), full(eg), full(bd)],
        out_specs=[row(GW), row(GW), row(GW), row(GW), row(GW),
                   pl.BlockSpec((1, nct, 8, GW), lambda bi, ti: (bi, ti, 0, 0)),
                   pl.BlockSpec((1, 3, 3 * GW), lambda bi, ti: (bi, 0, 0))],
        out_shape=[jax.ShapeDtypeStruct((b, t, GW), F32), jax.ShapeDtypeStruct((b, t, GW), BF16),
                   jax.ShapeDtypeStruct((b, t, GW), BF16), jax.ShapeDtypeStruct((b, t, GW), BF16),
                   jax.ShapeDtypeStruct((b, t, GW), BF16),
                   jax.ShapeDtypeStruct((b, t // CHUNK, 8, GW), F32),
                   jax.ShapeDtypeStruct((b, 3, 3 * GW), F32)],
        scratch_shapes=[pltpu.VMEM((8 + tg, 3 * GW), F32)],
        compiler_params=_params(("arbitrary", "arbitrary")),
        name="gdn_chunk_solve",
    )(qkv, small, st8, cw, alog, dtb, lblk, eb, eg, bd)

    full2 = lambda a: pl.BlockSpec(a.shape, lambda bi, ti: (0,) * a.ndim)
    rowb = pl.BlockSpec((bb, tg, GW), lambda bi, ti: (bi, ti, 0))
    sspec = pl.BlockSpec((bb, GW, GW), lambda bi, ti: (bi, 0, 0))
    o, sfin = pl.pallas_call(
        functools.partial(_gdn2_kernel, bb=bb, nchunk=nct),
        grid=(b // bb, n_tiles),
        in_specs=[rowb, rowb, rowb, rowb, rowb,
                  pl.BlockSpec((bb, nct, 8, GW), lambda bi, ti: (bi, ti, 0, 0)),
                  rowb, sspec, full2(ng), full2(bd)],
        out_specs=[rowb, sspec],
        out_shape=[jax.ShapeDtypeStruct((b, t, GW), BF16), jax.ShapeDtypeStruct((b, GW, GW), F32)],
        scratch_shapes=[pltpu.VMEM((bb, GW, GW), F32)],
        compiler_params=_params(("arbitrary", "arbitrary")),
        name="gdn_recurrence",
    )(xu, xwk, qkd, qg, kdec, egl, z, s0bd, ng, bd)
    return o, conv_new, sfin


def _out_kernel(x_ref, oa_ref, ob_ref, oc_ref, od_ref, ga1_ref, sc2_ref, sh2_ref, ga2_ref, g2_ref,
                wo_ref, wup_ref, cw_ref, wdn_ref, st_ref, y_ref, cnew_ref,
                ubuf_ref, car_ref, gs_ref, *, tm, tiles_per_seq, nff):
    i = pl.program_id(0)

    @pl.when(i % tiles_per_seq == 0)
    def _():
        car_ref[...] = st_ref[0]

    mix = jnp.dot(oa_ref[...], wo_ref[0:GW, :], preferred_element_type=F32)
    for n, ref in enumerate((ob_ref, oc_ref, od_ref)):
        mix = mix + jnp.dot(ref[...], wo_ref[(n + 1) * GW:(n + 2) * GW, :], preferred_element_type=F32)
    x1 = x_ref[...] + ga1_ref[0] * mix
    ms = jnp.mean(x1 * x1, axis=-1, keepdims=True)
    h = x1 * lax.rsqrt(ms + EPS) * g2_ref[...]
    h = _bf(h * (1.0 + sc2_ref[0]) + sh2_ref[0])
    dff = nff * GW
    for j in range(nff):
        halves = []
        for off in (0, dff):
            cols = slice(off + j * GW, off + (j + 1) * GW)
            u = jnp.dot(h, wup_ref[:, cols], preferred_element_type=F32)
            ubuf_ref[0:8, :] = car_ref[:, cols]
            ubuf_ref[8:8 + tm, :] = u
            halves.append(cw_ref[0:1, cols] * ubuf_ref[6:6 + tm, :] + cw_ref[1:2, cols] * ubuf_ref[7:7 + tm, :]
                          + cw_ref[2:3, cols] * u)
            car_ref[:, cols] = ubuf_ref[tm:tm + 8, :]
        gs_ref[:, j * GW:(j + 1) * GW] = _bf(_silu(halves[0]) * halves[1])
    y_ref[...] = x1 + ga2_ref[0] * jnp.dot(gs_ref[...], wdn_ref[...], preferred_element_type=F32)
    cnew_ref[0] = car_ref[...]


def _out_call(x2, oa, ob, oc, od, ga1, sc2, sh2, ga2, g2, wo, wup, cw, wdn, st8, tm, rows_per_seq):
    m, d = x2.shape
    dff2 = wup.shape[1]
    nff = dff2 // (2 * GW)
    tiles_per_seq = rows_per_seq // tm
    nseq = m // rows_per_seq
    r = ga1.shape[1]
    rows_per_mod = m // ga1.shape[0]
    tiles_per_mod = rows_per_mod // tm
    mod_spec = pl.BlockSpec((1, r, d), lambda i: (i // tiles_per_mod, 0, 0))
    full = lambda a: pl.BlockSpec(a.shape, lambda i: (0,) * a.ndim)
    row = lambda n: pl.BlockSpec((tm, n), lambda i: (i, 0))
    stspec = pl.BlockSpec((1, 8, dff2), lambda i: (i // tiles_per_seq, 0, 0))
    return pl.pallas_call(
        functools.partial(_out_kernel, tm=tm, tiles_per_seq=tiles_per_seq, nff=nff),
        grid=(m // tm,),
        in_specs=[row(d), row(GW), row(GW), row(GW), row(GW), mod_spec, mod_spec, mod_spec, mod_spec,
                  full(g2), full(wo), full(wup), full(cw), full(wdn), stspec],
        out_specs=[row(d), stspec],
        out_shape=[jax.ShapeDtypeStruct((m, d), F32), jax.ShapeDtypeStruct((nseq, 8, dff2), F32)],
        scratch_shapes=[pltpu.VMEM((8 + tm, GW), F32), pltpu.VMEM((8, dff2), F32),
                        pltpu.VMEM((tm, dff2 // 2), BF16)],
        compiler_params=_params(("arbitrary",)),
        name="out_projection_mlp",
    )(x2, oa, ob, oc, od, ga1, sc2, sh2, ga2, g2, wo, wup, cw, wdn, st8)


def _pack_in_weights(w_in):
    a = 4 * GW + 2 * N_HEADS
    off_b = a
    off_c = off_b + 3 * GW
    off_d = off_c + 3 * GW + N_HEADS
    dd = w_in.shape[:2]
    zeros = lambda n: jnp.zeros(dd + (n,), w_in.dtype)
    small = jnp.concatenate([w_in[..., 4 * GW:a], zeros(128 - 2 * N_HEADS),
                             w_in[..., off_c + 3 * GW:off_d], zeros(128 - N_HEADS)], axis=-1)
    packed = jnp.concatenate([w_in[..., :4 * GW], w_in[..., off_b:off_b + 3 * GW],
                              w_in[..., off_c:off_c + 3 * GW], w_in[..., off_d:off_d + 3 * GW], small],
                             axis=-1)
    return _bf(packed)


def _lane_vec(v, offset):
    depth = v.shape[0]
    out = jnp.zeros((depth, 1, 128), F32)
    return out.at[:, 0, offset:offset + N_HEADS].set(v.astype(F32))


def _band_ext(rel_table, lq, hist):
    c = np.arange(BAND_EXT)
    rel = np.clip(hist + lq - 1 - c, -REL_CLIP, REL_CLIP) + REL_CLIP
    return rel_table.astype(F32)[:, rel]


def _layer(x, mods_in, mods_out, wts, cache, cfg, stack=None):
    b, t, d = x.shape
    m = b * t
    bd = wts["bd"]
    sc1, sh1 = mods_in
    outs = _in_call(x.reshape(m, d), sc1, sh1, wts["norm_mix_g"], wts["w_in"], wts["qk_gains"],
                    wts["fox_b_f"], bd, cfg["tm_in"], cfg["rows_per_mod_in"], stack)
    (gqkv, gz, small, sbq, sbk, sbv, fq, fk, fv, logf, bq, bk, bv), kv_t = outs[:13], tuple(outs[13:])
    r3 = lambda a: a.reshape(b, t, a.shape[-1])
    new = {"fox_logf": r3(logf)}
    if stack is None:
        new.update({"sb_k": r3(sbk), "sb_v": r3(sbv), "fox_k": r3(fk), "fox_v": r3(fv)})
    past = cache is not None

    tp = cfg["t_gdn"]
    pad_t = lambda a: jnp.pad(r3(a), ((0, 0), (0, tp - t), (0, 0)))
    if past:
        st8 = jnp.pad(cache["gdn_conv"], ((0, 0), (5, 0), (0, 0)))
        s0 = cache["gdn_state"]
        eye = jnp.eye(N_HEADS, dtype=F32)
        s0bd = jnp.einsum("bhkv,hg->bhkgv", s0, eye).reshape(b, GW, GW)
    else:
        st8 = jnp.zeros((b, 8, 3 * GW), F32)
        s0bd = jnp.zeros((b, GW, GW), F32)
    oa, conv_new, sfin = _gdn_call(pad_t(gqkv), pad_t(small), pad_t(gz), st8, s0bd,
                                   wts["gdn_conv_w"], wts["gdn_a_log"], wts["gdn_dt_bias"],
                                   wts["gdn_norm_g"], bd, cfg["tg"], t, cfg["bb"])
    oa = oa[:, :t].reshape(m, GW)
    sfin = sfin.reshape(b, N_HEADS, HEAD_DIM, N_HEADS, HEAD_DIM)
    new["gdn_state"] = jnp.stack([sfin[:, h, :, h, :] for h in range(N_HEADS)], axis=1)
    new["gdn_conv"] = conv_new

    tq, tk, tkf = cfg["tq"], cfg["tk"], cfg["tk_fox"]
    if past:
        pos0 = cache["sb_k"].shape[1]
        tk_all = -(-(pos0 + t) // tk) * tk
        cat = lambda c, n: jnp.pad(jnp.concatenate([c.reshape(b, pos0, -1), r3(n)], axis=1),
                                   ((0, 0), (0, tk_all - pos0 - t), (0, 0)))
        ksb, vsb = cat(cache["sb_k"], sbk), cat(cache["sb_v"], sbv)
        kfx, vfx = cat(cache["fox_k"], fk), cat(cache["fox_v"], fv)
        lf_all = cat(cache["fox_logf"], logf)
    else:
        pos0, tk_all = 0, t
        ksb, vsb, kfx, vfx, lf_all = r3(sbk), r3(sbv), r3(fk), r3(fv), r3(logf)
    ob = _sb_call(r3(sbq), ksb, vsb, wts["merge_g"][0:1], bd, tq, tk, pos0)
    fcum = _cumsum_call(jnp.swapaxes(lf_all, 1, 2).reshape(b * N_HEADS, tk_all))
    fcum = fcum.reshape(b, N_HEADS, tk_all)
    f_q = jnp.swapaxes(fcum[:, :, pos0:pos0 + t], 1, 2)
    f_k = jnp.swapaxes(fcum.reshape(b, N_HEADS, tk_all // tkf, tkf), 1, 2)
    oc = _fox_call(r3(fq), kfx, vfx, f_q, f_k, wts["merge_g"][1:2], bd, tq, tkf, pos0)

    if past:
        hk, hv = cache["band_k"].reshape(b, -1, GW), cache["band_v"].reshape(b, -1, GW)
        keep = hk.shape[1]
        new["band_k"] = jnp.concatenate([hk, r3(bk)], axis=1)[:, -keep:]
        new["band_v"] = jnp.concatenate([hv, r3(bv)], axis=1)[:, -keep:]
    else:
        hk = hv = jnp.zeros((b, BAND_ROWS, GW), F32)
        keep = min(BAND_ROWS, t)
        new["band_k"], new["band_v"] = r3(bk)[:, -keep:], r3(bv)[:, -keep:]
    od = _band_call(r3(bq), r3(bk), r3(bv), hk, hv, wts["band_ext_" + cfg["name"]],
                    wts["merge_g"][2:3], bd, cfg["lq"], cfg["nch"], past)

    if past:
        st_ffn = jnp.pad(cache["ffn_conv"], ((0, 0), (6, 0), (0, 0)))
    else:
        st_ffn = jnp.zeros((b, 8, wts["w_up"].shape[1]), F32)
    ga1, sc2, sh2, ga2 = mods_out
    y, cnew = _out_call(x.reshape(m, d), oa, ob.reshape(m, GW), oc.reshape(m, GW), od.reshape(m, GW),
                        ga1, sc2, sh2, ga2, wts["norm_ffn_g"], wts["w_o"], wts["w_up"],
                        wts["ffn_conv_w"], wts["w_down"], st_ffn, cfg["tm_out"], t)
    new["ffn_conv"] = cnew[:, 6:8]
    hd = lambda a: a.reshape(a.shape[0], a.shape[1], N_HEADS, HEAD_DIM)
    for n in ("sb_k", "sb_v", "fox_k", "fox_v", "band_k", "band_v"):
        if n in new:
            new[n] = hd(new[n])
    return y.reshape(b, t, d), new, kv_t


STATE_KEYS = ("gdn_conv", "gdn_state", "sb_k", "sb_v", "fox_k", "fox_v", "fox_logf",
              "band_k", "band_v", "ffn_conv")


def _gdn_batch(b):
    return next(n for n in (8, 4, 2, 1) if b % n == 0)


def _group_cfg(name, b, t):
    if t % 512 == 0:
        return dict(name=name, tm_in=512, rows_per_mod_in=t, tm_out=512, t_gdn=t, tg=512,
                    bb=_gdn_batch(b), tq=256, tk=256, tk_fox=256, lq=CHUNK, nch=4)
    assert t % 16 == 0 and t <= CHUNK
    return dict(name=name, tm_in=b * t, rows_per_mod_in=b * t, tm_out=t, t_gdn=CHUNK, tg=CHUNK,
                bb=_gdn_batch(b), tq=t, tk=256, tk_fox=256, lq=t, nch=1)


def kernel(x_prompt, x_sample, c_prompt, c_sample, state_gdn_conv, state_gdn, cache_sb_k, cache_sb_v, cache_fox_k, cache_fox_v, cache_fox_logf, cache_band_k, cache_band_v, state_ffn_conv, ada_w, ada_b, norm_mix_g, w_in, gdn_conv_w, gdn_a_log, gdn_dt_bias, gdn_norm_g, fox_q_g, fox_k_g, fox_b_f, band_q_g, band_k_g, band_rel_bias, merge_g, w_o, norm_ffn_g, w_up, ffn_conv_w, w_down):
    depth = ada_w.shape[0]
    bp, tp, d = x_prompt.shape
    bs, ts, _ = x_sample.shape
    cfg_p = _group_cfg("p", bp, tp)
    cfg_s = _group_cfg("s", bs, ts)
    hist_s = cache_band_k.shape[2]

    mod = _ada_call(jnp.concatenate([c_prompt, c_sample], axis=0), ada_w, ada_b)
    mod = mod.reshape(depth, bp + bs, 6, d)

    tile_h = lambda g: jnp.tile(g.astype(F32), (1, N_HEADS))[:, None, :]
    w_in_p = _pack_in_weights(w_in)
    w_o_b, w_up_b, w_dn_b = _bf(w_o), _bf(w_up), _bf(w_down)
    ffn_cw = jnp.pad(ffn_conv_w.astype(F32), ((0, 0), (0, 5), (0, 0)))
    gdn_cw = jnp.pad(gdn_conv_w.astype(F32), ((0, 0), (0, 4), (0, 0)))
    qk_gains = jnp.concatenate([tile_h(fox_q_g), tile_h(fox_k_g), tile_h(band_q_g), tile_h(band_k_g),
                                jnp.zeros((depth, 4, GW), F32)], axis=1)
    bd = _const_bd256()

    y_p, y_s = x_prompt, x_sample
    kv_p = None
    new_p = {n: [] for n in STATE_KEYS}
    new_s = {n: [] for n in STATE_KEYS}
    for l in range(depth):
        wts = {
            "bd": bd, "norm_mix_g": norm_mix_g[l][None].astype(F32), "w_in": w_in_p[l],
            "qk_gains": qk_gains[l], "fox_b_f": _lane_vec(fox_b_f, 0)[l],
            "gdn_conv_w": gdn_cw[l], "gdn_a_log": _lane_vec(gdn_a_log, N_HEADS)[l],
            "gdn_dt_bias": _lane_vec(gdn_dt_bias, N_HEADS)[l], "gdn_norm_g": tile_h(gdn_norm_g)[l],
            "merge_g": merge_g[l].reshape(3, GW).astype(F32),
            "band_ext_p": _band_ext(band_rel_bias[l], cfg_p["lq"], BAND_ROWS),
            "band_ext_s": _band_ext(band_rel_bias[l], cfg_s["lq"], hist_s),
            "w_o": w_o_b[l], "norm_ffn_g": norm_ffn_g[l][None].astype(F32), "w_up": w_up_b[l],
            "ffn_conv_w": ffn_cw[l], "w_down": w_dn_b[l],
        }
        mp = mod[l, :bp]
        ms = mod[l, bp:]
        pm = lambda i: mp[:, i][:, None, :]
        mods_in_p = (pm(1), pm(0))
        mods_out_p = (pm(2), pm(4), pm(3), pm(5))
        srow = lambda i: jnp.repeat(ms[:, i], ts, axis=0)[None]
        sm1 = lambda i: ms[:, i][:, None, :]
        mods_in_s = (srow(1), srow(0))
        mods_out_s = (sm1(2), sm1(4), sm1(3), sm1(5))
        cache = {"gdn_conv": state_gdn_conv[l], "gdn_state": state_gdn[l],
                 "sb_k": cache_sb_k[l], "sb_v": cache_sb_v[l], "fox_k": cache_fox_k[l],
                 "fox_v": cache_fox_v[l], "fox_logf": cache_fox_logf[l],
                 "band_k": cache_band_k[l], "band_v": cache_band_v[l], "ffn_conv": state_ffn_conv[l]}
        y_p, st_p, kv_p = _layer(y_p, mods_in_p, mods_out_p, wts, None, cfg_p, (depth, l, tp, kv_p or None))
        y_s, st_s, _ = _layer(y_s, mods_in_s, mods_out_s, wts, cache, cfg_s)
        for n in STATE_KEYS:
            if n in st_p:
                new_p[n].append(st_p[n])
            new_s[n].append(st_s[n])
    out_p = {n: jnp.stack(v) for n, v in new_p.items() if v}
    for n, a in zip(("sb_k", "sb_v", "fox_k", "fox_v"), kv_p):
        out_p[n] = jnp.transpose(a.reshape(depth, bp, N_HEADS, HEAD_DIM, tp), (0, 1, 4, 2, 3))
    return ((y_p, y_s) + tuple(out_p[n] for n in STATE_KEYS)
            + tuple(jnp.stack(new_s[n]) for n in STATE_KEYS))
```

```python
import functools

import numpy as np
import jax
import jax.numpy as jnp
from jax import lax
from jax.experimental import pallas as pl
from jax.experimental.pallas import tpu as pltpu

F32 = jnp.float32
BF16 = jnp.bfloat16

HEAD_DIM = 64
N_HEADS = 4
GW = N_HEADS * HEAD_DIM
CHUNK = 64
BAND_ROWS = 512
REL_CLIP = 2 * CHUNK
EPS = 1e-6
NEG = -1e30
VMEM_LIMIT = 56 * 1024 * 1024
BAND_WIN = BAND_ROWS + 128
BAND_EXT = BAND_WIN + 128


def _bf(x):
    return x.astype(BF16)


def _dot(a, b):
    return jnp.dot(_bf(a), _bf(b), preferred_element_type=F32)


def _dot_nt(a, b):
    return lax.dot_general(_bf(a), _bf(b), (((1,), (1,)), ((), ())), preferred_element_type=F32)


def _split(x, n):
    parts = []
    for _ in range(n - 1):
        hi = _bf(x)
        parts.append(hi)
        x = x - hi.astype(F32)
    parts.append(_bf(x))
    return parts


def _dot_exact_rhs(x, m, n):
    return sum(jnp.dot(p, m, preferred_element_type=F32) for p in _split(x, n))


def _dot_exact_lhs(m, x, n):
    return sum(jnp.dot(m, p, preferred_element_type=F32) for p in _split(x, n))


def _sigmoid(x):
    return 1.0 / (1.0 + jnp.exp(-x))


def _silu(x):
    return x * _sigmoid(x)


def _softplus(x):
    return jnp.maximum(x, 0.0) + jnp.log(1.0 + jnp.exp(-jnp.abs(x)))


def _head_rms(o, bd, gain, passes=1):
    ss = _dot_exact_rhs(o * o, bd, passes)
    return o * lax.rsqrt(ss * (1.0 / HEAD_DIM) + EPS) * gain


def _head_masks(width=GW):
    lane = lax.broadcasted_iota(jnp.int32, (1, width), 1)
    return [((lane & (GW - 1)) >> 6) == h for h in range(N_HEADS)]


def _params(sem, **kw):
    return pltpu.CompilerParams(dimension_semantics=sem, vmem_limit_bytes=VMEM_LIMIT, **kw)


def _np_bd(n, blk):
    i = np.arange(n)
    return (i[:, None] // blk == i[None, :] // blk)


def _const_bd256():
    return jnp.asarray(_np_bd(GW, HEAD_DIM), BF16)


def _const_tri_fwd(n):
    i = np.arange(n)
    return jnp.asarray(i[:, None] <= i[None, :], BF16)


def _const_tri_rev(n):
    i = np.arange(n)
    return jnp.asarray(i[:, None] >= i[None, :], BF16)


def _ada_kernel(c_ref, w_ref, b_ref, o_ref):
    s = _silu(c_ref[...])
    o_ref[0] = _dot(s, w_ref[0]) + b_ref[0]


def _ada_call(c_all, ada_w, ada_b):
    depth, d, n = ada_w.shape
    rows = c_all.shape[0]
    tn = 1536
    return pl.pallas_call(
        _ada_kernel,
        grid=(depth, n // tn),
        in_specs=[pl.BlockSpec((rows, d), lambda l, j: (0, 0)),
                  pl.BlockSpec((1, d, tn), lambda l, j: (l, 0, j)),
                  pl.BlockSpec((1, 1, tn), lambda l, j: (l, 0, j))],
        out_specs=pl.BlockSpec((1, rows, tn), lambda l, j: (l, 0, j)),
        out_shape=jax.ShapeDtypeStruct((depth, rows, n), F32),
        compiler_params=_params(("arbitrary", "arbitrary")),
        name="ada_modulation",
    )(c_all, ada_w, ada_b.reshape(depth, 1, n))


IN_GROUPS = 14
IN_COLS_PACKED = IN_GROUPS * GW


def _in_kernel(x_ref, sc_ref, sh_ref, g_ref, w_ref, gains_ref, bf_ref, bd_ref,
               gqkv_ref, gz_ref, small_ref, sbq_ref, sbk_ref, sbv_ref,
               fq_ref, fk_ref, fv_ref, logf_ref, bq_ref, bk_ref, bv_ref, *t_refs):
    x = x_ref[...]
    ms = jnp.mean(x * x, axis=-1, keepdims=True)
    h = x * lax.rsqrt(ms + EPS) * g_ref[...]
    h = _bf(h * (1.0 + sc_ref[0]) + sh_ref[0])
    bd = bd_ref[...]
    scale = HEAD_DIM ** -0.5

    def col(j, n=1):
        return jnp.dot(h, w_ref[:, j * GW:(j + n) * GW], preferred_element_type=F32)

    gqkv_ref[...] = col(0, 3)
    gz_ref[...] = col(3)
    sbq_ref[...] = _bf(col(4) * scale)
    def keep(ref, n, val):
        ref[...] = val
        if t_refs:
            t_refs[n][...] = val.T

    keep(sbk_ref, 0, col(5))
    keep(sbv_ref, 1, col(6))
    fq_ref[...] = _bf(_head_rms(col(7), bd, gains_ref[0:1, :], 2) * scale)
    keep(fk_ref, 2, _head_rms(col(8), bd, gains_ref[1:2, :], 2))
    keep(fv_ref, 3, col(9))
    bq_ref[...] = _bf(_head_rms(col(10), bd, gains_ref[2:3, :], 2) * scale)
    bk = _head_rms(col(11), bd, gains_ref[3:4, :], 2)
    bv = col(12)
    bk_ref[...] = bk
    bv_ref[...] = bv
    if t_refs:
        t_refs[4][...] = bk.T
        t_refs[5][...] = bv.T
    sm = col(13)
    small_ref[...] = sm[:, :128]
    lf = sm[:, 128:] + bf_ref[...]
    logf = jnp.minimum(lf, 0.0) - jnp.log(1.0 + jnp.exp(-jnp.abs(lf)))
    logf_ref[...] = logf[:, :N_HEADS]


N_STACKED = 6


def _in_kernel_aliased(*refs):
    n_in = 8
    _in_kernel(*refs[:n_in], *refs[n_in + N_STACKED:])


def _in_call(x2, sc, sh, g, w, layer, gains, bfv, bd, tm, rows_per_mod, stack=None):
    m, d = x2.shape
    r = sc.shape[1]
    tiles_per_mod = rows_per_mod // tm
    mod_spec = pl.BlockSpec((1, r, d), lambda i: (i // tiles_per_mod, 0, 0))
    full = lambda a: pl.BlockSpec(a.shape, lambda i: (0,) * a.ndim)
    row = lambda n: pl.BlockSpec((tm, n), lambda i: (i, 0))
    out_defs = [(3 * GW, F32), (GW, F32), (128, F32), (GW, BF16), (GW, F32), (GW, F32),
                (GW, BF16), (GW, F32), (GW, F32), (N_HEADS, F32), (GW, BF16), (GW, F32), (GW, F32)]
    out_specs = [row(n) for n, _ in out_defs]
    out_shape = [jax.ShapeDtypeStruct((m, n), dt) for n, dt in out_defs]
    wspec = pl.BlockSpec((None,) + w.shape[1:], lambda i: (layer, 0, 0))
    in_specs = [row(d), mod_spec, mod_spec, full(g), wspec, full(gains), full(bfv), full(bd)]
    args = [x2, sc, sh, g, w, gains, bfv, bd]
    body, aliases = _in_kernel, {}
    if stack is not None:
        depth, layer, t, bufs = stack
        tps = t // tm
        assert tm == min(BAND_ROWS, t)
        tspec = pl.BlockSpec((None, None, GW, tm), lambda i: (layer, i // tps, 0, i % tps))
        bspec = pl.BlockSpec((None, None, GW, tm), lambda i: (layer, i // tps, 0, 0))
        if bufs is not None:
            body = _in_kernel_aliased
            in_specs += [pl.BlockSpec(memory_space=pl.ANY)] * N_STACKED
            aliases = {len(args) + n: len(out_specs) + n for n in range(N_STACKED)}
            args += list(bufs)
        out_specs += [tspec] * 4 + [bspec] * 2
        out_shape += ([jax.ShapeDtypeStruct((depth, m // t, GW, t), F32)] * 4
                      + [jax.ShapeDtypeStruct((depth, m // t, GW, tm), F32)] * 2)
    return pl.pallas_call(
        body,
        grid=(m // tm,),
        in_specs=in_specs,
        out_specs=out_specs,
        out_shape=out_shape,
        input_output_aliases=aliases,
        compiler_params=_params(("arbitrary",)),
        name="in_projection",
    )(*args)


def _cumsum_kernel(x_ref, u_ref, o_ref):
    r, tk = x_ref.shape
    carry = jnp.zeros((r, 1), F32)
    for blk in range(tk // 256):
        o = _dot_exact_rhs(x_ref[:, blk * 256:(blk + 1) * 256], u_ref[...], 3) + carry
        o_ref[:, blk * 256:(blk + 1) * 256] = o
        carry = o[:, 255:256]


def _cumsum_call(x):
    return pl.pallas_call(
        _cumsum_kernel,
        out_shape=jax.ShapeDtypeStruct(x.shape, F32),
        compiler_params=pltpu.CompilerParams(vmem_limit_bytes=VMEM_LIMIT),
        name="logf_cumsum",
    )(x, _const_tri_fwd(256))


def _fill_kv(k_ref, v_ref, kb_ref, vm_ref, hmask, tk):
    for kbi in range(k_ref.shape[1] // tk):
        rows = slice(kbi * tk, (kbi + 1) * tk)
        kb_ref[rows, :] = _bf(k_ref[0, rows, :])
        vv = v_ref[0, rows, :]
        for h in range(N_HEADS):
            vm_ref[kbi, h * tk:(h + 1) * tk, :] = _bf(jnp.where(hmask[h], vv, 0.0))


def _stack_heads(q, hmask):
    return jnp.concatenate([jnp.where(hmask[h], q, jnp.zeros_like(q)) for h in range(N_HEADS)], axis=0)


def _heads_to_lanes(w, tq):
    return jnp.concatenate([w[h * tq:(h + 1) * tq] for h in range(N_HEADS)], axis=1)


def _spread_heads(col, hmask, tq):
    out = col[(N_HEADS - 1) * tq:]
    for h in range(N_HEADS - 2, -1, -1):
        out = jnp.where(hmask[h], col[h * tq:(h + 1) * tq], out)
    return jnp.broadcast_to(out, (tq, GW))


def _stacked_positions(q0, kbi, tq, tk):
    rows = lax.broadcasted_iota(jnp.int32, (N_HEADS * tq, tk), 0) & (tq - 1)
    cols = lax.broadcasted_iota(jnp.int32, (N_HEADS * tq, tk), 1)
    return q0 + rows, kbi * tk + cols


def _sb_kernel(q_ref, k_ref, v_ref, u_ref, bd_ref, mg_ref, o_ref,
               kb_ref, vm_ref, acc_ref, car_ref, *, tq, tk, pos0):
    qi = pl.program_id(1)
    hmask = _head_masks()

    @pl.when(qi == 0)
    def _():
        _fill_kv(k_ref, v_ref, kb_ref, vm_ref, hmask, tk)

    nqs = _stack_heads(-q_ref[0], hmask)
    acc_ref[...] = jnp.zeros_like(acc_ref)
    car_ref[...] = jnp.zeros_like(car_ref)
    q0 = pos0 + qi * tq
    n_full = q0 // tk

    def scores(kbi):
        start = pl.multiple_of(kbi * tk, tk)
        return _dot_nt(nqs, kb_ref[pl.ds(start, tk), :])

    def log_rest(kbi, y, masked):
        neg_abs = pltpu.bitcast(pltpu.bitcast(y, jnp.uint32) | jnp.uint32(0x80000000), F32)
        r = jnp.minimum(y, 0.0) - jnp.log(1.0 + jnp.exp(neg_abs))
        valid = None
        if masked:
            rowp, colp = _stacked_positions(q0, kbi, tq, tk)
            valid = colp < rowp
            r = jnp.where(valid, r, 0.0)
        rc = jnp.dot(_bf(r), u_ref[...], preferred_element_type=F32)
        return rc, valid

    def weights(kbi, y, rc, valid):
        car = car_ref[...]
        w = jnp.exp(rc + jnp.concatenate([car] * (tk // 128), axis=1) - y)
        if valid is not None:
            w = jnp.where(valid, w, 0.0)
        car_ref[...] = car + jnp.broadcast_to(rc[:, 0:1], car.shape)
        acc_ref[...] += jnp.dot(_heads_to_lanes(_bf(w), tq), vm_ref[kbi], preferred_element_type=F32)

    def blocks(kbis, masked):
        ys = [scores(kbi) for kbi in kbis]
        pend = None
        for kbi, y, mk in zip(kbis, ys, masked):
            rc, valid = log_rest(kbi, y, mk)
            if pend is not None:
                weights(*pend)
            pend = (kbi, y, rc, valid)
        weights(*pend)

    @pl.when(n_full == 0)
    def _():
        blocks([n_full], [True])

    @pl.when(n_full > 0)
    def _():
        blocks([n_full, n_full - 1], [True, False])

    rest = jnp.maximum(n_full - 1, 0)

    def body(i, c):
        hi = rest - 1 - 2 * i
        blocks([hi, hi - 1], [False, False])
        return c

    lax.fori_loop(0, rest // 2, body, 0)

    @pl.when(rest % 2 == 1)
    def _():
        blocks([0], [False])

    o_ref[0] = _bf(_head_rms(acc_ref[...], bd_ref[...], mg_ref[...]))


def _fox_kernel(q_ref, k_ref, v_ref, fq_ref, fk_ref, bd_ref, mg_ref, o_ref,
                kb_ref, vm_ref, acc_ref, fqb_ref, *, tq, tk, pos0):
    qi = pl.program_id(1)
    hmask = _head_masks()

    @pl.when(qi == 0)
    def _():
        _fill_kv(k_ref, v_ref, kb_ref, vm_ref, hmask, tk)

    qs = _stack_heads(q_ref[0], hmask)
    fq = fq_ref[0]
    for h in range(N_HEADS):
        fqb_ref[h * tq:(h + 1) * tq, :] = jnp.broadcast_to(fq[:, h:h + 1], (tq, tk))
    acc_ref[...] = jnp.zeros_like(acc_ref)
    q0 = pos0 + qi * tq
    n_full = q0 // tk

    def scores(kbi):
        start = pl.multiple_of(kbi * tk, tk)
        return _dot_nt(qs, kb_ref[pl.ds(start, tk), :])

    def update(kbi, z, masked, ml):
        m_old, l_old = ml
        fk = fk_ref[0, kbi]
        s = jnp.concatenate([z[h * tq:(h + 1) * tq] + fqb_ref[h * tq:(h + 1) * tq, :] - fk[h:h + 1, :]
                             for h in range(N_HEADS)], axis=0)
        if masked:
            rowp, colp = _stacked_positions(q0, kbi, tq, tk)
            s = jnp.where(colp <= rowp, s, NEG)
        m_new = jnp.maximum(m_old, jnp.max(s, axis=1, keepdims=True))
        alpha = jnp.exp(m_old - m_new)
        p = jnp.exp(s - m_new)
        l_new = alpha * l_old + jnp.sum(p, axis=1, keepdims=True)
        pv = jnp.dot(_heads_to_lanes(_bf(p), tq), vm_ref[kbi], preferred_element_type=F32)
        acc_ref[...] = acc_ref[...] * _spread_heads(alpha, hmask, tq) + pv
        return m_new, l_new

    def pair(i, ml):
        za, zb = scores(2 * i), scores(2 * i + 1)
        return update(2 * i + 1, zb, False, update(2 * i, za, False, ml))

    ml = (jnp.full((N_HEADS * tq, 1), NEG, F32), jnp.zeros((N_HEADS * tq, 1), F32))
    rest = jnp.maximum(n_full - 1, 0)
    ml = lax.fori_loop(0, rest // 2, pair, ml)
    ml = lax.cond(rest % 2 == 1, lambda c: update(rest - 1, scores(rest - 1), False, c), lambda c: c, ml)

    def last_two(c):
        za, zb = scores(n_full - 1), scores(n_full)
        return update(n_full, zb, True, update(n_full - 1, za, False, c))

    _, l_fin = lax.cond(n_full > 0, last_two, lambda c: update(n_full, scores(n_full), True, c), ml)
    o = acc_ref[...] / _spread_heads(l_fin, hmask, tq)
    o_ref[0] = _bf(_head_rms(o, bd_ref[...], mg_ref[...]))


def _attn_specs(tk_all, tq):
    qspec = pl.BlockSpec((1, tq, GW), lambda bi, qi: (bi, qi, 0))
    kspec = pl.BlockSpec((1, tk_all, GW), lambda bi, qi: (bi, 0, 0))
    return qspec, kspec


def _sb_call(q, k, v, mg, bd, tq, tk, pos0):
    b, tq_all, _ = q.shape
    tk_all = k.shape[1]
    qspec, kspec = _attn_specs(tk_all, tq)
    full = lambda a: pl.BlockSpec(a.shape, lambda bi, qi: (0,) * a.ndim)
    u = _const_tri_rev(tk)
    return pl.pallas_call(
        functools.partial(_sb_kernel, tq=tq, tk=tk, pos0=pos0),
        grid=(b, tq_all // tq),
        in_specs=[qspec, kspec, kspec, full(u), full(bd), full(mg)],
        out_specs=qspec,
        out_shape=jax.ShapeDtypeStruct((b, tq_all, GW), BF16),
        scratch_shapes=[pltpu.VMEM((tk_all, GW), BF16), pltpu.VMEM((tk_all // tk, N_HEADS * tk, GW), BF16),
                        pltpu.VMEM((tq, GW), F32), pltpu.VMEM((N_HEADS * tq, 128), F32)],
        compiler_params=_params(("arbitrary", "arbitrary")),
        name="stick_breaking_attention",
    )(q, k, v, u, bd, mg)


def _fox_call(q, k, v, fq, fk, mg, bd, tq, tk, pos0):
    b, tq_all, _ = q.shape
    tk_all = k.shape[1]
    qspec, kspec = _attn_specs(tk_all, tq)
    full = lambda a: pl.BlockSpec(a.shape, lambda bi, qi: (0,) * a.ndim)
    return pl.pallas_call(
        functools.partial(_fox_kernel, tq=tq, tk=tk, pos0=pos0),
        grid=(b, tq_all // tq),
        in_specs=[qspec, kspec, kspec,
                  pl.BlockSpec((1, tq, N_HEADS), lambda bi, qi: (bi, qi, 0)),
                  pl.BlockSpec((1, tk_all // tk, N_HEADS, tk), lambda bi, qi: (bi, 0, 0, 0)),
                  full(bd), full(mg)],
        out_specs=qspec,
        out_shape=jax.ShapeDtypeStruct((b, tq_all, GW), BF16),
        scratch_shapes=[pltpu.VMEM((tk_all, GW), BF16), pltpu.VMEM((tk_all // tk, N_HEADS * tk, GW), BF16),
                        pltpu.VMEM((tq, GW), F32), pltpu.VMEM((N_HEADS * tq, tk), F32)],
        compiler_params=_params(("arbitrary", "arbitrary")),
        name="forgetting_attention",
    )(q, k, v, fq, fk, bd, mg)


def _band_kernel(q_ref, k_ref, v_ref, hk_ref, hv_ref, ext_ref, bd_ref, mg_ref, o_ref,
                 kp_ref, vp_ref, bias_ref, *, lq, nch, has_hist):
    qi = pl.program_id(1)
    hmask = _head_masks()
    t = k_ref.shape[1]
    hist = hk_ref.shape[1]
    ext_w = ext_ref.shape[1]

    @pl.when(qi == 0)
    def _():
        def put(r0, kk, vv):
            n = kk.shape[0]
            kp_ref[r0:r0 + n, :] = _bf(kk)
            vp_ref[r0:r0 + n, :] = _bf(vv)
        step = min(256, t)
        for r in range(0, hist, 256):
            put(r, hk_ref[0, r:r + 256, :], hv_ref[0, r:r + 256, :])
        for r in range(0, t, step):
            put(hist + r, k_ref[0, r:r + step, :], v_ref[0, r:r + step, :])
        zeros = jnp.zeros((128, GW), F32)
        put(hist + t, zeros, zeros)
        for h in range(N_HEADS):
            e = jnp.broadcast_to(ext_ref[h:h + 1, :], (lq, ext_w))
            e = pltpu.roll(e, ext_w - (lq - 1), 1, stride=1, stride_axis=0)
            colb = lax.broadcasted_iota(jnp.int32, (lq, BAND_WIN), 1)
            bias_ref[h * lq:(h + 1) * lq, :] = jnp.where(colb < hist + lq, e[:, :BAND_WIN], NEG)

    col = lax.broadcasted_iota(jnp.int32, (N_HEADS * lq, BAND_WIN), 1)
    chunks = range(nch)
    starts = [pl.multiple_of((qi * nch + ci) * lq, lq) for ci in chunks]
    s_all = [_dot_nt(_stack_heads(q_ref[0, ci * lq:(ci + 1) * lq, :], hmask),
                     kp_ref[pl.ds(starts[ci], BAND_WIN), :]) for ci in chunks]
    p_all = []
    for ci in chunks:
        s = s_all[ci] + bias_ref[...]
        if not has_hist:
            s = jnp.where(col >= hist - (qi * nch + ci) * lq, s, NEG)
        p = jnp.exp(s - jnp.max(s, axis=1, keepdims=True))
        p_all.append(_bf(p / jnp.sum(p, axis=1, keepdims=True)))
    pv_all = [jnp.dot(p_all[ci], vp_ref[pl.ds(starts[ci], BAND_WIN), :], preferred_element_type=F32)
              for ci in chunks]
    for ci in chunks:
        acc = jnp.zeros((lq, GW), F32)
        for h in range(N_HEADS):
            acc = acc + jnp.where(hmask[h], pv_all[ci][h * lq:(h + 1) * lq], 0.0)
        o_ref[0, ci * lq:(ci + 1) * lq, :] = _bf(_head_rms(acc, bd_ref[...], mg_ref[...]))


def _band_call(q, k, v, hk, hv, ext, mg, bd, lq, nch, has_hist):
    b, t, _ = q.shape
    hist = hk.shape[1]
    tqb = lq * nch
    qspec = pl.BlockSpec((1, tqb, GW), lambda bi, qi: (bi, qi, 0))
    kspec = pl.BlockSpec((1, t, GW), lambda bi, qi: (bi, 0, 0))
    hspec = pl.BlockSpec((1, hist, GW), lambda bi, qi: (bi, 0, 0))
    full = lambda a: pl.BlockSpec(a.shape, lambda bi, qi: (0,) * a.ndim)
    rows = hist + t + 128
    return pl.pallas_call(
        functools.partial(_band_kernel, lq=lq, nch=nch, has_hist=has_hist),
        grid=(b, t // tqb),
        in_specs=[qspec, kspec, kspec, hspec, hspec, full(ext), full(bd), full(mg)],
        out_specs=qspec,
        out_shape=jax.ShapeDtypeStruct((b, t, GW), BF16),
        scratch_shapes=[pltpu.VMEM((rows, GW), BF16), pltpu.VMEM((rows, GW), BF16),
                        pltpu.VMEM((N_HEADS * lq, BAND_WIN), F32)],
        compiler_params=_params(("arbitrary", "arbitrary")),
        name="band_attention",
    )(q, k, v, hk, hv, ext, bd, mg)


def _tile4(x):
    return jnp.concatenate([x] * N_HEADS, axis=0)


def _gdn1_kernel(qkv_ref, small_ref, st_ref, cw_ref, alog_ref, dtb_ref,
                 lblk_ref, eb_ref, eg_ref, bd_ref,
                 xu_ref, xwk_ref, qkd_ref, qg_ref, kdec_ref, egl_ref, conv_ref,
                 xbuf_ref, *, tg, t_valid, n_tiles):
    ti = pl.program_id(1)

    @pl.when(ti == 0)
    def _():
        xbuf_ref[0:8, :] = st_ref[0]

    xbuf_ref[8:8 + tg, :] = qkv_ref[0]
    y = cw_ref[3:4, :] * xbuf_ref[8:8 + tg, :]
    for i in range(3):
        y = y + cw_ref[i:i + 1, :] * xbuf_ref[5 + i:5 + i + tg, :]
    tv = t_valid - (n_tiles - 1) * tg

    @pl.when(ti == n_tiles - 1)
    def _():
        conv_ref[0] = xbuf_ref[8 + tv - 3:8 + tv, :]

    xbuf_ref[0:8, :] = xbuf_ref[tg:tg + 8, :]
    y = _silu(y)
    bd = bd_ref[...]
    q, k, v = y[:, :GW], y[:, GW:2 * GW], y[:, 2 * GW:]
    qn = q * lax.rsqrt(_dot_exact_rhs(q * q, bd, 2) + EPS) * (HEAD_DIM ** -0.5)
    kn = k * lax.rsqrt(_dot_exact_rhs(k * k, bd, 2) + EPS)

    sm = small_ref[0]
    rowv = ti * tg + lax.broadcasted_iota(jnp.int32, (tg, 1), 0) < t_valid
    beta_all = jnp.where(rowv, _sigmoid(sm), 0.0)
    g_all = jnp.where(rowv, -jnp.exp(alog_ref[...]) * _softplus(sm + dtb_ref[...]), 0.0)
    gc = _dot_exact_lhs(lblk_ref[...], g_all, 3)
    gl = jnp.concatenate([jnp.broadcast_to(gc[c * CHUNK + CHUNK - 1:(c + 1) * CHUNK, :], (CHUNK, 128))
                          for c in range(tg // CHUNK)], axis=0)
    bexp = _dot_exact_rhs(beta_all, eb_ref[...], 3)
    gx = _dot_exact_rhs(gc, eg_ref[...], 3)
    glx = _dot_exact_rhs(gl, eg_ref[...], 3)
    eg = jnp.exp(gx)
    kb = kn * bexp
    vb = v * bexp
    kbg = kb * eg
    qg_ref[0] = _bf(qn * eg)
    kdec_ref[0] = _bf(kn * jnp.exp(glx - gx))
    egl = jnp.exp(glx)

    ri = lax.broadcasted_iota(jnp.int32, (CHUNK, GW), 0)
    li = lax.broadcasted_iota(jnp.int32, (CHUNK, GW), 1) & (CHUNK - 1)
    eye = ri == li
    incl = ri >= li
    strict = ri > li
    rb = lax.broadcasted_iota(jnp.int32, (GW, GW), 0) >> 6
    bdm = rb == (lax.broadcasted_iota(jnp.int32, (GW, GW), 1) >> 6)
    bdm2 = jnp.concatenate([bdm, bdm], axis=1)

    def bd_weights(m):
        return [jnp.where(bdm, _tile4(part), jnp.zeros((), BF16)) for part in _split(m, 2)]

    def times_bd(a, m):
        ah, al = _split(a, 2)
        mh, ml = bd_weights(m)
        n = a.shape[0]
        top = jnp.dot(jnp.concatenate([ah, al], axis=0), mh, preferred_element_type=F32)
        return top[:n] + top[n:] + jnp.dot(ah, ml, preferred_element_type=F32)

    chunks = range(tg // CHUNK)
    sls = [slice(c * CHUNK, (c + 1) * CHUNK) for c in chunks]
    kq = [_dot_nt(jnp.concatenate([kb[sl], qn[sl]], axis=0), jnp.where(bdm, _tile4(kn[sl]), 0.0))
          for sl in sls]
    nmat, tmat = [], []
    for c in chunks:
        gxc = gx[sls[c]]
        grow = jnp.sum(jnp.where(eye, gxc, 0.0), axis=0, keepdims=True)
        dec = jnp.where(incl, jnp.exp(gxc - grow), 0.0)
        nmat.append(jnp.where(strict, kq[c][:CHUNK] * dec, 0.0))
        qkd_ref[0, sls[c], :] = _bf(kq[c][CHUNK:] * dec)
        tmat.append(jnp.where(eye, 1.0, 0.0) - nmat[c])
        egl_ref[0, c] = egl[c * CHUNK:c * CHUNK + 8, :]
    pmat = [times_bd(nmat[c], nmat[c]) for c in chunks]
    for step in range(5):
        if step < 4:
            tp = [times_bd(jnp.concatenate([tmat[c], pmat[c]], axis=0), pmat[c]) for c in chunks]
            tmat = [tmat[c] + tp[c][:CHUNK] for c in chunks]
            pmat = [tp[c][CHUNK:] for c in chunks]
        else:
            tmat = [tmat[c] + times_bd(tmat[c], pmat[c]) for c in chunks]
    xs = [_dot(tmat[c], jnp.where(bdm2, _tile4(jnp.concatenate([vb[sls[c]], kbg[sls[c]]], axis=1)), 0.0))
          for c in chunks]
    for c in chunks:
        xu_ref[0, sls[c], :] = xs[c][:, :GW]
        xwk_ref[0, sls[c], :] = _bf(xs[c][:, GW:])


def _gdn2_kernel(xu_ref, xwk_ref, qkd_ref, qg_ref, kdec_ref, egl_ref, z_ref, s0_ref, ng_ref, bd_ref,
                 o_ref, sfin_ref, s_ref, *, bb, nchunk):
    ti = pl.program_id(1)

    @pl.when(ti == 0)
    def _():
        s_ref[...] = s0_ref[...]

    rb = lax.broadcasted_iota(jnp.int32, (GW, GW), 0) >> 6
    bdm = rb == (lax.broadcasted_iota(jnp.int32, (GW, GW), 1) >> 6)
    bd = bd_ref[...]
    ng = ng_ref[...]

    def chunk(c, carry):
        r0 = pl.multiple_of(c * CHUNK, CHUNK)
        rows = pl.ds(r0, CHUNK)
        bs = range(bb)
        s_old = [s_ref[b] for b in bs]
        r = [jnp.dot(jnp.concatenate([xwk_ref[b, rows, :], qg_ref[b, rows, :]], axis=0), _bf(s_old[b]),
                     preferred_element_type=F32) for b in bs]
        vnew = [xu_ref[b, rows, :] - r[b][:CHUNK] for b in bs]
        o = [r[b][CHUNK:] + jnp.dot(qkd_ref[b, rows, :], _bf(jnp.where(bdm, _tile4(vnew[b]), 0.0)),
                                    preferred_element_type=F32) for b in bs]
        upd = [_dot(kdec_ref[b, rows, :].astype(F32).T, vnew[b]) for b in bs]
        for b in bs:
            s_ref[b] = egl_ref[b, c, 0:1, :] * s_old[b] + jnp.where(bdm, upd[b], 0.0)
            o_ref[b, rows, :] = _bf(_head_rms(o[b], bd, ng) * _silu(z_ref[b, rows, :]))
        return carry

    lax.fori_loop(0, nchunk, chunk, 0)
    sfin_ref[...] = s_ref[...]


def _gdn_consts(tg):
    i = np.arange(tg)
    same = i[:, None] // CHUNK == i[None, :] // CHUNK
    lblk = jnp.asarray(same & (i[:, None] >= i[None, :]), BF16)
    eb = np.zeros((128, GW), np.float32)
    eg = np.zeros((128, GW), np.float32)
    for h in range(N_HEADS):
        eb[h, h * HEAD_DIM:(h + 1) * HEAD_DIM] = 1
        eg[N_HEADS + h, h * HEAD_DIM:(h + 1) * HEAD_DIM] = 1
    return lblk, jnp.asarray(eb, BF16), jnp.asarray(eg, BF16)


def _gdn_call(qkv, small, z, st8, s0bd, cw, alog, dtb, ng, bd, tg, t_valid, bb):
    b, t, _ = qkv.shape
    n_tiles = t // tg
    lblk, eb, eg = _gdn_consts(tg)
    full = lambda a: pl.BlockSpec(a.shape, lambda bi, ti: (0,) * a.ndim)
    row = lambda n: pl.BlockSpec((1, tg, n), lambda bi, ti: (bi, ti, 0))
    nct = tg // CHUNK
    xu, xwk, qkd, qg, kdec, egl, conv_new = pl.pallas_call(
        functools.partial(_gdn1_kernel, tg=tg, t_valid=t_valid, n_tiles=n_tiles),
        grid=(b, n_tiles),
        in_specs=[row(3 * GW), row(128),
                  pl.BlockSpec((1, 8, 3 * GW), lambda bi, ti: (bi, 0, 0)),
                  full(cw), full(alog), full(dtb), full(lblk), full(eb), full(eg), full(bd)],
        out_specs=[row(GW), row(GW), row(GW), row(GW), row(GW),
                   pl.BlockSpec((1, nct, 8, GW), lambda bi, ti: (bi, ti, 0, 0)),
                   pl.BlockSpec((1, 3, 3 * GW), lambda bi, ti: (bi, 0, 0))],
        out_shape=[jax.ShapeDtypeStruct((b, t, GW), F32), jax.ShapeDtypeStruct((b, t, GW), BF16),
                   jax.ShapeDtypeStruct((b, t, GW), BF16), jax.ShapeDtypeStruct((b, t, GW), BF16),
                   jax.ShapeDtypeStruct((b, t, GW), BF16),
                   jax.ShapeDtypeStruct((b, t // CHUNK, 8, GW), F32),
                   jax.ShapeDtypeStruct((b, 3, 3 * GW), F32)],
        scratch_shapes=[pltpu.VMEM((8 + tg, 3 * GW), F32)],
        compiler_params=_params(("arbitrary", "arbitrary")),
        name="gdn_chunk_solve",
    )(qkv, small, st8, cw, alog, dtb, lblk, eb, eg, bd)

    full2 = lambda a: pl.BlockSpec(a.shape, lambda bi, ti: (0,) * a.ndim)
    rowb = pl.BlockSpec((bb, tg, GW), lambda bi, ti: (bi, ti, 0))
    sspec = pl.BlockSpec((bb, GW, GW), lambda bi, ti: (bi, 0, 0))
    o, sfin = pl.pallas_call(
        functools.partial(_gdn2_kernel, bb=bb, nchunk=nct),
        grid=(b // bb, n_tiles),
        in_specs=[rowb, rowb, rowb, rowb, rowb,
                  pl.BlockSpec((bb, nct, 8, GW), lambda bi, ti: (bi, ti, 0, 0)),
                  rowb, sspec, full2(ng), full2(bd)],
        out_specs=[rowb, sspec],
        out_shape=[jax.ShapeDtypeStruct((b, t, GW), BF16), jax.ShapeDtypeStruct((b, GW, GW), F32)],
        scratch_shapes=[pltpu.VMEM((bb, GW, GW), F32)],
        compiler_params=_params(("arbitrary", "arbitrary")),
        name="gdn_recurrence",
    )(xu, xwk, qkd, qg, kdec, egl, z, s0bd, ng, bd)
    return o, conv_new, sfin


def _out_kernel(x_ref, oa_ref, ob_ref, oc_ref, od_ref, ga1_ref, sc2_ref, sh2_ref, ga2_ref, g2_ref,
                wo_ref, wup_ref, cw_ref, wdn_ref, st_ref, y_ref, cnew_ref,
                ubuf_ref, car_ref, gs_ref, *, tm, tiles_per_seq, nff):
    i = pl.program_id(0)

    @pl.when(i % tiles_per_seq == 0)
    def _():
        car_ref[...] = st_ref[0]

    mix = jnp.dot(oa_ref[...], wo_ref[0:GW, :], preferred_element_type=F32)
    for n, ref in enumerate((ob_ref, oc_ref, od_ref)):
        mix = mix + jnp.dot(ref[...], wo_ref[(n + 1) * GW:(n + 2) * GW, :], preferred_element_type=F32)
    x1 = x_ref[...] + ga1_ref[0] * mix
    ms = jnp.mean(x1 * x1, axis=-1, keepdims=True)
    h = x1 * lax.rsqrt(ms + EPS) * g2_ref[...]
    h = _bf(h * (1.0 + sc2_ref[0]) + sh2_ref[0])
    dff = nff * GW
    for j in range(nff):
        halves = []
        for off in (0, dff):
            cols = slice(off + j * GW, off + (j + 1) * GW)
            u = jnp.dot(h, wup_ref[:, cols], preferred_element_type=F32)
            ubuf_ref[0:8, :] = car_ref[:, cols]
            ubuf_ref[8:8 + tm, :] = u
            halves.append(cw_ref[0:1, cols] * ubuf_ref[6:6 + tm, :] + cw_ref[1:2, cols] * ubuf_ref[7:7 + tm, :]
                          + cw_ref[2:3, cols] * u)
            car_ref[:, cols] = ubuf_ref[tm:tm + 8, :]
        gs_ref[:, j * GW:(j + 1) * GW] = _bf(_silu(halves[0]) * halves[1])
    y_ref[...] = x1 + ga2_ref[0] * jnp.dot(gs_ref[...], wdn_ref[...], preferred_element_type=F32)
    cnew_ref[0] = car_ref[...]


def _out_call(x2, oa, ob, oc, od, ga1, sc2, sh2, ga2, g2, wo, wup, cw, wdn, layer, st8, tm, rows_per_seq):
    m, d = x2.shape
    dff2 = wup.shape[2]
    nff = dff2 // (2 * GW)
    tiles_per_seq = rows_per_seq // tm
    nseq = m // rows_per_seq
    r = ga1.shape[1]
    rows_per_mod = m // ga1.shape[0]
    tiles_per_mod = rows_per_mod // tm
    mod_spec = pl.BlockSpec((1, r, d), lambda i: (i // tiles_per_mod, 0, 0))
    full = lambda a: pl.BlockSpec(a.shape, lambda i: (0,) * a.ndim)
    row = lambda n: pl.BlockSpec((tm, n), lambda i: (i, 0))
    stspec = pl.BlockSpec((1, 8, dff2), lambda i: (i // tiles_per_seq, 0, 0))
    lspec = lambda a: pl.BlockSpec((None,) + a.shape[1:], lambda i: (layer, 0, 0))
    return pl.pallas_call(
        functools.partial(_out_kernel, tm=tm, tiles_per_seq=tiles_per_seq, nff=nff),
        grid=(m // tm,),
        in_specs=[row(d), row(GW), row(GW), row(GW), row(GW), mod_spec, mod_spec, mod_spec, mod_spec,
                  full(g2), lspec(wo), lspec(wup), full(cw), lspec(wdn), stspec],
        out_specs=[row(d), stspec],
        out_shape=[jax.ShapeDtypeStruct((m, d), F32), jax.ShapeDtypeStruct((nseq, 8, dff2), F32)],
        scratch_shapes=[pltpu.VMEM((8 + tm, GW), F32), pltpu.VMEM((8, dff2), F32),
                        pltpu.VMEM((tm, dff2 // 2), BF16)],
        compiler_params=_params(("arbitrary",)),
        name="out_projection_mlp",
    )(x2, oa, ob, oc, od, ga1, sc2, sh2, ga2, g2, wo, wup, cw, wdn, st8)


def _pack_in_weights(w_in):
    a = 4 * GW + 2 * N_HEADS
    off_b = a
    off_c = off_b + 3 * GW
    off_d = off_c + 3 * GW + N_HEADS
    dd = w_in.shape[:2]
    zeros = lambda n: jnp.zeros(dd + (n,), w_in.dtype)
    small = jnp.concatenate([w_in[..., 4 * GW:a], zeros(128 - 2 * N_HEADS),
                             w_in[..., off_c + 3 * GW:off_d], zeros(128 - N_HEADS)], axis=-1)
    packed = jnp.concatenate([w_in[..., :4 * GW], w_in[..., off_b:off_b + 3 * GW],
                              w_in[..., off_c:off_c + 3 * GW], w_in[..., off_d:off_d + 3 * GW], small],
                             axis=-1)
    return _bf(packed)


def _lane_vec(v, offset):
    depth = v.shape[0]
    out = jnp.zeros((depth, 1, 128), F32)
    return out.at[:, 0, offset:offset + N_HEADS].set(v.astype(F32))


def _band_ext(rel_table, lq, hist):
    c = np.arange(BAND_EXT)
    rel = np.clip(hist + lq - 1 - c, -REL_CLIP, REL_CLIP) + REL_CLIP
    return rel_table.astype(F32)[:, rel]


def _layer(x, mods_in, mods_out, wts, cache, cfg, stack=None):
    b, t, d = x.shape
    m = b * t
    bd = wts["bd"]
    sc1, sh1 = mods_in
    outs = _in_call(x.reshape(m, d), sc1, sh1, wts["norm_mix_g"], wts["w_in"], wts["layer"], wts["qk_gains"],
                    wts["fox_b_f"], bd, cfg["tm_in"], cfg["rows_per_mod_in"], stack)
    (gqkv, gz, small, sbq, sbk, sbv, fq, fk, fv, logf, bq, bk, bv), kv_t = outs[:13], tuple(outs[13:])
    r3 = lambda a: a.reshape(b, t, a.shape[-1])
    new = {"fox_logf": r3(logf)}
    if stack is None:
        new.update({"sb_k": r3(sbk), "sb_v": r3(sbv), "fox_k": r3(fk), "fox_v": r3(fv)})
    past = cache is not None

    tp = cfg["t_gdn"]
    pad_t = lambda a: jnp.pad(r3(a), ((0, 0), (0, tp - t), (0, 0)))
    if past:
        st8 = jnp.pad(cache["gdn_conv"], ((0, 0), (5, 0), (0, 0)))
        s0 = cache["gdn_state"]
        eye = jnp.eye(N_HEADS, dtype=F32)
        s0bd = jnp.einsum("bhkv,hg->bhkgv", s0, eye).reshape(b, GW, GW)
    else:
        st8 = jnp.zeros((b, 8, 3 * GW), F32)
        s0bd = jnp.zeros((b, GW, GW), F32)
    oa, conv_new, sfin = _gdn_call(pad_t(gqkv), pad_t(small), pad_t(gz), st8, s0bd,
                                   wts["gdn_conv_w"], wts["gdn_a_log"], wts["gdn_dt_bias"],
                                   wts["gdn_norm_g"], bd, cfg["tg"], t, cfg["bb"])
    oa = oa[:, :t].reshape(m, GW)
    sfin = sfin.reshape(b, N_HEADS, HEAD_DIM, N_HEADS, HEAD_DIM)
    new["gdn_state"] = jnp.stack([sfin[:, h, :, h, :] for h in range(N_HEADS)], axis=1)
    new["gdn_conv"] = conv_new

    tq, tk, tkf = cfg["tq"], cfg["tk"], cfg["tk_fox"]
    if past:
        pos0 = cache["sb_k"].shape[1]
        tk_all = -(-(pos0 + t) // tk) * tk
        cat = lambda c, n: jnp.pad(jnp.concatenate([c.reshape(b, pos0, -1), r3(n)], axis=1),
                                   ((0, 0), (0, tk_all - pos0 - t), (0, 0)))
        ksb, vsb = cat(cache["sb_k"], sbk), cat(cache["sb_v"], sbv)
        kfx, vfx = cat(cache["fox_k"], fk), cat(cache["fox_v"], fv)
        lf_all = cat(cache["fox_logf"], logf)
    else:
        pos0, tk_all = 0, t
        ksb, vsb, kfx, vfx, lf_all = r3(sbk), r3(sbv), r3(fk), r3(fv), r3(logf)
    ob = _sb_call(r3(sbq), ksb, vsb, wts["merge_g"][0:1], bd, tq, tk, pos0)
    fcum = _cumsum_call(jnp.swapaxes(lf_all, 1, 2).reshape(b * N_HEADS, tk_all))
    fcum = fcum.reshape(b, N_HEADS, tk_all)
    f_q = jnp.swapaxes(fcum[:, :, pos0:pos0 + t], 1, 2)
    f_k = jnp.swapaxes(fcum.reshape(b, N_HEADS, tk_all // tkf, tkf), 1, 2)
    oc = _fox_call(r3(fq), kfx, vfx, f_q, f_k, wts["merge_g"][1:2], bd, tq, tkf, pos0)

    if past:
        hk, hv = cache["band_k"].reshape(b, -1, GW), cache["band_v"].reshape(b, -1, GW)
        keep = hk.shape[1]
        new["band_k"] = jnp.concatenate([hk, r3(bk)], axis=1)[:, -keep:]
        new["band_v"] = jnp.concatenate([hv, r3(bv)], axis=1)[:, -keep:]
    else:
        hk = hv = jnp.zeros((b, BAND_ROWS, GW), F32)
        if stack is None:
            keep = min(BAND_ROWS, t)
            new["band_k"], new["band_v"] = r3(bk)[:, -keep:], r3(bv)[:, -keep:]
    od = _band_call(r3(bq), r3(bk), r3(bv), hk, hv, wts["band_ext_" + cfg["name"]],
                    wts["merge_g"][2:3], bd, cfg["lq"], cfg["nch"], past)

    if past:
        st_ffn = jnp.pad(cache["ffn_conv"], ((0, 0), (6, 0), (0, 0)))
    else:
        st_ffn = jnp.zeros((b, 8, wts["w_up"].shape[2]), F32)
    ga1, sc2, sh2, ga2 = mods_out
    y, cnew = _out_call(x.reshape(m, d), oa, ob.reshape(m, GW), oc.reshape(m, GW), od.reshape(m, GW),
                        ga1, sc2, sh2, ga2, wts["norm_ffn_g"], wts["w_o"], wts["w_up"],
                        wts["ffn_conv_w"], wts["w_down"], wts["layer"], st_ffn, cfg["tm_out"], t)
    new["ffn_conv"] = cnew[:, 6:8]
    hd = lambda a: a.reshape(a.shape[0], a.shape[1], N_HEADS, HEAD_DIM)
    for n in ("sb_k", "sb_v", "fox_k", "fox_v", "band_k", "band_v"):
        if n in new:
            new[n] = hd(new[n])
    return y.reshape(b, t, d), new, kv_t


STATE_KEYS = ("gdn_conv", "gdn_state", "sb_k", "sb_v", "fox_k", "fox_v", "fox_logf",
              "band_k", "band_v", "ffn_conv")


def _gdn_batch(b):
    return next(n for n in (8, 4, 2, 1) if b % n == 0)


def _group_cfg(name, b, t):
    if t % 512 == 0:
        return dict(name=name, tm_in=512, rows_per_mod_in=t, tm_out=512, t_gdn=t, tg=512,
                    bb=_gdn_batch(b), tq=256, tk=256, tk_fox=256, lq=CHUNK, nch=4)
    assert t % 16 == 0 and t <= CHUNK
    return dict(name=name, tm_in=b * t, rows_per_mod_in=b * t, tm_out=t, t_gdn=CHUNK, tg=CHUNK,
                bb=_gdn_batch(b), tq=t, tk=256, tk_fox=256, lq=t, nch=1)


def kernel(x_prompt, x_sample, c_prompt, c_sample, state_gdn_conv, state_gdn, cache_sb_k, cache_sb_v, cache_fox_k, cache_fox_v, cache_fox_logf, cache_band_k, cache_band_v, state_ffn_conv, ada_w, ada_b, norm_mix_g, w_in, gdn_conv_w, gdn_a_log, gdn_dt_bias, gdn_norm_g, fox_q_g, fox_k_g, fox_b_f, band_q_g, band_k_g, band_rel_bias, merge_g, w_o, norm_ffn_g, w_up, ffn_conv_w, w_down):
    depth = ada_w.shape[0]
    bp, tp, d = x_prompt.shape
    bs, ts, _ = x_sample.shape
    cfg_p = _group_cfg("p", bp, tp)
    cfg_s = _group_cfg("s", bs, ts)
    hist_s = cache_band_k.shape[2]

    mod = _ada_call(jnp.concatenate([c_prompt, c_sample], axis=0), ada_w, ada_b)
    mod = mod.reshape(depth, bp + bs, 6, d)

    tile_h = lambda g: jnp.tile(g.astype(F32), (1, N_HEADS))[:, None, :]
    w_in_p = _pack_in_weights(w_in)
    w_o_b, w_up_b, w_dn_b = _bf(w_o), _bf(w_up), _bf(w_down)
    ffn_cw = jnp.pad(ffn_conv_w.astype(F32), ((0, 0), (0, 5), (0, 0)))
    gdn_cw = jnp.pad(gdn_conv_w.astype(F32), ((0, 0), (0, 4), (0, 0)))
    qk_gains = jnp.concatenate([tile_h(fox_q_g), tile_h(fox_k_g), tile_h(band_q_g), tile_h(band_k_g),
                                jnp.zeros((depth, 4, GW), F32)], axis=1)
    bd = _const_bd256()

    y_p, y_s = x_prompt, x_sample
    kv_p = None
    new_p = {n: [] for n in STATE_KEYS}
    new_s = {n: [] for n in STATE_KEYS}
    for l in range(depth):
        wts = {
            "bd": bd, "layer": l, "norm_mix_g": norm_mix_g[l][None].astype(F32), "w_in": w_in_p,
            "qk_gains": qk_gains[l], "fox_b_f": _lane_vec(fox_b_f, 0)[l],
            "gdn_conv_w": gdn_cw[l], "gdn_a_log": _lane_vec(gdn_a_log, N_HEADS)[l],
            "gdn_dt_bias": _lane_vec(gdn_dt_bias, N_HEADS)[l], "gdn_norm_g": tile_h(gdn_norm_g)[l],
            "merge_g": merge_g[l].reshape(3, GW).astype(F32),
            "band_ext_p": _band_ext(band_rel_bias[l], cfg_p["lq"], BAND_ROWS),
            "band_ext_s": _band_ext(band_rel_bias[l], cfg_s["lq"], hist_s),
            "w_o": w_o_b, "norm_ffn_g": norm_ffn_g[l][None].astype(F32), "w_up": w_up_b,
            "ffn_conv_w": ffn_cw[l], "w_down": w_dn_b,
        }
        mp = mod[l, :bp]
        ms = mod[l, bp:]
        pm = lambda i: mp[:, i][:, None, :]
        mods_in_p = (pm(1), pm(0))
        mods_out_p = (pm(2), pm(4), pm(3), pm(5))
        srow = lambda i: jnp.repeat(ms[:, i], ts, axis=0)[None]
        sm1 = lambda i: ms[:, i][:, None, :]
        mods_in_s = (srow(1), srow(0))
        mods_out_s = (sm1(2), sm1(4), sm1(3), sm1(5))
        cache = {"gdn_conv": state_gdn_conv[l], "gdn_state": state_gdn[l],
                 "sb_k": cache_sb_k[l], "sb_v": cache_sb_v[l], "fox_k": cache_fox_k[l],
                 "fox_v": cache_fox_v[l], "fox_logf": cache_fox_logf[l],
                 "band_k": cache_band_k[l], "band_v": cache_band_v[l], "ffn_conv": state_ffn_conv[l]}
        y_p, st_p, kv_p = _layer(y_p, mods_in_p, mods_out_p, wts, None, cfg_p, (depth, l, tp, kv_p or None))
        y_s, st_s, _ = _layer(y_s, mods_in_s, mods_out_s, wts, cache, cfg_s)
        for n in STATE_KEYS:
            if n in st_p:
                new_p[n].append(st_p[n])
            new_s[n].append(st_s[n])
    out_p = {n: jnp.stack(v) for n, v in new_p.items() if v}
    for n, a in zip(("sb_k", "sb_v", "fox_k", "fox_v", "band_k", "band_v"), kv_p):
        out_p[n] = jnp.transpose(a.reshape(depth, bp, N_HEADS, HEAD_DIM, a.shape[-1]), (0, 1, 4, 2, 3))
    return ((y_p, y_s) + tuple(out_p[n] for n in STATE_KEYS)
            + tuple(jnp.stack(new_s[n]) for n in STATE_KEYS))
```

```python
import functools

import numpy as np
import jax
import jax.numpy as jnp
from jax import lax
from jax.experimental import pallas as pl
from jax.experimental.pallas import tpu as pltpu

F32 = jnp.float32
BF16 = jnp.bfloat16

HEAD_DIM = 64
N_HEADS = 4
GW = N_HEADS * HEAD_DIM
CHUNK = 64
BAND_ROWS = 512
REL_CLIP = 2 * CHUNK
EPS = 1e-6
NEG = -1e30
VMEM_LIMIT = 56 * 1024 * 1024
BAND_WIN = BAND_ROWS + 128
BAND_EXT = BAND_WIN + 128


def _bf(x):
    return x.astype(BF16)


def _dot(a, b):
    return jnp.dot(_bf(a), _bf(b), preferred_element_type=F32)


def _dot_nt(a, b):
    return lax.dot_general(_bf(a), _bf(b), (((1,), (1,)), ((), ())), preferred_element_type=F32)


def _split(x, n):
    parts = []
    for _ in range(n - 1):
        hi = _bf(x)
        parts.append(hi)
        x = x - hi.astype(F32)
    parts.append(_bf(x))
    return parts


def _dot_exact_rhs(x, m, n):
    return sum(jnp.dot(p, m, preferred_element_type=F32) for p in _split(x, n))


def _dot_exact_lhs(m, x, n):
    return sum(jnp.dot(m, p, preferred_element_type=F32) for p in _split(x, n))


def _sigmoid(x):
    return 1.0 / (1.0 + jnp.exp(-x))


def _silu(x):
    return x * _sigmoid(x)


def _softplus(x):
    return jnp.maximum(x, 0.0) + jnp.log(1.0 + jnp.exp(-jnp.abs(x)))


def _head_rms(o, bd, gain, passes=1):
    ss = _dot_exact_rhs(o * o, bd, passes)
    return o * lax.rsqrt(ss * (1.0 / HEAD_DIM) + EPS) * gain


def _head_masks(width=GW):
    lane = lax.broadcasted_iota(jnp.int32, (1, width), 1)
    return [((lane & (GW - 1)) >> 6) == h for h in range(N_HEADS)]


def _mod_spec(mod, tiles_per_mod, d):
    arr, first, stride = mod
    return pl.BlockSpec((1, arr.shape[1], d), lambda i: (first + (i // tiles_per_mod) * stride, 0, 0))


def _params(sem, **kw):
    return pltpu.CompilerParams(dimension_semantics=sem, vmem_limit_bytes=VMEM_LIMIT, **kw)


def _np_bd(n, blk):
    i = np.arange(n)
    return (i[:, None] // blk == i[None, :] // blk)


def _const_bd256():
    return jnp.asarray(_np_bd(GW, HEAD_DIM), BF16)


def _const_tri_fwd(n):
    i = np.arange(n)
    return jnp.asarray(i[:, None] <= i[None, :], BF16)


def _const_tri_rev(n):
    i = np.arange(n)
    return jnp.asarray(i[:, None] >= i[None, :], BF16)


def _ada_kernel(c_ref, w_ref, b_ref, o_ref):
    s = _silu(c_ref[...])
    o_ref[0] = _dot(s, w_ref[0]) + b_ref[0]


def _ada_call(c_all, ada_w, ada_b):
    depth, d, n = ada_w.shape
    rows = c_all.shape[0]
    tn = 1536
    return pl.pallas_call(
        _ada_kernel,
        grid=(depth, n // tn),
        in_specs=[pl.BlockSpec((rows, d), lambda l, j: (0, 0)),
                  pl.BlockSpec((1, d, tn), lambda l, j: (l, 0, j)),
                  pl.BlockSpec((1, 1, tn), lambda l, j: (l, 0, j))],
        out_specs=pl.BlockSpec((1, rows, tn), lambda l, j: (l, 0, j)),
        out_shape=jax.ShapeDtypeStruct((depth, rows, n), F32),
        compiler_params=_params(("arbitrary", "arbitrary")),
        name="ada_modulation",
    )(c_all, ada_w, ada_b.reshape(depth, 1, n))


IN_GROUPS = 14
IN_COLS_PACKED = IN_GROUPS * GW


def _in_kernel(x_ref, sc_ref, sh_ref, g_ref, w_ref, gains_ref, bf_ref, bd_ref,
               gqkv_ref, gz_ref, small_ref, sbq_ref, sbk_ref, sbv_ref,
               fq_ref, fk_ref, fv_ref, logf_ref, bq_ref, bk_ref, bv_ref, *t_refs):
    x = x_ref[...]
    ms = jnp.mean(x * x, axis=-1, keepdims=True)
    h = x * lax.rsqrt(ms + EPS) * g_ref[...]
    h = _bf(h * (1.0 + sc_ref[0]) + sh_ref[0])
    bd = bd_ref[...]
    scale = HEAD_DIM ** -0.5

    def col(j, n=1):
        return jnp.dot(h, w_ref[:, j * GW:(j + n) * GW], preferred_element_type=F32)

    gqkv_ref[...] = col(0, 3)
    gz_ref[...] = col(3)
    sbq_ref[...] = _bf(col(4) * scale)
    def keep(ref, n, val):
        ref[...] = val
        if t_refs:
            t_refs[n][...] = val.T

    keep(sbk_ref, 0, col(5))
    keep(sbv_ref, 1, col(6))
    fq_ref[...] = _bf(_head_rms(col(7), bd, gains_ref[0:1, :], 2) * scale)
    keep(fk_ref, 2, _head_rms(col(8), bd, gains_ref[1:2, :], 2))
    keep(fv_ref, 3, col(9))
    bq_ref[...] = _bf(_head_rms(col(10), bd, gains_ref[2:3, :], 2) * scale)
    bk = _head_rms(col(11), bd, gains_ref[3:4, :], 2)
    bv = col(12)
    bk_ref[...] = bk
    bv_ref[...] = bv
    if t_refs:
        t_refs[4][...] = bk.T
        t_refs[5][...] = bv.T
    sm = col(13)
    small_ref[...] = sm[:, :128]
    lf = sm[:, 128:] + bf_ref[...]
    logf = jnp.minimum(lf, 0.0) - jnp.log(1.0 + jnp.exp(-jnp.abs(lf)))
    logf_ref[...] = logf[:, :N_HEADS]
    if t_refs:
        t_refs[6][...] = logf.T[:N_HEADS]


N_STACKED = 7


def _in_kernel_aliased(*refs):
    n_in = 8
    _in_kernel(*refs[:n_in], *refs[n_in + N_STACKED:])


def _in_call(x2, sc, sh, g, w, layer, gains, bfv, bd, tm, rows_per_mod, stack=None):
    m, d = x2.shape
    tiles_per_mod = rows_per_mod // tm
    full = lambda a: pl.BlockSpec(a.shape, lambda i: (0,) * a.ndim)
    row = lambda n: pl.BlockSpec((tm, n), lambda i: (i, 0))
    out_defs = [(3 * GW, F32), (GW, F32), (128, F32), (GW, BF16), (GW, F32), (GW, F32),
                (GW, BF16), (GW, F32), (GW, F32), (N_HEADS, F32), (GW, BF16), (GW, F32), (GW, F32)]
    out_specs = [row(n) for n, _ in out_defs]
    out_shape = [jax.ShapeDtypeStruct((m, n), dt) for n, dt in out_defs]
    wspec = pl.BlockSpec((None,) + w.shape[1:], lambda i: (layer, 0, 0))
    in_specs = [row(d), _mod_spec(sc, tiles_per_mod, d), _mod_spec(sh, tiles_per_mod, d), full(g), wspec,
                full(gains), full(bfv), full(bd)]
    args = [x2, sc[0], sh[0], g, w, gains, bfv, bd]
    body, aliases = _in_kernel, {}
    if stack is not None:
        depth, layer, t, bufs = stack
        tps = t // tm
        assert tm == min(BAND_ROWS, t)
        tspec = pl.BlockSpec((None, None, GW, tm), lambda i: (layer, i // tps, 0, i % tps))
        bspec = pl.BlockSpec((None, None, GW, tm), lambda i: (layer, i // tps, 0, 0))
        if bufs is not None:
            body = _in_kernel_aliased
            in_specs += [pl.BlockSpec(memory_space=pl.ANY)] * N_STACKED
            aliases = {len(args) + n: len(out_specs) + n for n in range(N_STACKED)}
            args += list(bufs)
        fspec = pl.BlockSpec((None, None, N_HEADS, tm), lambda i: (layer, i // tps, 0, i % tps))
        out_specs += [tspec] * 4 + [bspec] * 2 + [fspec]
        out_shape += ([jax.ShapeDtypeStruct((depth, m // t, GW, t), F32)] * 4
                      + [jax.ShapeDtypeStruct((depth, m // t, GW, tm), F32)] * 2
                      + [jax.ShapeDtypeStruct((depth, m // t, N_HEADS, t), F32)])
    return pl.pallas_call(
        body,
        grid=(m // tm,),
        in_specs=in_specs,
        out_specs=out_specs,
        out_shape=out_shape,
        input_output_aliases=aliases,
        compiler_params=_params(("arbitrary",)),
        name="in_projection",
    )(*args)


def _cumsum_kernel(x_ref, u_ref, o_ref):
    r, tk = x_ref.shape
    carry = jnp.zeros((r, 1), F32)
    for blk in range(tk // 256):
        o = _dot_exact_rhs(x_ref[:, blk * 256:(blk + 1) * 256], u_ref[...], 3) + carry
        o_ref[:, blk * 256:(blk + 1) * 256] = o
        carry = o[:, 255:256]


def _cumsum_call(x):
    return pl.pallas_call(
        _cumsum_kernel,
        out_shape=jax.ShapeDtypeStruct(x.shape, F32),
        compiler_params=pltpu.CompilerParams(vmem_limit_bytes=VMEM_LIMIT),
        name="logf_cumsum",
    )(x, _const_tri_fwd(256))


def _fill_kv(k_ref, v_ref, kb_ref, vm_ref, hmask, tk):
    for kbi in range(k_ref.shape[1] // tk):
        rows = slice(kbi * tk, (kbi + 1) * tk)
        kb_ref[rows, :] = _bf(k_ref[0, rows, :])
        vv = v_ref[0, rows, :]
        for h in range(N_HEADS):
            vm_ref[kbi, h * tk:(h + 1) * tk, :] = _bf(jnp.where(hmask[h], vv, 0.0))


def _stack_heads(q, hmask):
    return jnp.concatenate([jnp.where(hmask[h], q, jnp.zeros_like(q)) for h in range(N_HEADS)], axis=0)


def _heads_to_lanes(w, tq):
    return jnp.concatenate([w[h * tq:(h + 1) * tq] for h in range(N_HEADS)], axis=1)


def _spread_heads(col, hmask, tq):
    out = col[(N_HEADS - 1) * tq:]
    for h in range(N_HEADS - 2, -1, -1):
        out = jnp.where(hmask[h], col[h * tq:(h + 1) * tq], out)
    return jnp.broadcast_to(out, (tq, GW))


def _stacked_positions(q0, kbi, tq, tk):
    rows = lax.broadcasted_iota(jnp.int32, (N_HEADS * tq, tk), 0) & (tq - 1)
    cols = lax.broadcasted_iota(jnp.int32, (N_HEADS * tq, tk), 1)
    return q0 + rows, kbi * tk + cols


def _sb_kernel(q_ref, k_ref, v_ref, u_ref, bd_ref, mg_ref, o_ref,
               kb_ref, vm_ref, acc_ref, car_ref, *, tq, tk, pos0):
    qi = pl.program_id(1)
    hmask = _head_masks()

    @pl.when(qi == 0)
    def _():
        _fill_kv(k_ref, v_ref, kb_ref, vm_ref, hmask, tk)

    nqs = _stack_heads(-q_ref[0], hmask)
    acc_ref[...] = jnp.zeros_like(acc_ref)
    car_ref[...] = jnp.zeros_like(car_ref)
    q0 = pos0 + qi * tq
    n_full = q0 // tk

    def scores(kbi):
        start = pl.multiple_of(kbi * tk, tk)
        return _dot_nt(nqs, kb_ref[pl.ds(start, tk), :])

    def log_rest(kbi, y, masked):
        neg_abs = pltpu.bitcast(pltpu.bitcast(y, jnp.uint32) | jnp.uint32(0x80000000), F32)
        r = jnp.minimum(y, 0.0) - jnp.log(1.0 + jnp.exp(neg_abs))
        valid = None
        if masked:
            rowp, colp = _stacked_positions(q0, kbi, tq, tk)
            valid = colp < rowp
            r = jnp.where(valid, r, 0.0)
        rc = jnp.dot(_bf(r), u_ref[...], preferred_element_type=F32)
        return rc, valid

    def weights(kbi, y, rc, valid):
        car = car_ref[...]
        w = jnp.exp(rc + jnp.concatenate([car] * (tk // 128), axis=1) - y)
        if valid is not None:
            w = jnp.where(valid, w, 0.0)
        car_ref[...] = car + jnp.broadcast_to(rc[:, 0:1], car.shape)
        acc_ref[...] += jnp.dot(_heads_to_lanes(_bf(w), tq), vm_ref[kbi], preferred_element_type=F32)

    def blocks(kbis, masked):
        ys = [scores(kbi) for kbi in kbis]
        pend = None
        for kbi, y, mk in zip(kbis, ys, masked):
            rc, valid = log_rest(kbi, y, mk)
            if pend is not None:
                weights(*pend)
            pend = (kbi, y, rc, valid)
        weights(*pend)

    @pl.when(n_full == 0)
    def _():
        blocks([n_full], [True])

    @pl.when(n_full > 0)
    def _():
        blocks([n_full, n_full - 1], [True, False])

    rest = jnp.maximum(n_full - 1, 0)

    def body(i, c):
        hi = rest - 1 - 2 * i
        blocks([hi, hi - 1], [False, False])
        return c

    lax.fori_loop(0, rest // 2, body, 0)

    @pl.when(rest % 2 == 1)
    def _():
        blocks([0], [False])

    o_ref[0] = _bf(_head_rms(acc_ref[...], bd_ref[...], mg_ref[...]))


def _fox_kernel(q_ref, k_ref, v_ref, fq_ref, fk_ref, bd_ref, mg_ref, o_ref,
                kb_ref, vm_ref, acc_ref, fqb_ref, *, tq, tk, pos0):
    qi = pl.program_id(1)
    hmask = _head_masks()

    @pl.when(qi == 0)
    def _():
        _fill_kv(k_ref, v_ref, kb_ref, vm_ref, hmask, tk)

    qs = _stack_heads(q_ref[0], hmask)
    fq = fq_ref[0]
    for h in range(N_HEADS):
        fqb_ref[h * tq:(h + 1) * tq, :] = jnp.broadcast_to(fq[:, h:h + 1], (tq, tk))
    acc_ref[...] = jnp.zeros_like(acc_ref)
    q0 = pos0 + qi * tq
    n_full = q0 // tk

    def scores(kbi):
        start = pl.multiple_of(kbi * tk, tk)
        return _dot_nt(qs, kb_ref[pl.ds(start, tk), :])

    def update(kbi, z, masked, ml):
        m_old, l_old = ml
        fk = fk_ref[0, kbi]
        s = jnp.concatenate([z[h * tq:(h + 1) * tq] + fqb_ref[h * tq:(h + 1) * tq, :] - fk[h:h + 1, :]
                             for h in range(N_HEADS)], axis=0)
        if masked:
            rowp, colp = _stacked_positions(q0, kbi, tq, tk)
            s = jnp.where(colp <= rowp, s, NEG)
        m_new = jnp.maximum(m_old, jnp.max(s, axis=1, keepdims=True))
        alpha = jnp.exp(m_old - m_new)
        p = jnp.exp(s - m_new)
        l_new = alpha * l_old + jnp.sum(p, axis=1, keepdims=True)
        pv = jnp.dot(_heads_to_lanes(_bf(p), tq), vm_ref[kbi], preferred_element_type=F32)
        acc_ref[...] = acc_ref[...] * _spread_heads(alpha, hmask, tq) + pv
        return m_new, l_new

    def pair(i, ml):
        za, zb = scores(2 * i), scores(2 * i + 1)
        return update(2 * i + 1, zb, False, update(2 * i, za, False, ml))

    ml = (jnp.full((N_HEADS * tq, 1), NEG, F32), jnp.zeros((N_HEADS * tq, 1), F32))
    rest = jnp.maximum(n_full - 1, 0)
    ml = lax.fori_loop(0, rest // 2, pair, ml)
    ml = lax.cond(rest % 2 == 1, lambda c: update(rest - 1, scores(rest - 1), False, c), lambda c: c, ml)

    def last_two(c):
        za, zb = scores(n_full - 1), scores(n_full)
        return update(n_full, zb, True, update(n_full - 1, za, False, c))

    _, l_fin = lax.cond(n_full > 0, last_two, lambda c: update(n_full, scores(n_full), True, c), ml)
    o = acc_ref[...] / _spread_heads(l_fin, hmask, tq)
    o_ref[0] = _bf(_head_rms(o, bd_ref[...], mg_ref[...]))


def _attn_specs(tk_all, tq):
    qspec = pl.BlockSpec((1, tq, GW), lambda bi, qi: (bi, qi, 0))
    kspec = pl.BlockSpec((1, tk_all, GW), lambda bi, qi: (bi, 0, 0))
    return qspec, kspec


def _sb_call(q, k, v, mg, bd, tq, tk, pos0):
    b, tq_all, _ = q.shape
    tk_all = k.shape[1]
    qspec, kspec = _attn_specs(tk_all, tq)
    full = lambda a: pl.BlockSpec(a.shape, lambda bi, qi: (0,) * a.ndim)
    u = _const_tri_rev(tk)
    return pl.pallas_call(
        functools.partial(_sb_kernel, tq=tq, tk=tk, pos0=pos0),
        grid=(b, tq_all // tq),
        in_specs=[qspec, kspec, kspec, full(u), full(bd), full(mg)],
        out_specs=qspec,
        out_shape=jax.ShapeDtypeStruct((b, tq_all, GW), BF16),
        scratch_shapes=[pltpu.VMEM((tk_all, GW), BF16), pltpu.VMEM((tk_all // tk, N_HEADS * tk, GW), BF16),
                        pltpu.VMEM((tq, GW), F32), pltpu.VMEM((N_HEADS * tq, 128), F32)],
        compiler_params=_params(("arbitrary", "arbitrary")),
        name="stick_breaking_attention",
    )(q, k, v, u, bd, mg)


def _fox_call(q, k, v, fq, fk, mg, bd, tq, tk, pos0):
    b, tq_all, _ = q.shape
    tk_all = k.shape[1]
    qspec, kspec = _attn_specs(tk_all, tq)
    full = lambda a: pl.BlockSpec(a.shape, lambda bi, qi: (0,) * a.ndim)
    return pl.pallas_call(
        functools.partial(_fox_kernel, tq=tq, tk=tk, pos0=pos0),
        grid=(b, tq_all // tq),
        in_specs=[qspec, kspec, kspec,
                  pl.BlockSpec((1, tq, N_HEADS), lambda bi, qi: (bi, qi, 0)),
                  pl.BlockSpec((1, tk_all // tk, N_HEADS, tk), lambda bi, qi: (bi, 0, 0, 0)),
                  full(bd), full(mg)],
        out_specs=qspec,
        out_shape=jax.ShapeDtypeStruct((b, tq_all, GW), BF16),
        scratch_shapes=[pltpu.VMEM((tk_all, GW), BF16), pltpu.VMEM((tk_all // tk, N_HEADS * tk, GW), BF16),
                        pltpu.VMEM((tq, GW), F32), pltpu.VMEM((N_HEADS * tq, tk), F32)],
        compiler_params=_params(("arbitrary", "arbitrary")),
        name="forgetting_attention",
    )(q, k, v, fq, fk, bd, mg)


def _band_kernel(q_ref, k_ref, v_ref, hk_ref, hv_ref, ext_ref, bd_ref, mg_ref, o_ref,
                 kp_ref, vp_ref, bias_ref, *, lq, nch, has_hist):
    qi = pl.program_id(1)
    hmask = _head_masks()
    t = k_ref.shape[1]
    hist = hk_ref.shape[1]
    ext_w = ext_ref.shape[1]

    @pl.when(qi == 0)
    def _():
        def put(r0, kk, vv):
            n = kk.shape[0]
            kp_ref[r0:r0 + n, :] = _bf(kk)
            vp_ref[r0:r0 + n, :] = _bf(vv)
        step = min(256, t)
        for r in range(0, hist, 256):
            put(r, hk_ref[0, r:r + 256, :], hv_ref[0, r:r + 256, :])
        for r in range(0, t, step):
            put(hist + r, k_ref[0, r:r + step, :], v_ref[0, r:r + step, :])
        zeros = jnp.zeros((128, GW), F32)
        put(hist + t, zeros, zeros)
        for h in range(N_HEADS):
            e = jnp.broadcast_to(ext_ref[h:h + 1, :], (lq, ext_w))
            e = pltpu.roll(e, ext_w - (lq - 1), 1, stride=1, stride_axis=0)
            colb = lax.broadcasted_iota(jnp.int32, (lq, BAND_WIN), 1)
            bias_ref[h * lq:(h + 1) * lq, :] = jnp.where(colb < hist + lq, e[:, :BAND_WIN], NEG)

    col = lax.broadcasted_iota(jnp.int32, (N_HEADS * lq, BAND_WIN), 1)
    chunks = range(nch)
    starts = [pl.multiple_of((qi * nch + ci) * lq, lq) for ci in chunks]
    s_all = [_dot_nt(_stack_heads(q_ref[0, ci * lq:(ci + 1) * lq, :], hmask),
                     kp_ref[pl.ds(starts[ci], BAND_WIN), :]) for ci in chunks]
    p_all = []
    for ci in chunks:
        s = s_all[ci] + bias_ref[...]
        if not has_hist:
            s = jnp.where(col >= hist - (qi * nch + ci) * lq, s, NEG)
        p = jnp.exp(s - jnp.max(s, axis=1, keepdims=True))
        p_all.append(_bf(p / jnp.sum(p, axis=1, keepdims=True)))
    pv_all = [jnp.dot(p_all[ci], vp_ref[pl.ds(starts[ci], BAND_WIN), :], preferred_element_type=F32)
              for ci in chunks]
    for ci in chunks:
        acc = jnp.zeros((lq, GW), F32)
        for h in range(N_HEADS):
            acc = acc + jnp.where(hmask[h], pv_all[ci][h * lq:(h + 1) * lq], 0.0)
        o_ref[0, ci * lq:(ci + 1) * lq, :] = _bf(_head_rms(acc, bd_ref[...], mg_ref[...]))


def _band_call(q, k, v, hk, hv, ext, mg, bd, lq, nch, has_hist):
    b, t, _ = q.shape
    hist = hk.shape[1]
    tqb = lq * nch
    qspec = pl.BlockSpec((1, tqb, GW), lambda bi, qi: (bi, qi, 0))
    kspec = pl.BlockSpec((1, t, GW), lambda bi, qi: (bi, 0, 0))
    hspec = pl.BlockSpec((1, hist, GW), lambda bi, qi: (bi, 0, 0))
    full = lambda a: pl.BlockSpec(a.shape, lambda bi, qi: (0,) * a.ndim)
    rows = hist + t + 128
    return pl.pallas_call(
        functools.partial(_band_kernel, lq=lq, nch=nch, has_hist=has_hist),
        grid=(b, t // tqb),
        in_specs=[qspec, kspec, kspec, hspec, hspec, full(ext), full(bd), full(mg)],
        out_specs=qspec,
        out_shape=jax.ShapeDtypeStruct((b, t, GW), BF16),
        scratch_shapes=[pltpu.VMEM((rows, GW), BF16), pltpu.VMEM((rows, GW), BF16),
                        pltpu.VMEM((N_HEADS * lq, BAND_WIN), F32)],
        compiler_params=_params(("arbitrary", "arbitrary")),
        name="band_attention",
    )(q, k, v, hk, hv, ext, bd, mg)


def _tile4(x):
    return jnp.concatenate([x] * N_HEADS, axis=0)


def _gdn1_kernel(qkv_ref, small_ref, st_ref, cw_ref, alog_ref, dtb_ref,
                 lblk_ref, eb_ref, eg_ref, bd_ref,
                 xu_ref, xwk_ref, qkd_ref, qg_ref, kdec_ref, egl_ref, conv_ref,
                 xbuf_ref, *, tg, t_valid, n_tiles):
    ti = pl.program_id(1)

    @pl.when(ti == 0)
    def _():
        xbuf_ref[0:8, :] = st_ref[0]

    xbuf_ref[8:8 + tg, :] = qkv_ref[0]
    y = cw_ref[3:4, :] * xbuf_ref[8:8 + tg, :]
    for i in range(3):
        y = y + cw_ref[i:i + 1, :] * xbuf_ref[5 + i:5 + i + tg, :]
    tv = t_valid - (n_tiles - 1) * tg

    @pl.when(ti == n_tiles - 1)
    def _():
        conv_ref[0] = xbuf_ref[8 + tv - 3:8 + tv, :]

    xbuf_ref[0:8, :] = xbuf_ref[tg:tg + 8, :]
    y = _silu(y)
    bd = bd_ref[...]
    q, k, v = y[:, :GW], y[:, GW:2 * GW], y[:, 2 * GW:]
    qn = q * lax.rsqrt(_dot_exact_rhs(q * q, bd, 2) + EPS) * (HEAD_DIM ** -0.5)
    kn = k * lax.rsqrt(_dot_exact_rhs(k * k, bd, 2) + EPS)

    sm = small_ref[0]
    rowv = ti * tg + lax.broadcasted_iota(jnp.int32, (tg, 1), 0) < t_valid
    beta_all = jnp.where(rowv, _sigmoid(sm), 0.0)
    g_all = jnp.where(rowv, -jnp.exp(alog_ref[...]) * _softplus(sm + dtb_ref[...]), 0.0)
    gc = _dot_exact_lhs(lblk_ref[...], g_all, 3)
    gl = jnp.concatenate([jnp.broadcast_to(gc[c * CHUNK + CHUNK - 1:(c + 1) * CHUNK, :], (CHUNK, 128))
                          for c in range(tg // CHUNK)], axis=0)
    bexp = _dot_exact_rhs(beta_all, eb_ref[...], 3)
    gx = _dot_exact_rhs(gc, eg_ref[...], 3)
    glx = _dot_exact_rhs(gl, eg_ref[...], 3)
    eg = jnp.exp(gx)
    kb = kn * bexp
    vb = v * bexp
    kbg = kb * eg
    qg_ref[0] = _bf(qn * eg)
    kdec_ref[0] = _bf(kn * jnp.exp(glx - gx))
    egl = jnp.exp(glx)

    ri = lax.broadcasted_iota(jnp.int32, (CHUNK, GW), 0)
    li = lax.broadcasted_iota(jnp.int32, (CHUNK, GW), 1) & (CHUNK - 1)
    eye = ri == li
    incl = ri >= li
    strict = ri > li
    rb = lax.broadcasted_iota(jnp.int32, (GW, GW), 0) >> 6
    bdm = rb == (lax.broadcasted_iota(jnp.int32, (GW, GW), 1) >> 6)
    bdm2 = jnp.concatenate([bdm, bdm], axis=1)

    def bd_weights(m):
        return [jnp.where(bdm, _tile4(part), jnp.zeros((), BF16)) for part in _split(m, 2)]

    def times_bd(a, m):
        ah, al = _split(a, 2)
        mh, ml = bd_weights(m)
        n = a.shape[0]
        top = jnp.dot(jnp.concatenate([ah, al], axis=0), mh, preferred_element_type=F32)
        return top[:n] + top[n:] + jnp.dot(ah, ml, preferred_element_type=F32)

    chunks = range(tg // CHUNK)
    sls = [slice(c * CHUNK, (c + 1) * CHUNK) for c in chunks]
    kq = [_dot_nt(jnp.concatenate([kb[sl], qn[sl]], axis=0), jnp.where(bdm, _tile4(kn[sl]), 0.0))
          for sl in sls]
    nmat, tmat = [], []
    for c in chunks:
        gxc = gx[sls[c]]
        grow = jnp.sum(jnp.where(eye, gxc, 0.0), axis=0, keepdims=True)
        dec = jnp.where(incl, jnp.exp(gxc - grow), 0.0)
        nmat.append(jnp.where(strict, kq[c][:CHUNK] * dec, 0.0))
        qkd_ref[0, sls[c], :] = _bf(kq[c][CHUNK:] * dec)
        tmat.append(jnp.where(eye, 1.0, 0.0) - nmat[c])
        egl_ref[0, c] = egl[c * CHUNK:c * CHUNK + 8, :]
    pmat = [times_bd(nmat[c], nmat[c]) for c in chunks]
    for step in range(5):
        if step < 4:
            tp = [times_bd(jnp.concatenate([tmat[c], pmat[c]], axis=0), pmat[c]) for c in chunks]
            tmat = [tmat[c] + tp[c][:CHUNK] for c in chunks]
            pmat = [tp[c][CHUNK:] for c in chunks]
        else:
            tmat = [tmat[c] + times_bd(tmat[c], pmat[c]) for c in chunks]
    def apply_inverse(t, rhs):
        th, tl = _split(t, 2)
        rh, rl = [jnp.where(bdm2, _tile4(part), jnp.zeros((), BF16)) for part in _split(rhs, 2)]
        top = jnp.dot(jnp.concatenate([th, tl], axis=0), rh, preferred_element_type=F32)
        return top[:CHUNK] + top[CHUNK:] + jnp.dot(th, rl, preferred_element_type=F32)

    xs = [apply_inverse(tmat[c], jnp.concatenate([vb[sls[c]], kbg[sls[c]]], axis=1)) for c in chunks]
    for c in chunks:
        xu_ref[0, sls[c], :] = xs[c][:, :GW]
        xwk_ref[0, sls[c], :] = _bf(xs[c][:, GW:])


def _gdn2_kernel(xu_ref, xwk_ref, qkd_ref, qg_ref, kdec_ref, egl_ref, z_ref, s0_ref, ng_ref, bd_ref,
                 o_ref, sfin_ref, s_ref, *, bb, nchunk):
    ti = pl.program_id(1)

    @pl.when(ti == 0)
    def _():
        s_ref[...] = s0_ref[...]

    rb = lax.broadcasted_iota(jnp.int32, (GW, GW), 0) >> 6
    bdm = rb == (lax.broadcasted_iota(jnp.int32, (GW, GW), 1) >> 6)
    bd = bd_ref[...]
    ng = ng_ref[...]

    def chunk(c, carry):
        r0 = pl.multiple_of(c * CHUNK, CHUNK)
        rows = pl.ds(r0, CHUNK)
        bs = range(bb)
        s_old = [s_ref[b] for b in bs]
        r = [jnp.dot(jnp.concatenate([xwk_ref[b, rows, :], qg_ref[b, rows, :]], axis=0), _bf(s_old[b]),
                     preferred_element_type=F32) for b in bs]
        vnew = [xu_ref[b, rows, :] - r[b][:CHUNK] for b in bs]
        o = [r[b][CHUNK:] + jnp.dot(qkd_ref[b, rows, :], _bf(jnp.where(bdm, _tile4(vnew[b]), 0.0)),
                                    preferred_element_type=F32) for b in bs]
        upd = [_dot(kdec_ref[b, rows, :].astype(F32).T, vnew[b]) for b in bs]
        for b in bs:
            s_ref[b] = egl_ref[b, c, 0:1, :] * s_old[b] + jnp.where(bdm, upd[b], 0.0)
            o_ref[b, rows, :] = _bf(_head_rms(o[b], bd, ng) * _silu(z_ref[b, rows, :]))
        return carry

    lax.fori_loop(0, nchunk, chunk, 0)
    sfin_ref[...] = s_ref[...]


def _gdn_consts(tg):
    i = np.arange(tg)
    same = i[:, None] // CHUNK == i[None, :] // CHUNK
    lblk = jnp.asarray(same & (i[:, None] >= i[None, :]), BF16)
    eb = np.zeros((128, GW), np.float32)
    eg = np.zeros((128, GW), np.float32)
    for h in range(N_HEADS):
        eb[h, h * HEAD_DIM:(h + 1) * HEAD_DIM] = 1
        eg[N_HEADS + h, h * HEAD_DIM:(h + 1) * HEAD_DIM] = 1
    return lblk, jnp.asarray(eb, BF16), jnp.asarray(eg, BF16)


def _gdn_call(qkv, small, z, st8, s0bd, cw, alog, dtb, ng, bd, tg, t_valid, bb):
    b, t, _ = qkv.shape
    n_tiles = t // tg
    lblk, eb, eg = _gdn_consts(tg)
    full = lambda a: pl.BlockSpec(a.shape, lambda bi, ti: (0,) * a.ndim)
    row = lambda n: pl.BlockSpec((1, tg, n), lambda bi, ti: (bi, ti, 0))
    nct = tg // CHUNK
    xu, xwk, qkd, qg, kdec, egl, conv_new = pl.pallas_call(
        functools.partial(_gdn1_kernel, tg=tg, t_valid=t_valid, n_tiles=n_tiles),
        grid=(b, n_tiles),
        in_specs=[row(3 * GW), row(128),
                  pl.BlockSpec((1, 8, 3 * GW), lambda bi, ti: (bi, 0, 0)),
                  full(cw), full(alog), full(dtb), full(lblk), full(eb), full(eg), full(bd)],
        out_specs=[row(GW), row(GW), row(GW), row(GW), row(GW),
                   pl.BlockSpec((1, nct, 8, GW), lambda bi, ti: (bi, ti, 0, 0)),
                   pl.BlockSpec((1, 3, 3 * GW), lambda bi, ti: (bi, 0, 0))],
        out_shape=[jax.ShapeDtypeStruct((b, t, GW), F32), jax.ShapeDtypeStruct((b, t, GW), BF16),
                   jax.ShapeDtypeStruct((b, t, GW), BF16), jax.ShapeDtypeStruct((b, t, GW), BF16),
                   jax.ShapeDtypeStruct((b, t, GW), BF16),
                   jax.ShapeDtypeStruct((b, t // CHUNK, 8, GW), F32),
                   jax.ShapeDtypeStruct((b, 3, 3 * GW), F32)],
        scratch_shapes=[pltpu.VMEM((8 + tg, 3 * GW), F32)],
        compiler_params=_params(("arbitrary", "arbitrary")),
        name="gdn_chunk_solve",
    )(qkv, small, st8, cw, alog, dtb, lblk, eb, eg, bd)

    full2 = lambda a: pl.BlockSpec(a.shape, lambda bi, ti: (0,) * a.ndim)
    rowb = pl.BlockSpec((bb, tg, GW), lambda bi, ti: (bi, ti, 0))
    sspec = pl.BlockSpec((bb, GW, GW), lambda bi, ti: (bi, 0, 0))
    o, sfin = pl.pallas_call(
        functools.partial(_gdn2_kernel, bb=bb, nchunk=nct),
        grid=(b // bb, n_tiles),
        in_specs=[rowb, rowb, rowb, rowb, rowb,
                  pl.BlockSpec((bb, nct, 8, GW), lambda bi, ti: (bi, ti, 0, 0)),
                  rowb, sspec, full2(ng), full2(bd)],
        out_specs=[rowb, sspec],
        out_shape=[jax.ShapeDtypeStruct((b, t, GW), BF16), jax.ShapeDtypeStruct((b, GW, GW), F32)],
        scratch_shapes=[pltpu.VMEM((bb, GW, GW), F32)],
        compiler_params=_params(("arbitrary", "arbitrary")),
        name="gdn_recurrence",
    )(xu, xwk, qkd, qg, kdec, egl, z, s0bd, ng, bd)
    return o, conv_new, sfin


def _out_kernel(x_ref, oa_ref, ob_ref, oc_ref, od_ref, ga1_ref, sc2_ref, sh2_ref, ga2_ref, g2_ref,
                wo_ref, wup_ref, cw_ref, wdn_ref, st_ref, y_ref, cnew_ref,
                ubuf_ref, car_ref, gs_ref, *, tm, tiles_per_seq, nff):
    i = pl.program_id(0)

    @pl.when(i % tiles_per_seq == 0)
    def _():
        car_ref[...] = st_ref[0]

    mix = jnp.dot(oa_ref[...], wo_ref[0:GW, :], preferred_element_type=F32)
    for n, ref in enumerate((ob_ref, oc_ref, od_ref)):
        mix = mix + jnp.dot(ref[...], wo_ref[(n + 1) * GW:(n + 2) * GW, :], preferred_element_type=F32)
    x1 = x_ref[...] + ga1_ref[0] * mix
    ms = jnp.mean(x1 * x1, axis=-1, keepdims=True)
    h = x1 * lax.rsqrt(ms + EPS) * g2_ref[...]
    h = _bf(h * (1.0 + sc2_ref[0]) + sh2_ref[0])
    dff = nff * GW
    for j in range(nff):
        halves = []
        for off in (0, dff):
            cols = slice(off + j * GW, off + (j + 1) * GW)
            u = jnp.dot(h, wup_ref[:, cols], preferred_element_type=F32)
            ubuf_ref[0:8, :] = car_ref[:, cols]
            ubuf_ref[8:8 + tm, :] = u
            halves.append(cw_ref[0:1, cols] * ubuf_ref[6:6 + tm, :] + cw_ref[1:2, cols] * ubuf_ref[7:7 + tm, :]
                          + cw_ref[2:3, cols] * u)
            car_ref[:, cols] = ubuf_ref[tm:tm + 8, :]
        gs_ref[:, j * GW:(j + 1) * GW] = _bf(_silu(halves[0]) * halves[1])
    y_ref[...] = x1 + ga2_ref[0] * jnp.dot(gs_ref[...], wdn_ref[...], preferred_element_type=F32)
    cnew_ref[0] = car_ref[...]


def _out_call(x2, oa, ob, oc, od, ga1, sc2, sh2, ga2, g2, wo, wup, cw, wdn, layer, st8, tm, rows_per_seq):
    m, d = x2.shape
    dff2 = wup.shape[2]
    nff = dff2 // (2 * GW)
    tiles_per_seq = rows_per_seq // tm
    nseq = m // rows_per_seq
    mods = (ga1, sc2, sh2, ga2)
    mod_specs = [_mod_spec(mo, tiles_per_seq, d) for mo in mods]
    full = lambda a: pl.BlockSpec(a.shape, lambda i: (0,) * a.ndim)
    row = lambda n: pl.BlockSpec((tm, n), lambda i: (i, 0))
    stspec = pl.BlockSpec((1, 8, dff2), lambda i: (i // tiles_per_seq, 0, 0))
    lspec = lambda a: pl.BlockSpec((None,) + a.shape[1:], lambda i: (layer, 0, 0))
    return pl.pallas_call(
        functools.partial(_out_kernel, tm=tm, tiles_per_seq=tiles_per_seq, nff=nff),
        grid=(m // tm,),
        in_specs=[row(d), row(GW), row(GW), row(GW), row(GW)] + mod_specs + [
            full(g2), lspec(wo), lspec(wup), full(cw), lspec(wdn), stspec],
        out_specs=[row(d), stspec],
        out_shape=[jax.ShapeDtypeStruct((m, d), F32), jax.ShapeDtypeStruct((nseq, 8, dff2), F32)],
        scratch_shapes=[pltpu.VMEM((8 + tm, GW), F32), pltpu.VMEM((8, dff2), F32),
                        pltpu.VMEM((tm, dff2 // 2), BF16)],
        compiler_params=_params(("arbitrary",)),
        name="out_projection_mlp",
    )(x2, oa, ob, oc, od, *(mo[0] for mo in mods), g2, wo, wup, cw, wdn, st8)


def _pack_in_weights(w_in):
    a = 4 * GW + 2 * N_HEADS
    off_b = a
    off_c = off_b + 3 * GW
    off_d = off_c + 3 * GW + N_HEADS
    dd = w_in.shape[:2]
    zeros = lambda n: jnp.zeros(dd + (n,), w_in.dtype)
    small = jnp.concatenate([w_in[..., 4 * GW:a], zeros(128 - 2 * N_HEADS),
                             w_in[..., off_c + 3 * GW:off_d], zeros(128 - N_HEADS)], axis=-1)
    packed = jnp.concatenate([w_in[..., :4 * GW], w_in[..., off_b:off_b + 3 * GW],
                              w_in[..., off_c:off_c + 3 * GW], w_in[..., off_d:off_d + 3 * GW], small],
                             axis=-1)
    return _bf(packed)


def _lane_vec(v, offset):
    depth = v.shape[0]
    out = jnp.zeros((depth, 1, 128), F32)
    return out.at[:, 0, offset:offset + N_HEADS].set(v.astype(F32))


def _band_ext(rel_table, lq, hist):
    c = np.arange(BAND_EXT)
    rel = np.clip(hist + lq - 1 - c, -REL_CLIP, REL_CLIP) + REL_CLIP
    return rel_table.astype(F32)[:, rel]


def _layer(x, mods_in, mods_out, wts, cache, cfg, stack=None):
    b, t, d = x.shape
    m = b * t
    bd = wts["bd"]
    sc1, sh1 = mods_in
    outs = _in_call(x.reshape(m, d), sc1, sh1, wts["norm_mix_g"], wts["w_in"], wts["layer"], wts["qk_gains"],
                    wts["fox_b_f"], bd, cfg["tm_in"], cfg["rows_per_mod_in"], stack)
    (gqkv, gz, small, sbq, sbk, sbv, fq, fk, fv, logf, bq, bk, bv), kv_t = outs[:13], tuple(outs[13:])
    r3 = lambda a: a.reshape(b, t, a.shape[-1])
    new = {}
    if stack is None:
        new.update({"sb_k": r3(sbk), "sb_v": r3(sbv), "fox_k": r3(fk), "fox_v": r3(fv), "fox_logf": r3(logf)})
    past = cache is not None

    tp = cfg["t_gdn"]
    pad_t = lambda a: jnp.pad(r3(a), ((0, 0), (0, tp - t), (0, 0)))
    if past:
        st8 = jnp.pad(cache["gdn_conv"], ((0, 0), (5, 0), (0, 0)))
        s0 = cache["gdn_state"]
        eye = jnp.eye(N_HEADS, dtype=F32)
        s0bd = jnp.einsum("bhkv,hg->bhkgv", s0, eye).reshape(b, GW, GW)
    else:
        st8 = jnp.zeros((b, 8, 3 * GW), F32)
        s0bd = jnp.zeros((b, GW, GW), F32)
    oa, conv_new, sfin = _gdn_call(pad_t(gqkv), pad_t(small), pad_t(gz), st8, s0bd,
                                   wts["gdn_conv_w"], wts["gdn_a_log"], wts["gdn_dt_bias"],
                                   wts["gdn_norm_g"], bd, cfg["tg"], t, cfg["bb"])
    oa = oa[:, :t].reshape(m, GW)
    sfin = sfin.reshape(b, N_HEADS, HEAD_DIM, N_HEADS, HEAD_DIM)
    new["gdn_state"] = jnp.stack([sfin[:, h, :, h, :] for h in range(N_HEADS)], axis=1)
    new["gdn_conv"] = conv_new

    tq, tk, tkf = cfg["tq"], cfg["tk"], cfg["tk_fox"]
    if past:
        pos0 = cache["sb_k"].shape[1]
        tk_all = -(-(pos0 + t) // tk) * tk
        cat = lambda c, n: jnp.pad(jnp.concatenate([c.reshape(b, pos0, -1), r3(n)], axis=1),
                                   ((0, 0), (0, tk_all - pos0 - t), (0, 0)))
        ksb, vsb = cat(cache["sb_k"], sbk), cat(cache["sb_v"], sbv)
        kfx, vfx = cat(cache["fox_k"], fk), cat(cache["fox_v"], fv)
        lf_all = cat(cache["fox_logf"], logf)
    else:
        pos0, tk_all = 0, t
        ksb, vsb, kfx, vfx = r3(sbk), r3(sbv), r3(fk), r3(fv)
    ob = _sb_call(r3(sbq), ksb, vsb, wts["merge_g"][0:1], bd, tq, tk, pos0)
    lf_t = jnp.swapaxes(lf_all, 1, 2) if past or stack is None else kv_t[6][stack[1]]
    fcum = _cumsum_call(lf_t.reshape(b * N_HEADS, tk_all))
    fcum = fcum.reshape(b, N_HEADS, tk_all)
    f_q = jnp.swapaxes(fcum[:, :, pos0:pos0 + t], 1, 2)
    f_k = jnp.swapaxes(fcum.reshape(b, N_HEADS, tk_all // tkf, tkf), 1, 2)
    oc = _fox_call(r3(fq), kfx, vfx, f_q, f_k, wts["merge_g"][1:2], bd, tq, tkf, pos0)

    if past:
        hk, hv = cache["band_k"].reshape(b, -1, GW), cache["band_v"].reshape(b, -1, GW)
        keep = hk.shape[1]
        new["band_k"] = jnp.concatenate([hk, r3(bk)], axis=1)[:, -keep:]
        new["band_v"] = jnp.concatenate([hv, r3(bv)], axis=1)[:, -keep:]
    else:
        hk = hv = jnp.zeros((b, BAND_ROWS, GW), F32)
        if stack is None:
            keep = min(BAND_ROWS, t)
            new["band_k"], new["band_v"] = r3(bk)[:, -keep:], r3(bv)[:, -keep:]
    od = _band_call(r3(bq), r3(bk), r3(bv), hk, hv, wts["band_ext_" + cfg["name"]],
                    wts["merge_g"][2:3], bd, cfg["lq"], cfg["nch"], past)

    if past:
        st_ffn = jnp.pad(cache["ffn_conv"], ((0, 0), (6, 0), (0, 0)))
    else:
        st_ffn = jnp.zeros((b, 8, wts["w_up"].shape[2]), F32)
    ga1, sc2, sh2, ga2 = mods_out
    y, cnew = _out_call(x.reshape(m, d), oa, ob.reshape(m, GW), oc.reshape(m, GW), od.reshape(m, GW),
                        ga1, sc2, sh2, ga2, wts["norm_ffn_g"], wts["w_o"], wts["w_up"],
                        wts["ffn_conv_w"], wts["w_down"], wts["layer"], st_ffn, cfg["tm_out"], t)
    new["ffn_conv"] = cnew[:, 6:8]
    hd = lambda a: a.reshape(a.shape[0], a.shape[1], N_HEADS, HEAD_DIM)
    for n in ("sb_k", "sb_v", "fox_k", "fox_v", "band_k", "band_v"):
        if n in new:
            new[n] = hd(new[n])
    return y.reshape(b, t, d), new, kv_t


STATE_KEYS = ("gdn_conv", "gdn_state", "sb_k", "sb_v", "fox_k", "fox_v", "fox_logf",
              "band_k", "band_v", "ffn_conv")


def _gdn_batch(b):
    return next(n for n in (8, 4, 2, 1) if b % n == 0)


def _group_cfg(name, b, t):
    if t % 512 == 0:
        return dict(name=name, tm_in=512, rows_per_mod_in=t, tm_out=512, t_gdn=t, tg=512,
                    bb=_gdn_batch(b), tq=256, tk=256, tk_fox=256, lq=CHUNK, nch=4)
    assert t % 16 == 0 and t <= CHUNK
    return dict(name=name, tm_in=b * t, rows_per_mod_in=b * t, tm_out=t, t_gdn=CHUNK, tg=CHUNK,
                bb=_gdn_batch(b), tq=t, tk=256, tk_fox=256, lq=t, nch=1)


def kernel(x_prompt, x_sample, c_prompt, c_sample, state_gdn_conv, state_gdn, cache_sb_k, cache_sb_v, cache_fox_k, cache_fox_v, cache_fox_logf, cache_band_k, cache_band_v, state_ffn_conv, ada_w, ada_b, norm_mix_g, w_in, gdn_conv_w, gdn_a_log, gdn_dt_bias, gdn_norm_g, fox_q_g, fox_k_g, fox_b_f, band_q_g, band_k_g, band_rel_bias, merge_g, w_o, norm_ffn_g, w_up, ffn_conv_w, w_down):
    depth = ada_w.shape[0]
    bp, tp, d = x_prompt.shape
    bs, ts, _ = x_sample.shape
    cfg_p = _group_cfg("p", bp, tp)
    cfg_s = _group_cfg("s", bs, ts)
    hist_s = cache_band_k.shape[2]

    mod = _ada_call(jnp.concatenate([c_prompt, c_sample], axis=0), ada_w, ada_b)
    nb = bp + bs
    mod = mod.reshape(depth, nb, 6, d)
    mod_rows = mod.reshape(depth * nb * 6, 1, d)

    tile_h = lambda g: jnp.tile(g.astype(F32), (1, N_HEADS))[:, None, :]
    w_in_p = _pack_in_weights(w_in)
    w_o_b, w_up_b, w_dn_b = _bf(w_o), _bf(w_up), _bf(w_down)
    ffn_cw = jnp.pad(ffn_conv_w.astype(F32), ((0, 0), (0, 5), (0, 0)))
    gdn_cw = jnp.pad(gdn_conv_w.astype(F32), ((0, 0), (0, 4), (0, 0)))
    qk_gains = jnp.concatenate([tile_h(fox_q_g), tile_h(fox_k_g), tile_h(band_q_g), tile_h(band_k_g),
                                jnp.zeros((depth, 4, GW), F32)], axis=1)
    bd = _const_bd256()

    y_p, y_s = x_prompt, x_sample
    kv_p = None
    new_p = {n: [] for n in STATE_KEYS}
    new_s = {n: [] for n in STATE_KEYS}
    for l in range(depth):
        wts = {
            "bd": bd, "layer": l, "norm_mix_g": norm_mix_g[l][None].astype(F32), "w_in": w_in_p,
            "qk_gains": qk_gains[l], "fox_b_f": _lane_vec(fox_b_f, 0)[l],
            "gdn_conv_w": gdn_cw[l], "gdn_a_log": _lane_vec(gdn_a_log, N_HEADS)[l],
            "gdn_dt_bias": _lane_vec(gdn_dt_bias, N_HEADS)[l], "gdn_norm_g": tile_h(gdn_norm_g)[l],
            "merge_g": merge_g[l].reshape(3, GW).astype(F32),
            "band_ext_p": _band_ext(band_rel_bias[l], cfg_p["lq"], BAND_ROWS),
            "band_ext_s": _band_ext(band_rel_bias[l], cfg_s["lq"], hist_s),
            "w_o": w_o_b, "norm_ffn_g": norm_ffn_g[l][None].astype(F32), "w_up": w_up_b,
            "ffn_conv_w": ffn_cw[l], "w_down": w_dn_b,
        }
        pm = lambda c: (mod_rows, l * nb * 6 + c, 6)
        sm1 = lambda c: (mod_rows, (l * nb + bp) * 6 + c, 6)
        srow = lambda c: (jnp.repeat(mod[l, bp:, c], ts, axis=0)[None], 0, 0)
        mods_in_p = (pm(1), pm(0))
        mods_out_p = (pm(2), pm(4), pm(3), pm(5))
        mods_in_s = (srow(1), srow(0))
        mods_out_s = (sm1(2), sm1(4), sm1(3), sm1(5))
        cache = {"gdn_conv": state_gdn_conv[l], "gdn_state": state_gdn[l],
                 "sb_k": cache_sb_k[l], "sb_v": cache_sb_v[l], "fox_k": cache_fox_k[l],
                 "fox_v": cache_fox_v[l], "fox_logf": cache_fox_logf[l],
                 "band_k": cache_band_k[l], "band_v": cache_band_v[l], "ffn_conv": state_ffn_conv[l]}
        y_p, st_p, kv_p = _layer(y_p, mods_in_p, mods_out_p, wts, None, cfg_p, (depth, l, tp, kv_p or None))
        y_s, st_s, _ = _layer(y_s, mods_in_s, mods_out_s, wts, cache, cfg_s)
        for n in STATE_KEYS:
            if n in st_p:
                new_p[n].append(st_p[n])
            new_s[n].append(st_s[n])
    out_p = {n: jnp.stack(v) for n, v in new_p.items() if v}
    for n, a in zip(("sb_k", "sb_v", "fox_k", "fox_v", "band_k", "band_v"), kv_p):
        out_p[n] = jnp.transpose(a.reshape(depth, bp, N_HEADS, HEAD_DIM, a.shape[-1]), (0, 1, 4, 2, 3))
    out_p["fox_logf"] = jnp.swapaxes(kv_p[6], 2, 3)
    return ((y_p, y_s) + tuple(out_p[n] for n in STATE_KEYS)
            + tuple(jnp.stack(new_s[n]) for n in STATE_KEYS))
```

```python
import functools

import numpy as np
import jax
import jax.numpy as jnp
from jax import lax
from jax.experimental import pallas as pl
from jax.experimental.pallas import tpu as pltpu

F32 = jnp.float32
BF16 = jnp.bfloat16

HEAD_DIM = 64
N_HEADS = 4
GW = N_HEADS * HEAD_DIM
CHUNK = 64
BAND_ROWS = 512
REL_CLIP = 2 * CHUNK
EPS = 1e-6
NEG = -1e30
VMEM_LIMIT = 56 * 1024 * 1024
BAND_WIN = BAND_ROWS + 128
BAND_EXT = BAND_WIN + 128


def _bf(x):
    return x.astype(BF16)


def _dot(a, b):
    return jnp.dot(_bf(a), _bf(b), preferred_element_type=F32)


def _dot_nt(a, b):
    return lax.dot_general(_bf(a), _bf(b), (((1,), (1,)), ((), ())), preferred_element_type=F32)


def _split(x, n):
    parts = []
    for _ in range(n - 1):
        hi = _bf(x)
        parts.append(hi)
        x = x - hi.astype(F32)
    parts.append(_bf(x))
    return parts


def _dot_exact_rhs(x, m, n):
    return sum(jnp.dot(p, m, preferred_element_type=F32) for p in _split(x, n))


def _dot_exact_lhs(m, x, n):
    return sum(jnp.dot(m, p, preferred_element_type=F32) for p in _split(x, n))


def _sigmoid(x):
    return 1.0 / (1.0 + jnp.exp(-x))


def _silu(x):
    return x * _sigmoid(x)


def _softplus(x):
    return jnp.maximum(x, 0.0) + jnp.log(1.0 + jnp.exp(-jnp.abs(x)))


def _head_rms(o, bd, gain, passes=1):
    ss = _dot_exact_rhs(o * o, bd, passes)
    return o * lax.rsqrt(ss * (1.0 / HEAD_DIM) + EPS) * gain


def _head_masks(width=GW):
    lane = lax.broadcasted_iota(jnp.int32, (1, width), 1)
    return [((lane & (GW - 1)) >> 6) == h for h in range(N_HEADS)]


def _mod_spec(mod, tiles_per_mod, d):
    arr, first, stride = mod
    return pl.BlockSpec((1, arr.shape[1], d), lambda i: (first + (i // tiles_per_mod) * stride, 0, 0))


def _params(sem, **kw):
    return pltpu.CompilerParams(dimension_semantics=sem, vmem_limit_bytes=VMEM_LIMIT, **kw)


def _np_bd(n, blk):
    i = np.arange(n)
    return (i[:, None] // blk == i[None, :] // blk)


def _const_bd256():
    return jnp.asarray(_np_bd(GW, HEAD_DIM), BF16)


def _const_tri_fwd(n):
    i = np.arange(n)
    return jnp.asarray(i[:, None] <= i[None, :], BF16)


def _const_tri_rev(n):
    i = np.arange(n)
    return jnp.asarray(i[:, None] >= i[None, :], BF16)


def _ada_kernel(c_ref, w_ref, b_ref, o_ref):
    s = _silu(c_ref[...])
    o_ref[0] = _dot(s, w_ref[0]) + b_ref[0]


def _ada_call(c_all, ada_w, ada_b):
    depth, d, n = ada_w.shape
    rows = c_all.shape[0]
    tn = 1536
    return pl.pallas_call(
        _ada_kernel,
        grid=(depth, n // tn),
        in_specs=[pl.BlockSpec((rows, d), lambda l, j: (0, 0)),
                  pl.BlockSpec((1, d, tn), lambda l, j: (l, 0, j)),
                  pl.BlockSpec((1, 1, tn), lambda l, j: (l, 0, j))],
        out_specs=pl.BlockSpec((1, rows, tn), lambda l, j: (l, 0, j)),
        out_shape=jax.ShapeDtypeStruct((depth, rows, n), F32),
        compiler_params=_params(("arbitrary", "arbitrary")),
        name="ada_modulation",
    )(c_all, ada_w, ada_b.reshape(depth, 1, n))


IN_GROUPS = 14
IN_COLS_PACKED = IN_GROUPS * GW


def _in_kernel(x_ref, sc_ref, sh_ref, g_ref, w_ref, gains_ref, bf_ref, bd_ref,
               gqkv_ref, gz_ref, small_ref, sbq_ref, sbk_ref, sbv_ref,
               fq_ref, fk_ref, fv_ref, logf_ref, bq_ref, bk_ref, bv_ref, *t_refs):
    x = x_ref[...]
    ms = jnp.mean(x * x, axis=-1, keepdims=True)
    h = x * lax.rsqrt(ms + EPS) * g_ref[...]
    h = _bf(h * (1.0 + sc_ref[0]) + sh_ref[0])
    bd = bd_ref[...]
    scale = HEAD_DIM ** -0.5

    def col(j, n=1):
        return jnp.dot(h, w_ref[:, j * GW:(j + n) * GW], preferred_element_type=F32)

    gqkv_ref[...] = col(0, 3)
    gz_ref[...] = col(3)
    sbq_ref[...] = _bf(col(4) * scale)
    def keep(ref, n, val):
        ref[...] = val
        if t_refs:
            t_refs[n][...] = val.T

    keep(sbk_ref, 0, col(5))
    keep(sbv_ref, 1, col(6))
    fq_ref[...] = _bf(_head_rms(col(7), bd, gains_ref[0:1, :], 2) * scale)
    keep(fk_ref, 2, _head_rms(col(8), bd, gains_ref[1:2, :], 2))
    keep(fv_ref, 3, col(9))
    bq_ref[...] = _bf(_head_rms(col(10), bd, gains_ref[2:3, :], 2) * scale)
    bk = _head_rms(col(11), bd, gains_ref[3:4, :], 2)
    bv = col(12)
    bk_ref[...] = bk
    bv_ref[...] = bv
    if t_refs:
        t_refs[4][...] = bk.T
        t_refs[5][...] = bv.T
    sm = col(13)
    small_ref[...] = sm[:, :128]
    lf = sm[:, 128:] + bf_ref[...]
    logf = jnp.minimum(lf, 0.0) - jnp.log(1.0 + jnp.exp(-jnp.abs(lf)))
    logf_ref[...] = logf[:, :N_HEADS]
    if t_refs:
        t_refs[6][...] = logf.T[:N_HEADS]


N_STACKED = 7


def _in_kernel_aliased(*refs):
    n_in = 8
    _in_kernel(*refs[:n_in], *refs[n_in + N_STACKED:])


def _in_call(x2, sc, sh, g, w, layer, gains, bfv, bd, tm, rows_per_mod, stack=None):
    m, d = x2.shape
    tiles_per_mod = rows_per_mod // tm
    full = lambda a: pl.BlockSpec(a.shape, lambda i: (0,) * a.ndim)
    row = lambda n: pl.BlockSpec((tm, n), lambda i: (i, 0))
    out_defs = [(3 * GW, F32), (GW, F32), (128, F32), (GW, BF16), (GW, F32), (GW, F32),
                (GW, BF16), (GW, F32), (GW, F32), (N_HEADS, F32), (GW, BF16), (GW, F32), (GW, F32)]
    out_specs = [row(n) for n, _ in out_defs]
    out_shape = [jax.ShapeDtypeStruct((m, n), dt) for n, dt in out_defs]
    wspec = pl.BlockSpec((None,) + w.shape[1:], lambda i: (layer, 0, 0))
    in_specs = [row(d), _mod_spec(sc, tiles_per_mod, d), _mod_spec(sh, tiles_per_mod, d), full(g), wspec,
                full(gains), full(bfv), full(bd)]
    args = [x2, sc[0], sh[0], g, w, gains, bfv, bd]
    body, aliases = _in_kernel, {}
    if stack is not None:
        depth, layer, t, bufs = stack
        tps = t // tm
        assert tm == min(BAND_ROWS, t)
        tspec = pl.BlockSpec((None, None, GW, tm), lambda i: (layer, i // tps, 0, i % tps))
        bspec = pl.BlockSpec((None, None, GW, tm), lambda i: (layer, i // tps, 0, 0))
        if bufs is not None:
            body = _in_kernel_aliased
            in_specs += [pl.BlockSpec(memory_space=pl.ANY)] * N_STACKED
            aliases = {len(args) + n: len(out_specs) + n for n in range(N_STACKED)}
            args += list(bufs)
        fspec = pl.BlockSpec((None, None, N_HEADS, tm), lambda i: (layer, i // tps, 0, i % tps))
        out_specs += [tspec] * 4 + [bspec] * 2 + [fspec]
        out_shape += ([jax.ShapeDtypeStruct((depth, m // t, GW, t), F32)] * 4
                      + [jax.ShapeDtypeStruct((depth, m // t, GW, tm), F32)] * 2
                      + [jax.ShapeDtypeStruct((depth, m // t, N_HEADS, t), F32)])
    return pl.pallas_call(
        body,
        grid=(m // tm,),
        in_specs=in_specs,
        out_specs=out_specs,
        out_shape=out_shape,
        input_output_aliases=aliases,
        compiler_params=_params(("arbitrary",)),
        name="in_projection",
    )(*args)


def _cumsum_kernel(x_ref, u_ref, o_ref):
    r, tk = x_ref.shape
    carry = jnp.zeros((r, 1), F32)
    for blk in range(tk // 256):
        o = _dot_exact_rhs(x_ref[:, blk * 256:(blk + 1) * 256], u_ref[...], 3) + carry
        o_ref[:, blk * 256:(blk + 1) * 256] = o
        carry = o[:, 255:256]


def _cumsum_call(x):
    return pl.pallas_call(
        _cumsum_kernel,
        out_shape=jax.ShapeDtypeStruct(x.shape, F32),
        compiler_params=pltpu.CompilerParams(vmem_limit_bytes=VMEM_LIMIT),
        name="logf_cumsum",
    )(x, _const_tri_fwd(256))


def _fill_kv(k_ref, v_ref, kb_ref, vm_ref, hmask, tk):
    for kbi in range(k_ref.shape[1] // tk):
        rows = slice(kbi * tk, (kbi + 1) * tk)
        kb_ref[rows, :] = _bf(k_ref[0, rows, :])
        vv = v_ref[0, rows, :]
        for h in range(N_HEADS):
            vm_ref[kbi, h * tk:(h + 1) * tk, :] = _bf(jnp.where(hmask[h], vv, 0.0))


def _stack_heads(q, hmask):
    return jnp.concatenate([jnp.where(hmask[h], q, jnp.zeros_like(q)) for h in range(N_HEADS)], axis=0)


def _heads_to_lanes(w, tq):
    return jnp.concatenate([w[h * tq:(h + 1) * tq] for h in range(N_HEADS)], axis=1)


def _spread_heads(col, hmask, tq):
    out = col[(N_HEADS - 1) * tq:]
    for h in range(N_HEADS - 2, -1, -1):
        out = jnp.where(hmask[h], col[h * tq:(h + 1) * tq], out)
    return jnp.broadcast_to(out, (tq, GW))


def _stacked_positions(q0, kbi, tq, tk):
    rows = lax.broadcasted_iota(jnp.int32, (N_HEADS * tq, tk), 0) & (tq - 1)
    cols = lax.broadcasted_iota(jnp.int32, (N_HEADS * tq, tk), 1)
    return q0 + rows, kbi * tk + cols


def _sb_kernel(q_ref, k_ref, v_ref, u_ref, bd_ref, mg_ref, o_ref,
               kb_ref, vm_ref, acc_ref, car_ref, *, tq, tk, pos0):
    qi = pl.program_id(1)
    hmask = _head_masks()

    @pl.when(qi == 0)
    def _():
        _fill_kv(k_ref, v_ref, kb_ref, vm_ref, hmask, tk)

    nqs = _stack_heads(-q_ref[0], hmask)
    acc_ref[...] = jnp.zeros_like(acc_ref)
    car_ref[...] = jnp.zeros_like(car_ref)
    q0 = pos0 + qi * tq
    n_full = q0 // tk

    def row_parts(n):
        rows = N_HEADS * tq // n
        return [slice(i * rows, (i + 1) * rows) for i in range(n)]

    def scores(kbi, n):
        start = pl.multiple_of(kbi * tk, tk)
        kblk = kb_ref[pl.ds(start, tk), :]
        return [_dot_nt(nqs[sl], kblk) for sl in row_parts(n)]

    def log_rest(kbi, y, masked):
        rcs, valid = [], None
        parts = row_parts(len(y))
        if masked:
            rowp, colp = _stacked_positions(q0, kbi, tq, tk)
            valid = colp < rowp
        for sl, yh in zip(parts, y):
            neg_abs = pltpu.bitcast(pltpu.bitcast(yh, jnp.uint32) | jnp.uint32(0x80000000), F32)
            r = jnp.minimum(yh, 0.0) - jnp.log(1.0 + jnp.exp(neg_abs))
            if masked:
                r = jnp.where(valid[sl], r, 0.0)
            rcs.append(jnp.dot(_bf(r), u_ref[...], preferred_element_type=F32))
        return rcs, valid

    def weights(kbi, y, rc, valid):
        ws, parts = [], row_parts(len(y))
        for sl, yh, rch in zip(parts, y, rc):
            car = car_ref[sl, :]
            w = jnp.exp(rch + jnp.concatenate([car] * (tk // 128), axis=1) - yh)
            if valid is not None:
                w = jnp.where(valid[sl], w, 0.0)
            car_ref[sl, :] = car + jnp.broadcast_to(rch[:, 0:1], car.shape)
            ws.append(_bf(w))
        per = N_HEADS // len(y)
        wcat = jnp.concatenate([ws[h // per][(h % per) * tq:(h % per + 1) * tq] for h in range(N_HEADS)], axis=1)
        acc_ref[...] += jnp.dot(wcat, vm_ref[kbi], preferred_element_type=F32)

    def blocks(kbis, masked):
        ys = [scores(kbi, 2 if len(kbis) == 1 else 1) for kbi in kbis]
        pend = None
        for kbi, y, mk in zip(kbis, ys, masked):
            rc, valid = log_rest(kbi, y, mk)
            if pend is not None:
                weights(*pend)
            pend = (kbi, y, rc, valid)
        weights(*pend)

    @pl.when(n_full == 0)
    def _():
        blocks([n_full], [True])

    @pl.when(n_full > 0)
    def _():
        blocks([n_full, n_full - 1], [True, False])

    rest = jnp.maximum(n_full - 1, 0)

    def body(i, c):
        hi = rest - 1 - 2 * i
        blocks([hi, hi - 1], [False, False])
        return c

    lax.fori_loop(0, rest // 2, body, 0)

    @pl.when(rest % 2 == 1)
    def _():
        blocks([0], [False])

    o_ref[0] = _bf(_head_rms(acc_ref[...], bd_ref[...], mg_ref[...]))


def _fox_kernel(q_ref, k_ref, v_ref, fq_ref, fk_ref, bd_ref, mg_ref, o_ref,
                kb_ref, vm_ref, acc_ref, fqb_ref, *, tq, tk, pos0):
    qi = pl.program_id(1)
    hmask = _head_masks()

    @pl.when(qi == 0)
    def _():
        _fill_kv(k_ref, v_ref, kb_ref, vm_ref, hmask, tk)

    qs = _stack_heads(q_ref[0], hmask)
    fq = fq_ref[0]
    for h in range(N_HEADS):
        fqb_ref[h * tq:(h + 1) * tq, :] = jnp.broadcast_to(fq[:, h:h + 1], (tq, tk))
    acc_ref[...] = jnp.zeros_like(acc_ref)
    q0 = pos0 + qi * tq
    n_full = q0 // tk

    def scores(kbi):
        start = pl.multiple_of(kbi * tk, tk)
        return _dot_nt(qs, kb_ref[pl.ds(start, tk), :])

    def update(kbi, z, masked, ml):
        m_old, l_old = ml
        fk = fk_ref[0, kbi]
        s = jnp.concatenate([z[h * tq:(h + 1) * tq] + fqb_ref[h * tq:(h + 1) * tq, :] - fk[h:h + 1, :]
                             for h in range(N_HEADS)], axis=0)
        if masked:
            rowp, colp = _stacked_positions(q0, kbi, tq, tk)
            s = jnp.where(colp <= rowp, s, NEG)
        m_new = jnp.maximum(m_old, jnp.max(s, axis=1, keepdims=True))
        alpha = jnp.exp(m_old - m_new)
        p = jnp.exp(s - m_new)
        l_new = alpha * l_old + jnp.sum(p, axis=1, keepdims=True)
        pv = jnp.dot(_heads_to_lanes(_bf(p), tq), vm_ref[kbi], preferred_element_type=F32)
        acc_ref[...] = acc_ref[...] * _spread_heads(alpha, hmask, tq) + pv
        return m_new, l_new

    def pair(i, ml):
        za, zb = scores(2 * i), scores(2 * i + 1)
        return update(2 * i + 1, zb, False, update(2 * i, za, False, ml))

    ml = (jnp.full((N_HEADS * tq, 1), NEG, F32), jnp.zeros((N_HEADS * tq, 1), F32))
    rest = jnp.maximum(n_full - 1, 0)
    ml = lax.fori_loop(0, rest // 2, pair, ml)
    ml = lax.cond(rest % 2 == 1, lambda c: update(rest - 1, scores(rest - 1), False, c), lambda c: c, ml)

    def last_two(c):
        za, zb = scores(n_full - 1), scores(n_full)
        return update(n_full, zb, True, update(n_full - 1, za, False, c))

    _, l_fin = lax.cond(n_full > 0, last_two, lambda c: update(n_full, scores(n_full), True, c), ml)
    o = acc_ref[...] / _spread_heads(l_fin, hmask, tq)
    o_ref[0] = _bf(_head_rms(o, bd_ref[...], mg_ref[...]))


def _attn_specs(tk_all, tq):
    qspec = pl.BlockSpec((1, tq, GW), lambda bi, qi: (bi, qi, 0))
    kspec = pl.BlockSpec((1, tk_all, GW), lambda bi, qi: (bi, 0, 0))
    return qspec, kspec


def _sb_call(q, k, v, mg, bd, tq, tk, pos0):
    b, tq_all, _ = q.shape
    tk_all = k.shape[1]
    qspec, kspec = _attn_specs(tk_all, tq)
    full = lambda a: pl.BlockSpec(a.shape, lambda bi, qi: (0,) * a.ndim)
    u = _const_tri_rev(tk)
    return pl.pallas_call(
        functools.partial(_sb_kernel, tq=tq, tk=tk, pos0=pos0),
        grid=(b, tq_all // tq),
        in_specs=[qspec, kspec, kspec, full(u), full(bd), full(mg)],
        out_specs=qspec,
        out_shape=jax.ShapeDtypeStruct((b, tq_all, GW), BF16),
        scratch_shapes=[pltpu.VMEM((tk_all, GW), BF16), pltpu.VMEM((tk_all // tk, N_HEADS * tk, GW), BF16),
                        pltpu.VMEM((tq, GW), F32), pltpu.VMEM((N_HEADS * tq, 128), F32)],
        compiler_params=_params(("arbitrary", "arbitrary")),
        name="stick_breaking_attention",
    )(q, k, v, u, bd, mg)


def _fox_call(q, k, v, fq, fk, mg, bd, tq, tk, pos0):
    b, tq_all, _ = q.shape
    tk_all = k.shape[1]
    qspec, kspec = _attn_specs(tk_all, tq)
    full = lambda a: pl.BlockSpec(a.shape, lambda bi, qi: (0,) * a.ndim)
    return pl.pallas_call(
        functools.partial(_fox_kernel, tq=tq, tk=tk, pos0=pos0),
        grid=(b, tq_all // tq),
        in_specs=[qspec, kspec, kspec,
                  pl.BlockSpec((1, tq, N_HEADS), lambda bi, qi: (bi, qi, 0)),
                  pl.BlockSpec((1, tk_all // tk, N_HEADS, tk), lambda bi, qi: (bi, 0, 0, 0)),
                  full(bd), full(mg)],
        out_specs=qspec,
        out_shape=jax.ShapeDtypeStruct((b, tq_all, GW), BF16),
        scratch_shapes=[pltpu.VMEM((tk_all, GW), BF16), pltpu.VMEM((tk_all // tk, N_HEADS * tk, GW), BF16),
                        pltpu.VMEM((tq, GW), F32), pltpu.VMEM((N_HEADS * tq, tk), F32)],
        compiler_params=_params(("arbitrary", "arbitrary")),
        name="forgetting_attention",
    )(q, k, v, fq, fk, bd, mg)


def _band_kernel(q_ref, k_ref, v_ref, hk_ref, hv_ref, ext_ref, bd_ref, mg_ref, o_ref,
                 kp_ref, vp_ref, bias_ref, *, lq, nch, has_hist):
    qi = pl.program_id(1)
    hmask = _head_masks()
    t = k_ref.shape[1]
    hist = hk_ref.shape[1]
    ext_w = ext_ref.shape[1]

    @pl.when(qi == 0)
    def _():
        def put(r0, kk, vv):
            n = kk.shape[0]
            kp_ref[r0:r0 + n, :] = _bf(kk)
            vp_ref[r0:r0 + n, :] = _bf(vv)
        step = min(256, t)
        for r in range(0, hist, 256):
            put(r, hk_ref[0, r:r + 256, :], hv_ref[0, r:r + 256, :])
        for r in range(0, t, step):
            put(hist + r, k_ref[0, r:r + step, :], v_ref[0, r:r + step, :])
        zeros = jnp.zeros((128, GW), F32)
        put(hist + t, zeros, zeros)
        for h in range(N_HEADS):
            e = jnp.broadcast_to(ext_ref[h:h + 1, :], (lq, ext_w))
            e = pltpu.roll(e, ext_w - (lq - 1), 1, stride=1, stride_axis=0)
            colb = lax.broadcasted_iota(jnp.int32, (lq, BAND_WIN), 1)
            bias_ref[h * lq:(h + 1) * lq, :] = jnp.where(colb < hist + lq, e[:, :BAND_WIN], NEG)

    col = lax.broadcasted_iota(jnp.int32, (N_HEADS * lq, BAND_WIN), 1)
    chunks = range(nch)
    starts = [pl.multiple_of((qi * nch + ci) * lq, lq) for ci in chunks]
    s_all = [_dot_nt(_stack_heads(q_ref[0, ci * lq:(ci + 1) * lq, :], hmask),
                     kp_ref[pl.ds(starts[ci], BAND_WIN), :]) for ci in chunks]
    p_all = []
    for ci in chunks:
        s = s_all[ci] + bias_ref[...]
        if not has_hist:
            s = jnp.where(col >= hist - (qi * nch + ci) * lq, s, NEG)
        p = jnp.exp(s - jnp.max(s, axis=1, keepdims=True))
        p_all.append(_bf(p / jnp.sum(p, axis=1, keepdims=True)))
    pv_all = [jnp.dot(p_all[ci], vp_ref[pl.ds(starts[ci], BAND_WIN), :], preferred_element_type=F32)
              for ci in chunks]
    for ci in chunks:
        acc = jnp.zeros((lq, GW), F32)
        for h in range(N_HEADS):
            acc = acc + jnp.where(hmask[h], pv_all[ci][h * lq:(h + 1) * lq], 0.0)
        o_ref[0, ci * lq:(ci + 1) * lq, :] = _bf(_head_rms(acc, bd_ref[...], mg_ref[...]))


def _band_call(q, k, v, hk, hv, ext, mg, bd, lq, nch, has_hist):
    b, t, _ = q.shape
    hist = hk.shape[1]
    tqb = lq * nch
    qspec = pl.BlockSpec((1, tqb, GW), lambda bi, qi: (bi, qi, 0))
    kspec = pl.BlockSpec((1, t, GW), lambda bi, qi: (bi, 0, 0))
    hspec = pl.BlockSpec((1, hist, GW), lambda bi, qi: (bi, 0, 0))
    full = lambda a: pl.BlockSpec(a.shape, lambda bi, qi: (0,) * a.ndim)
    rows = hist + t + 128
    return pl.pallas_call(
        functools.partial(_band_kernel, lq=lq, nch=nch, has_hist=has_hist),
        grid=(b, t // tqb),
        in_specs=[qspec, kspec, kspec, hspec, hspec, full(ext), full(bd), full(mg)],
        out_specs=qspec,
        out_shape=jax.ShapeDtypeStruct((b, t, GW), BF16),
        scratch_shapes=[pltpu.VMEM((rows, GW), BF16), pltpu.VMEM((rows, GW), BF16),
                        pltpu.VMEM((N_HEADS * lq, BAND_WIN), F32)],
        compiler_params=_params(("arbitrary", "arbitrary")),
        name="band_attention",
    )(q, k, v, hk, hv, ext, bd, mg)


def _tile4(x):
    return jnp.concatenate([x] * N_HEADS, axis=0)


def _gdn1_kernel(qkv_ref, small_ref, st_ref, cw_ref, alog_ref, dtb_ref,
                 lblk_ref, eb_ref, eg_ref, bd_ref,
                 xu_ref, xwk_ref, qkd_ref, qg_ref, kdec_ref, egl_ref, conv_ref,
                 xbuf_ref, *, tg, t_valid, n_tiles):
    ti = pl.program_id(1)

    @pl.when(ti == 0)
    def _():
        xbuf_ref[0:8, :] = st_ref[0]

    xbuf_ref[8:8 + tg, :] = qkv_ref[0]
    y = cw_ref[3:4, :] * xbuf_ref[8:8 + tg, :]
    for i in range(3):
        y = y + cw_ref[i:i + 1, :] * xbuf_ref[5 + i:5 + i + tg, :]
    tv = t_valid - (n_tiles - 1) * tg

    @pl.when(ti == n_tiles - 1)
    def _():
        conv_ref[0] = xbuf_ref[8 + tv - 3:8 + tv, :]

    xbuf_ref[0:8, :] = xbuf_ref[tg:tg + 8, :]
    y = _silu(y)
    bd = bd_ref[...]
    q, k, v = y[:, :GW], y[:, GW:2 * GW], y[:, 2 * GW:]
    qn = q * lax.rsqrt(_dot_exact_rhs(q * q, bd, 1) + EPS) * (HEAD_DIM ** -0.5)
    kn = k * lax.rsqrt(_dot_exact_rhs(k * k, bd, 1) + EPS)

    sm = small_ref[0]
    rowv = ti * tg + lax.broadcasted_iota(jnp.int32, (tg, 1), 0) < t_valid
    beta_all = jnp.where(rowv, _sigmoid(sm), 0.0)
    g_all = jnp.where(rowv, -jnp.exp(alog_ref[...]) * _softplus(sm + dtb_ref[...]), 0.0)
    gc = _dot_exact_lhs(lblk_ref[...], g_all, 3)
    gl = jnp.concatenate([jnp.broadcast_to(gc[c * CHUNK + CHUNK - 1:(c + 1) * CHUNK, :], (CHUNK, 128))
                          for c in range(tg // CHUNK)], axis=0)
    bexp = _dot_exact_rhs(beta_all, eb_ref[...], 3)
    gx = _dot_exact_rhs(gc, eg_ref[...], 3)
    glx = _dot_exact_rhs(gl, eg_ref[...], 3)
    eg = jnp.exp(gx)
    kb = kn * bexp
    vb = v * bexp
    kbg = kb * eg
    qg_ref[0] = _bf(qn * eg)
    kdec_ref[0] = _bf(kn * jnp.exp(glx - gx))
    egl = jnp.exp(glx)

    ri = lax.broadcasted_iota(jnp.int32, (CHUNK, GW), 0)
    li = lax.broadcasted_iota(jnp.int32, (CHUNK, GW), 1) & (CHUNK - 1)
    eye = ri == li
    incl = ri >= li
    strict = ri > li
    rb = lax.broadcasted_iota(jnp.int32, (GW, GW), 0) >> 6
    bdm = rb == (lax.broadcasted_iota(jnp.int32, (GW, GW), 1) >> 6)
    bdm2 = jnp.concatenate([bdm, bdm], axis=1)

    def bd_weights(m):
        return [jnp.where(bdm, _tile4(part), jnp.zeros((), BF16)) for part in _split(m, 2)]

    def times_bd(a, m):
        ah, al = _split(a, 2)
        mh, ml = bd_weights(m)
        n = a.shape[0]
        top = jnp.dot(jnp.concatenate([ah, al], axis=0), mh, preferred_element_type=F32)
        return top[:n] + top[n:] + jnp.dot(ah, ml, preferred_element_type=F32)

    chunks = range(tg // CHUNK)
    sls = [slice(c * CHUNK, (c + 1) * CHUNK) for c in chunks]
    kq = [_dot_nt(jnp.concatenate([kb[sl], qn[sl]], axis=0), jnp.where(bdm, _tile4(kn[sl]), 0.0))
          for sl in sls]
    nmat, tmat = [], []
    for c in chunks:
        gxc = gx[sls[c]]
        grow = jnp.sum(jnp.where(eye, gxc, 0.0), axis=0, keepdims=True)
        dec = jnp.where(incl, jnp.exp(gxc - grow), 0.0)
        nmat.append(jnp.where(strict, kq[c][:CHUNK] * dec, 0.0))
        qkd_ref[0, sls[c], :] = _bf(kq[c][CHUNK:] * dec)
        tmat.append(jnp.where(eye, 1.0, 0.0) - nmat[c])
        egl_ref[0, c] = egl[c * CHUNK:c * CHUNK + 8, :]
    pmat = [times_bd(nmat[c], nmat[c]) for c in chunks]
    for step in range(5):
        if step < 4:
            tp = [times_bd(jnp.concatenate([tmat[c], pmat[c]], axis=0), pmat[c]) for c in chunks]
            tmat = [tmat[c] + tp[c][:CHUNK] for c in chunks]
            pmat = [tp[c][CHUNK:] for c in chunks]
        else:
            tmat = [tmat[c] + times_bd(tmat[c], pmat[c]) for c in chunks]
    def apply_inverse(t, rhs):
        th, tl = _split(t, 2)
        rh, rl = [jnp.where(bdm2, _tile4(part), jnp.zeros((), BF16)) for part in _split(rhs, 2)]
        top = jnp.dot(jnp.concatenate([th, tl], axis=0), rh, preferred_element_type=F32)
        return top[:CHUNK] + top[CHUNK:] + jnp.dot(th, rl, preferred_element_type=F32)

    xs = [apply_inverse(tmat[c], jnp.concatenate([vb[sls[c]], kbg[sls[c]]], axis=1)) for c in chunks]
    for c in chunks:
        xu_ref[0, sls[c], :] = xs[c][:, :GW]
        xwk_ref[0, sls[c], :] = _bf(xs[c][:, GW:])


def _gdn2_kernel(xu_ref, xwk_ref, qkd_ref, qg_ref, kdec_ref, egl_ref, z_ref, s0_ref, ng_ref, bd_ref,
                 o_ref, sfin_ref, s_ref, *, bb, nchunk):
    ti = pl.program_id(1)

    @pl.when(ti == 0)
    def _():
        s_ref[...] = s0_ref[...]

    rb = lax.broadcasted_iota(jnp.int32, (GW, GW), 0) >> 6
    bdm = rb == (lax.broadcasted_iota(jnp.int32, (GW, GW), 1) >> 6)
    bd = bd_ref[...]
    ng = ng_ref[...]

    def chunk(c, carry):
        r0 = pl.multiple_of(c * CHUNK, CHUNK)
        rows = pl.ds(r0, CHUNK)
        bs = range(bb)
        s_old = [s_ref[b] for b in bs]
        r = [jnp.dot(jnp.concatenate([xwk_ref[b, rows, :], qg_ref[b, rows, :]], axis=0), _bf(s_old[b]),
                     preferred_element_type=F32) for b in bs]
        vnew = [xu_ref[b, rows, :] - r[b][:CHUNK] for b in bs]
        o = [r[b][CHUNK:] + jnp.dot(qkd_ref[b, rows, :], _bf(jnp.where(bdm, _tile4(vnew[b]), 0.0)),
                                    preferred_element_type=F32) for b in bs]
        upd = [_dot(kdec_ref[b, rows, :].astype(F32).T, vnew[b]) for b in bs]
        for b in bs:
            s_ref[b] = egl_ref[b, c, 0:1, :] * s_old[b] + jnp.where(bdm, upd[b], 0.0)
            o_ref[b, rows, :] = _bf(_head_rms(o[b], bd, ng) * _silu(z_ref[b, rows, :]))
        return carry

    lax.fori_loop(0, nchunk, chunk, 0)
    sfin_ref[...] = s_ref[...]


def _gdn_consts(tg):
    i = np.arange(tg)
    same = i[:, None] // CHUNK == i[None, :] // CHUNK
    lblk = jnp.asarray(same & (i[:, None] >= i[None, :]), BF16)
    eb = np.zeros((128, GW), np.float32)
    eg = np.zeros((128, GW), np.float32)
    for h in range(N_HEADS):
        eb[h, h * HEAD_DIM:(h + 1) * HEAD_DIM] = 1
        eg[N_HEADS + h, h * HEAD_DIM:(h + 1) * HEAD_DIM] = 1
    return lblk, jnp.asarray(eb, BF16), jnp.asarray(eg, BF16)


def _gdn_call(qkv, small, z, st8, s0bd, cw, alog, dtb, ng, bd, tg, t_valid, bb):
    b, t, _ = qkv.shape
    n_tiles = t // tg
    lblk, eb, eg = _gdn_consts(tg)
    full = lambda a: pl.BlockSpec(a.shape, lambda bi, ti: (0,) * a.ndim)
    row = lambda n: pl.BlockSpec((1, tg, n), lambda bi, ti: (bi, ti, 0))
    nct = tg // CHUNK
    xu, xwk, qkd, qg, kdec, egl, conv_new = pl.pallas_call(
        functools.partial(_gdn1_kernel, tg=tg, t_valid=t_valid, n_tiles=n_tiles),
        grid=(b, n_tiles),
        in_specs=[row(3 * GW), row(128),
                  pl.BlockSpec((1, 8, 3 * GW), lambda bi, ti: (bi, 0, 0)),
                  full(cw), full(alog), full(dtb), full(lblk), full(eb), full(eg), full(bd)],
        out_specs=[row(GW), row(GW), row(GW), row(GW), row(GW),
                   pl.BlockSpec((1, nct, 8, GW), lambda bi, ti: (bi, ti, 0, 0)),
                   pl.BlockSpec((1, 3, 3 * GW), lambda bi, ti: (bi, 0, 0))],
        out_shape=[jax.ShapeDtypeStruct((b, t, GW), F32), jax.ShapeDtypeStruct((b, t, GW), BF16),
                   jax.ShapeDtypeStruct((b, t, GW), BF16), jax.ShapeDtypeStruct((b, t, GW), BF16),
                   jax.ShapeDtypeStruct((b, t, GW), BF16),
                   jax.ShapeDtypeStruct((b, t // CHUNK, 8, GW), F32),
                   jax.ShapeDtypeStruct((b, 3, 3 * GW), F32)],
        scratch_shapes=[pltpu.VMEM((8 + tg, 3 * GW), F32)],
        compiler_params=_params(("arbitrary", "arbitrary")),
        name="gdn_chunk_solve",
    )(qkv, small, st8, cw, alog, dtb, lblk, eb, eg, bd)

    full2 = lambda a: pl.BlockSpec(a.shape, lambda bi, ti: (0,) * a.ndim)
    rowb = pl.BlockSpec((bb, tg, GW), lambda bi, ti: (bi, ti, 0))
    sspec = pl.BlockSpec((bb, GW, GW), lambda bi, ti: (bi, 0, 0))
    o, sfin = pl.pallas_call(
        functools.partial(_gdn2_kernel, bb=bb, nchunk=nct),
        grid=(b // bb, n_tiles),
        in_specs=[rowb, rowb, rowb, rowb, rowb,
                  pl.BlockSpec((bb, nct, 8, GW), lambda bi, ti: (bi, ti, 0, 0)),
                  rowb, sspec, full2(ng), full2(bd)],
        out_specs=[rowb, sspec],
        out_shape=[jax.ShapeDtypeStruct((b, t, GW), BF16), jax.ShapeDtypeStruct((b, GW, GW), F32)],
        scratch_shapes=[pltpu.VMEM((bb, GW, GW), F32)],
        compiler_params=_params(("arbitrary", "arbitrary")),
        name="gdn_recurrence",
    )(xu, xwk, qkd, qg, kdec, egl, z, s0bd, ng, bd)
    return o, conv_new, sfin


def _out_kernel(x_ref, oa_ref, ob_ref, oc_ref, od_ref, ga1_ref, sc2_ref, sh2_ref, ga2_ref, g2_ref,
                wo_ref, wup_ref, cw_ref, wdn_ref, st_ref, y_ref, cnew_ref,
                ubuf_ref, car_ref, gs_ref, *, tm, tiles_per_seq, nff):
    i = pl.program_id(0)

    @pl.when(i % tiles_per_seq == 0)
    def _():
        car_ref[...] = st_ref[0]

    mix = jnp.dot(oa_ref[...], wo_ref[0:GW, :], preferred_element_type=F32)
    for n, ref in enumerate((ob_ref, oc_ref, od_ref)):
        mix = mix + jnp.dot(ref[...], wo_ref[(n + 1) * GW:(n + 2) * GW, :], preferred_element_type=F32)
    x1 = x_ref[...] + ga1_ref[0] * mix
    ms = jnp.mean(x1 * x1, axis=-1, keepdims=True)
    h = x1 * lax.rsqrt(ms + EPS) * g2_ref[...]
    h = _bf(h * (1.0 + sc2_ref[0]) + sh2_ref[0])
    dff = nff * GW
    for j in range(nff):
        halves = []
        for off in (0, dff):
            cols = slice(off + j * GW, off + (j + 1) * GW)
            u = jnp.dot(h, wup_ref[:, cols], preferred_element_type=F32)
            ubuf_ref[0:8, :] = car_ref[:, cols]
            ubuf_ref[8:8 + tm, :] = u
            halves.append(cw_ref[0:1, cols] * ubuf_ref[6:6 + tm, :] + cw_ref[1:2, cols] * ubuf_ref[7:7 + tm, :]
                          + cw_ref[2:3, cols] * u)
            car_ref[:, cols] = ubuf_ref[tm:tm + 8, :]
        gs_ref[:, j * GW:(j + 1) * GW] = _bf(_silu(halves[0]) * halves[1])
    y_ref[...] = x1 + ga2_ref[0] * jnp.dot(gs_ref[...], wdn_ref[...], preferred_element_type=F32)
    cnew_ref[0] = car_ref[...]


def _out_call(x2, oa, ob, oc, od, ga1, sc2, sh2, ga2, g2, wo, wup, cw, wdn, layer, st8, tm, rows_per_seq):
    m, d = x2.shape
    dff2 = wup.shape[2]
    nff = dff2 // (2 * GW)
    tiles_per_seq = rows_per_seq // tm
    nseq = m // rows_per_seq
    mods = (ga1, sc2, sh2, ga2)
    mod_specs = [_mod_spec(mo, tiles_per_seq, d) for mo in mods]
    full = lambda a: pl.BlockSpec(a.shape, lambda i: (0,) * a.ndim)
    row = lambda n: pl.BlockSpec((tm, n), lambda i: (i, 0))
    stspec = pl.BlockSpec((1, 8, dff2), lambda i: (i // tiles_per_seq, 0, 0))
    lspec = lambda a: pl.BlockSpec((None,) + a.shape[1:], lambda i: (layer, 0, 0))
    return pl.pallas_call(
        functools.partial(_out_kernel, tm=tm, tiles_per_seq=tiles_per_seq, nff=nff),
        grid=(m // tm,),
        in_specs=[row(d), row(GW), row(GW), row(GW), row(GW)] + mod_specs + [
            full(g2), lspec(wo), lspec(wup), full(cw), lspec(wdn), stspec],
        out_specs=[row(d), stspec],
        out_shape=[jax.ShapeDtypeStruct((m, d), F32), jax.ShapeDtypeStruct((nseq, 8, dff2), F32)],
        scratch_shapes=[pltpu.VMEM((8 + tm, GW), F32), pltpu.VMEM((8, dff2), F32),
                        pltpu.VMEM((tm, dff2 // 2), BF16)],
        compiler_params=_params(("arbitrary",)),
        name="out_projection_mlp",
    )(x2, oa, ob, oc, od, *(mo[0] for mo in mods), g2, wo, wup, cw, wdn, st8)


def _pack_in_weights(w_in):
    a = 4 * GW + 2 * N_HEADS
    off_b = a
    off_c = off_b + 3 * GW
    off_d = off_c + 3 * GW + N_HEADS
    dd = w_in.shape[:2]
    zeros = lambda n: jnp.zeros(dd + (n,), w_in.dtype)
    small = jnp.concatenate([w_in[..., 4 * GW:a], zeros(128 - 2 * N_HEADS),
                             w_in[..., off_c + 3 * GW:off_d], zeros(128 - N_HEADS)], axis=-1)
    packed = jnp.concatenate([w_in[..., :4 * GW], w_in[..., off_b:off_b + 3 * GW],
                              w_in[..., off_c:off_c + 3 * GW], w_in[..., off_d:off_d + 3 * GW], small],
                             axis=-1)
    return _bf(packed)


def _lane_vec(v, offset):
    depth = v.shape[0]
    out = jnp.zeros((depth, 1, 128), F32)
    return out.at[:, 0, offset:offset + N_HEADS].set(v.astype(F32))


def _band_ext(rel_table, lq, hist):
    c = np.arange(BAND_EXT)
    rel = np.clip(hist + lq - 1 - c, -REL_CLIP, REL_CLIP) + REL_CLIP
    return rel_table.astype(F32)[:, rel]


def _layer(x, mods_in, mods_out, wts, cache, cfg, stack=None):
    b, t, d = x.shape
    m = b * t
    bd = wts["bd"]
    sc1, sh1 = mods_in
    outs = _in_call(x.reshape(m, d), sc1, sh1, wts["norm_mix_g"], wts["w_in"], wts["layer"], wts["qk_gains"],
                    wts["fox_b_f"], bd, cfg["tm_in"], cfg["rows_per_mod_in"], stack)
    (gqkv, gz, small, sbq, sbk, sbv, fq, fk, fv, logf, bq, bk, bv), kv_t = outs[:13], tuple(outs[13:])
    r3 = lambda a: a.reshape(b, t, a.shape[-1])
    new = {}
    if stack is None:
        new.update({"sb_k": r3(sbk), "sb_v": r3(sbv), "fox_k": r3(fk), "fox_v": r3(fv), "fox_logf": r3(logf)})
    past = cache is not None

    tp = cfg["t_gdn"]
    pad_t = lambda a: jnp.pad(r3(a), ((0, 0), (0, tp - t), (0, 0)))
    if past:
        st8 = jnp.pad(cache["gdn_conv"], ((0, 0), (5, 0), (0, 0)))
        s0 = cache["gdn_state"]
        eye = jnp.eye(N_HEADS, dtype=F32)
        s0bd = jnp.einsum("bhkv,hg->bhkgv", s0, eye).reshape(b, GW, GW)
    else:
        st8 = jnp.zeros((b, 8, 3 * GW), F32)
        s0bd = jnp.zeros((b, GW, GW), F32)
    oa, conv_new, sfin = _gdn_call(pad_t(gqkv), pad_t(small), pad_t(gz), st8, s0bd,
                                   wts["gdn_conv_w"], wts["gdn_a_log"], wts["gdn_dt_bias"],
                                   wts["gdn_norm_g"], bd, cfg["tg"], t, cfg["bb"])
    oa = oa[:, :t].reshape(m, GW)
    sfin = sfin.reshape(b, N_HEADS, HEAD_DIM, N_HEADS, HEAD_DIM)
    new["gdn_state"] = jnp.stack([sfin[:, h, :, h, :] for h in range(N_HEADS)], axis=1)
    new["gdn_conv"] = conv_new

    tq, tk, tkf = cfg["tq"], cfg["tk"], cfg["tk_fox"]
    if past:
        pos0 = cache["sb_k"].shape[1]
        tk_all = -(-(pos0 + t) // tk) * tk
        cat = lambda c, n: jnp.pad(jnp.concatenate([c.reshape(b, pos0, -1), r3(n)], axis=1),
                                   ((0, 0), (0, tk_all - pos0 - t), (0, 0)))
        ksb, vsb = cat(cache["sb_k"], sbk), cat(cache["sb_v"], sbv)
        kfx, vfx = cat(cache["fox_k"], fk), cat(cache["fox_v"], fv)
        lf_all = cat(cache["fox_logf"], logf)
    else:
        pos0, tk_all = 0, t
        ksb, vsb, kfx, vfx = r3(sbk), r3(sbv), r3(fk), r3(fv)
    ob = _sb_call(r3(sbq), ksb, vsb, wts["merge_g"][0:1], bd, tq, tk, pos0)
    lf_t = jnp.swapaxes(lf_all, 1, 2) if past or stack is None else kv_t[6][stack[1]]
    fcum = _cumsum_call(lf_t.reshape(b * N_HEADS, tk_all))
    fcum = fcum.reshape(b, N_HEADS, tk_all)
    f_q = jnp.swapaxes(fcum[:, :, pos0:pos0 + t], 1, 2)
    f_k = jnp.swapaxes(fcum.reshape(b, N_HEADS, tk_all // tkf, tkf), 1, 2)
    oc = _fox_call(r3(fq), kfx, vfx, f_q, f_k, wts["merge_g"][1:2], bd, tq, tkf, pos0)

    if past:
        hk, hv = cache["band_k"].reshape(b, -1, GW), cache["band_v"].reshape(b, -1, GW)
        keep = hk.shape[1]
        new["band_k"] = jnp.concatenate([hk, r3(bk)], axis=1)[:, -keep:]
        new["band_v"] = jnp.concatenate([hv, r3(bv)], axis=1)[:, -keep:]
    else:
        hk = hv = jnp.zeros((b, BAND_ROWS, GW), F32)
        if stack is None:
            keep = min(BAND_ROWS, t)
            new["band_k"], new["band_v"] = r3(bk)[:, -keep:], r3(bv)[:, -keep:]
    od = _band_call(r3(bq), r3(bk), r3(bv), hk, hv, wts["band_ext_" + cfg["name"]],
                    wts["merge_g"][2:3], bd, cfg["lq"], cfg["nch"], past)

    if past:
        st_ffn = jnp.pad(cache["ffn_conv"], ((0, 0), (6, 0), (0, 0)))
    else:
        st_ffn = jnp.zeros((b, 8, wts["w_up"].shape[2]), F32)
    ga1, sc2, sh2, ga2 = mods_out
    y, cnew = _out_call(x.reshape(m, d), oa, ob.reshape(m, GW), oc.reshape(m, GW), od.reshape(m, GW),
                        ga1, sc2, sh2, ga2, wts["norm_ffn_g"], wts["w_o"], wts["w_up"],
                        wts["ffn_conv_w"], wts["w_down"], wts["layer"], st_ffn, cfg["tm_out"], t)
    new["ffn_conv"] = cnew[:, 6:8]
    hd = lambda a: a.reshape(a.shape[0], a.shape[1], N_HEADS, HEAD_DIM)
    for n in ("sb_k", "sb_v", "fox_k", "fox_v", "band_k", "band_v"):
        if n in new:
            new[n] = hd(new[n])
    return y.reshape(b, t, d), new, kv_t


STATE_KEYS = ("gdn_conv", "gdn_state", "sb_k", "sb_v", "fox_k", "fox_v", "fox_logf",
              "band_k", "band_v", "ffn_conv")


def _gdn_batch(b):
    return next(n for n in (8, 4, 2, 1) if b % n == 0)


def _group_cfg(name, b, t):
    if t % 512 == 0:
        return dict(name=name, tm_in=512, rows_per_mod_in=t, tm_out=512, t_gdn=t, tg=512,
                    bb=_gdn_batch(b), tq=256, tk=256, tk_fox=256, lq=CHUNK, nch=4)
    assert t % 16 == 0 and t <= CHUNK
    return dict(name=name, tm_in=b * t, rows_per_mod_in=b * t, tm_out=t, t_gdn=CHUNK, tg=CHUNK,
                bb=_gdn_batch(b), tq=t, tk=256, tk_fox=256, lq=t, nch=1)


def kernel(x_prompt, x_sample, c_prompt, c_sample, state_gdn_conv, state_gdn, cache_sb_k, cache_sb_v, cache_fox_k, cache_fox_v, cache_fox_logf, cache_band_k, cache_band_v, state_ffn_conv, ada_w, ada_b, norm_mix_g, w_in, gdn_conv_w, gdn_a_log, gdn_dt_bias, gdn_norm_g, fox_q_g, fox_k_g, fox_b_f, band_q_g, band_k_g, band_rel_bias, merge_g, w_o, norm_ffn_g, w_up, ffn_conv_w, w_down):
    depth = ada_w.shape[0]
    bp, tp, d = x_prompt.shape
    bs, ts, _ = x_sample.shape
    cfg_p = _group_cfg("p", bp, tp)
    cfg_s = _group_cfg("s", bs, ts)
    hist_s = cache_band_k.shape[2]

    mod = _ada_call(jnp.concatenate([c_prompt, c_sample], axis=0), ada_w, ada_b)
    nb = bp + bs
    mod = mod.reshape(depth, nb, 6, d)
    mod_rows = mod.reshape(depth * nb * 6, 1, d)

    tile_h = lambda g: jnp.tile(g.astype(F32), (1, N_HEADS))[:, None, :]
    w_in_p = _pack_in_weights(w_in)
    w_o_b, w_up_b, w_dn_b = _bf(w_o), _bf(w_up), _bf(w_down)
    ffn_cw = jnp.pad(ffn_conv_w.astype(F32), ((0, 0), (0, 5), (0, 0)))
    gdn_cw = jnp.pad(gdn_conv_w.astype(F32), ((0, 0), (0, 4), (0, 0)))
    qk_gains = jnp.concatenate([tile_h(fox_q_g), tile_h(fox_k_g), tile_h(band_q_g), tile_h(band_k_g),
                                jnp.zeros((depth, 4, GW), F32)], axis=1)
    bd = _const_bd256()

    y_p, y_s = x_prompt, x_sample
    kv_p = None
    new_p = {n: [] for n in STATE_KEYS}
    new_s = {n: [] for n in STATE_KEYS}
    for l in range(depth):
        wts = {
            "bd": bd, "layer": l, "norm_mix_g": norm_mix_g[l][None].astype(F32), "w_in": w_in_p,
            "qk_gains": qk_gains[l], "fox_b_f": _lane_vec(fox_b_f, 0)[l],
            "gdn_conv_w": gdn_cw[l], "gdn_a_log": _lane_vec(gdn_a_log, N_HEADS)[l],
            "gdn_dt_bias": _lane_vec(gdn_dt_bias, N_HEADS)[l], "gdn_norm_g": tile_h(gdn_norm_g)[l],
            "merge_g": merge_g[l].reshape(3, GW).astype(F32),
            "band_ext_p": _band_ext(band_rel_bias[l], cfg_p["lq"], BAND_ROWS),
            "band_ext_s": _band_ext(band_rel_bias[l], cfg_s["lq"], hist_s),
            "w_o": w_o_b, "norm_ffn_g": norm_ffn_g[l][None].astype(F32), "w_up": w_up_b,
            "ffn_conv_w": ffn_cw[l], "w_down": w_dn_b,
        }
        pm = lambda c: (mod_rows, l * nb * 6 + c, 6)
        sm1 = lambda c: (mod_rows, (l * nb + bp) * 6 + c, 6)
        srow = lambda c: (jnp.repeat(mod[l, bp:, c], ts, axis=0)[None], 0, 0)
        mods_in_p = (pm(1), pm(0))
        mods_out_p = (pm(2), pm(4), pm(3), pm(5))
        mods_in_s = (srow(1), srow(0))
        mods_out_s = (sm1(2), sm1(4), sm1(3), sm1(5))
        cache = {"gdn_conv": state_gdn_conv[l], "gdn_state": state_gdn[l],
                 "sb_k": cache_sb_k[l], "sb_v": cache_sb_v[l], "fox_k": cache_fox_k[l],
                 "fox_v": cache_fox_v[l], "fox_logf": cache_fox_logf[l],
                 "band_k": cache_band_k[l], "band_v": cache_band_v[l], "ffn_conv": state_ffn_conv[l]}
        y_p, st_p, kv_p = _layer(y_p, mods_in_p, mods_out_p, wts, None, cfg_p, (depth, l, tp, kv_p or None))
        y_s, st_s, _ = _layer(y_s, mods_in_s, mods_out_s, wts, cache, cfg_s)
        for n in STATE_KEYS:
            if n in st_p:
                new_p[n].append(st_p[n])
            new_s[n].append(st_s[n])
    out_p = {n: jnp.stack(v) for n, v in new_p.items() if v}
    for n, a in zip(("sb_k", "sb_v", "fox_k", "fox_v", "band_k", "band_v"), kv_p):
        out_p[n] = jnp.transpose(a.reshape(depth, bp, N_HEADS, HEAD_DIM, a.shape[-1]), (0, 1, 4, 2, 3))
    out_p["fox_logf"] = jnp.swapaxes(kv_p[6], 2, 3)
    return ((y_p, y_s) + tuple(out_p[n] for n in STATE_KEYS)
            + tuple(jnp.stack(new_s[n]) for n in STATE_KEYS))
```
